```python
import jax, jax.numpy as jnp
from jax import lax
import numpy as np

D_MODEL = 1024
BATCH = 4
SEQ = 4096
DEPTH = 1

GRID_W = 64
CTX_LEN = 256
N_HEADS = 8
QK_NOPE_DIM = 64
QK_ROPE_DIM = 32
V_HEAD_DIM = 64
QK_DIM = QK_NOPE_DIM + QK_ROPE_DIM
Q_LORA_RANK = 384
KV_LORA_RANK = 256
ROPE_BASE = 10000.0
Q_BLOCK = 128
LRU_WIDTH = 1280
LRU_BLOCKS = 10
LRU_BLOCK_W = LRU_WIDTH // LRU_BLOCKS
LRU_CONV_W = 4
LRU_C = 8.0
FFN_DIM = 2816
FFN_CONV_W = 3
EPS = 1e-6
OFF_KV = Q_LORA_RANK
OFF_KR = OFF_KV + KV_LORA_RANK
OFF_XB = OFF_KR + QK_ROPE_DIM
OFF_YB = OFF_XB + LRU_WIDTH
OFF_G = OFF_YB + LRU_WIDTH
IN_DIM = OFF_G + 2 * D_MODEL

kernel_name = "hybrid_mla_rglru_convffn_dit_block"


def rms_norm(x, g):
    xf = x.astype(jnp.float32)
    y = xf * lax.rsqrt(jnp.mean(xf * xf, axis=-1, keepdims=True) + EPS)
    return (y * g.astype(jnp.float32)).astype(x.dtype)


def modulate(h, shift, scale):
    return h * (1 + scale) + shift


def dwconv(x, w, b, left, right):
    t = x.shape[1]
    xp = jnp.pad(x, ((0, 0), (left, right), (0, 0)))
    out = b
    for k in range(w.shape[0]):
        out = out + xp[:, k:k + t] * w[k]
    return out


def axial_rope_tables(n):
    rows = n // GRID_W
    row_ids = jnp.repeat(jnp.arange(rows), GRID_W).astype(jnp.float32)
    col_ids = jnp.tile(jnp.arange(GRID_W), rows).astype(jnp.float32)
    axis_dim = QK_ROPE_DIM // 2
    inv = 1.0 / (ROPE_BASE ** (jnp.arange(0, axis_dim, 2, dtype=jnp.float32) / axis_dim))
    ang = jnp.concatenate([row_ids[:, None] * inv, col_ids[:, None] * inv], axis=-1)
    return jnp.cos(ang), jnp.sin(ang)


def apply_rope(x, cos, sin):
    half = QK_ROPE_DIM // 2
    cos = cos.astype(x.dtype)
    sin = sin.astype(x.dtype)
    x1, x2 = x[..., :half], x[..., half:]
    return jnp.concatenate([x1 * cos - x2 * sin, x2 * cos + x1 * sin], axis=-1)


def sdpa(q, k, v):
    s = jnp.einsum('bqhd,bkhd->bhqk', q, k).astype(jnp.float32) * (QK_DIM ** -0.5)
    p = jax.nn.softmax(s, axis=-1).astype(v.dtype)
    return jnp.einsum('bhqk,bkhd->bqhd', p, v)


def attend_blocks(q, k, v):
    b, s, h, dq = q.shape
    nb = s // Q_BLOCK
    qb = q.reshape(b, nb, Q_BLOCK, h, dq).transpose(1, 0, 2, 3, 4)
    ob = lax.map(lambda qq: sdpa(qq, k, v), qb)
    return ob.transpose(1, 0, 2, 3, 4).reshape(b, s, h * V_HEAD_DIM)


def rglru(x, w_a, b_a, w_x, b_x, lam, h0, reverse):
    b, t, w = x.shape
    xb = x.reshape(b, t, LRU_BLOCKS, LRU_BLOCK_W)
    r = jax.nn.sigmoid(jnp.einsum('btnd,nde->btne', xb, w_a).reshape(b, t, w) + b_a)
    i = jax.nn.sigmoid(jnp.einsum('btnd,nde->btne', xb, w_x).reshape(b, t, w) + b_x)
    log_a = -LRU_C * r.astype(jnp.float32) * jax.nn.softplus(-lam.astype(jnp.float32))
    a = jnp.exp(log_a)
    mult = jnp.sqrt(-jnp.expm1(2.0 * log_a))
    u = mult * (i * x).astype(jnp.float32)
    if reverse:
        u = u.at[:, -1].add(a[:, -1] * h0)
    else:
        u = u.at[:, 0].add(a[:, 0] * h0)

    def combine(e1, e2):
        a1, b1 = e1
        a2, b2 = e2
        return a1 * a2, a2 * b1 + b2

    _, h = lax.associative_scan(combine, (a, u), reverse=reverse, axis=1)
    return h


def lru_bidir(xc, lp, h0f, h0b):
    hf = rglru(xc, lp['lru_w_a'][0], lp['lru_b_a'][0], lp['lru_w_x'][0], lp['lru_b_x'][0],
               lp['lru_lambda'][0], h0f, reverse=False)
    hb = rglru(xc, lp['lru_w_a'][1], lp['lru_b_a'][1], lp['lru_w_x'][1], lp['lru_b_x'][1],
               lp['lru_lambda'][1], h0b, reverse=True)
    return hf, hb


def mixer_inputs(h, lp, cos, sin):
    b, t, _ = h.shape
    z = h @ lp['w_in']
    q_lat, kv_lat, k_rope, xb, yb, gl = jnp.split(z, (OFF_KV, OFF_KR, OFF_XB, OFF_YB, OFF_G), axis=-1)
    q = (rms_norm(q_lat, lp['q_norm_g']) @ lp['w_uq']).reshape(b, t, N_HEADS, QK_DIM)
    kv = (rms_norm(kv_lat, lp['kv_norm_g']) @ lp['w_ukv']).reshape(b, t, N_HEADS, QK_NOPE_DIM + V_HEAD_DIM)
    k_nope, v = kv[..., :QK_NOPE_DIM], kv[..., QK_NOPE_DIM:]
    if cos is not None:
        q = jnp.concatenate([q[..., :QK_NOPE_DIM], apply_rope(q[..., QK_NOPE_DIM:], cos[:, None, :], sin[:, None, :])], axis=-1)
        k_rope = apply_rope(k_rope, cos, sin)
    k = jnp.concatenate([k_nope, jnp.broadcast_to(k_rope[:, :, None, :], (b, t, N_HEADS, QK_ROPE_DIM))], axis=-1)
    xc = dwconv(xb, lp['lru_conv_w'], lp['lru_conv_b'], LRU_CONV_W // 2, LRU_CONV_W - 1 - LRU_CONV_W // 2)
    return q, k, v, xc, yb, gl


def merge_out(attn, hf, hb, yb, gl, lp):
    y_a = attn @ lp['w_o_attn']
    y_b = (((hf + hb).astype(yb.dtype)) * jax.nn.gelu(yb)) @ lp['w_o_lru']
    g_a, g_b = jnp.split(jax.nn.sigmoid(gl + lp['b_gate']), 2, axis=-1)
    return (g_a * y_a + g_b * y_b) @ lp['w_out']


def conv_ffn(h, lp):
    u = h @ lp['w_up']
    a, g = jnp.split(u, 2, axis=-1)
    a = dwconv(a, lp['ffn_conv_w'], lp['ffn_conv_b'], FFN_CONV_W // 2, FFN_CONV_W // 2)
    return (jax.nn.silu(a) * g) @ lp['w_down']


def setup_inputs(seed: int = 0) -> dict:
    key = jax.random.key(seed)
    ks = jax.random.split(key, 32)
    f32 = jnp.float32

    def nrm(k, shape, fan_in):
        return jax.random.normal(k, shape, f32) * (fan_in ** -0.5)

    def gain(k, shape):
        return 1.0 + 0.05 * jax.random.normal(k, shape, f32)

    def bias(k, shape):
        return 0.02 * jax.random.normal(k, shape, f32)

    a0 = jax.random.uniform(ks[20], (DEPTH, 2, LRU_WIDTH), f32, 0.9, 0.999)
    return {
        'x': jax.random.normal(ks[0], (BATCH, SEQ, D_MODEL), f32),
        'c': jax.random.normal(ks[1], (BATCH, D_MODEL), f32),
        'ctx': jax.random.normal(ks[2], (BATCH, CTX_LEN, D_MODEL), f32),
        'c_ctx': jax.random.normal(ks[3], (D_MODEL,), f32),
        'w_mod': nrm(ks[4], (DEPTH, D_MODEL, 6 * D_MODEL), D_MODEL),
        'b_mod': bias(ks[5], (DEPTH, 6 * D_MODEL)),
        'norm1_g': gain(ks[6], (DEPTH, D_MODEL)),
        'w_in': nrm(ks[7], (DEPTH, D_MODEL, IN_DIM), D_MODEL),
        'b_gate': bias(ks[8], (DEPTH, 2 * D_MODEL)),
        'q_norm_g': gain(ks[9], (DEPTH, Q_LORA_RANK)),
        'kv_norm_g': gain(ks[10], (DEPTH, KV_LORA_RANK)),
        'w_uq': nrm(ks[11], (DEPTH, Q_LORA_RANK, N_HEADS * QK_DIM), Q_LORA_RANK),
        'w_ukv': nrm(ks[12], (DEPTH, KV_LORA_RANK, N_HEADS * (QK_NOPE_DIM + V_HEAD_DIM)), KV_LORA_RANK),
        'w_o_attn': nrm(ks[13], (DEPTH, N_HEADS * V_HEAD_DIM, D_MODEL), N_HEADS * V_HEAD_DIM),
        'lru_conv_w': nrm(ks[14], (DEPTH, LRU_CONV_W, LRU_WIDTH), LRU_CONV_W),
        'lru_conv_b': bias(ks[15], (DEPTH, LRU_WIDTH)),
        'lru_w_a': nrm(ks[16], (DEPTH, 2, LRU_BLOCKS, LRU_BLOCK_W, LRU_BLOCK_W), LRU_BLOCK_W),
        'lru_b_a': bias(ks[17], (DEPTH, 2, LRU_WIDTH)),
        'lru_w_x': nrm(ks[18], (DEPTH, 2, LRU_BLOCKS, LRU_BLOCK_W, LRU_BLOCK_W), LRU_BLOCK_W),
        'lru_b_x': bias(ks[19], (DEPTH, 2, LRU_WIDTH)),
        'lru_lambda': jnp.log(a0 / (1.0 - a0)),
        'w_o_lru': nrm(ks[21], (DEPTH, LRU_WIDTH, D_MODEL), LRU_WIDTH),
        'w_out': nrm(ks[22], (DEPTH, D_MODEL, D_MODEL), D_MODEL),
        'norm2_g': gain(ks[23], (DEPTH, D_MODEL)),
        'w_up': nrm(ks[24], (DEPTH, D_MODEL, 2 * FFN_DIM), D_MODEL),
        'ffn_conv_w': nrm(ks[25], (DEPTH, FFN_CONV_W, FFN_DIM), FFN_CONV_W),
        'ffn_conv_b': bias(ks[26], (DEPTH, FFN_DIM)),
        'w_down': nrm(ks[27], (DEPTH, FFN_DIM, D_MODEL), FFN_DIM),
        'final_g': gain(ks[28], (D_MODEL,)),
    }


def reference(x, c, ctx, c_ctx, w_mod, b_mod, norm1_g, w_in, b_gate, q_norm_g, kv_norm_g,
              w_uq, w_ukv, w_o_attn, lru_conv_w, lru_conv_b, lru_w_a, lru_b_a, lru_w_x,
              lru_b_x, lru_lambda, w_o_lru, w_out, norm2_g, w_up, ffn_conv_w, ffn_conv_b,
              w_down, final_g):
    b, s, _ = x.shape
    cos, sin = axial_rope_tables(s)
    h_zero = jnp.zeros((b, LRU_WIDTH), jnp.float32)
    for i in range(DEPTH):
        lp = {
            'w_in': w_in[i], 'b_gate': b_gate[i], 'q_norm_g': q_norm_g[i], 'kv_norm_g': kv_norm_g[i],
            'w_uq': w_uq[i], 'w_ukv': w_ukv[i], 'w_o_attn': w_o_attn[i],
            'lru_conv_w': lru_conv_w[i], 'lru_conv_b': lru_conv_b[i],
            'lru_w_a': lru_w_a[i], 'lru_b_a': lru_b_a[i], 'lru_w_x': lru_w_x[i], 'lru_b_x': lru_b_x[i],
            'lru_lambda': lru_lambda[i], 'w_o_lru': w_o_lru[i], 'w_out': w_out[i],
            'w_up': w_up[i], 'ffn_conv_w': ffn_conv_w[i], 'ffn_conv_b': ffn_conv_b[i], 'w_down': w_down[i],
        }
        mod_l = (jax.nn.silu(c) @ w_mod[i] + b_mod[i])[:, None, :]
        mod_c = jax.nn.silu(c_ctx) @ w_mod[i] + b_mod[i]
        sh1_l, sc1_l, g1_l, sh2_l, sc2_l, g2_l = jnp.split(mod_l, 6, axis=-1)
        sh1_c, sc1_c, g1_c, sh2_c, sc2_c, g2_c = jnp.split(mod_c, 6, axis=-1)

        hc = modulate(rms_norm(ctx, norm1_g[i]), sh1_c, sc1_c)
        q_c, k_c, v_c, xc_c, yb_c, gl_c = mixer_inputs(hc, lp, None, None)
        hf_c, hb_c = lru_bidir(xc_c, lp, h_zero, h_zero)
        state_f, state_b = hf_c[:, -1], hb_c[:, 0]

        hl = modulate(rms_norm(x, norm1_g[i]), sh1_l, sc1_l)
        q_l, k_l, v_l, xc_l, yb_l, gl_l = mixer_inputs(hl, lp, cos, sin)
        attn_l = attend_blocks(q_l, jnp.concatenate([k_l, k_c], axis=1), jnp.concatenate([v_l, v_c], axis=1))
        hf_l, hb_l = lru_bidir(xc_l, lp, state_f, state_b)
        x = x + g1_l * merge_out(attn_l, hf_l, hb_l, yb_l, gl_l, lp)
        x = x + g2_l * conv_ffn(modulate(rms_norm(x, norm2_g[i]), sh2_l, sc2_l), lp)

        if i < DEPTH - 1:
            attn_c = sdpa(q_c, k_c, v_c).reshape(b, ctx.shape[1], N_HEADS * V_HEAD_DIM)
            ctx = ctx + g1_c * merge_out(attn_c, hf_c, hb_c, yb_c, gl_c, lp)
            ctx = ctx + g2_c * conv_ffn(modulate(rms_norm(ctx, norm2_g[i]), sh2_c, sc2_c), lp)
    return rms_norm(x, final_g)
```

```python
import functools

import jax
import jax.numpy as jnp
from jax import lax
from jax.experimental import pallas as pl
from jax.experimental.pallas import tpu as pltpu

F32 = jnp.float32
BF16 = jnp.bfloat16

D_MODEL = 1024
GRID_W = 64
N_HEADS = 8
QK_NOPE_DIM = 64
QK_ROPE_DIM = 32
ROPE_HALF = QK_ROPE_DIM // 2
V_HEAD_DIM = 64
QK_DIM = QK_NOPE_DIM + QK_ROPE_DIM
HEAD_PAD = 128
Q_LORA_RANK = 384
KV_LORA_RANK = 256
ROPE_BASE = 10000.0
LRU_WIDTH = 1280
LRU_BLOCKS = 10
LRU_BLOCK_W = LRU_WIDTH // LRU_BLOCKS
LRU_CONV_W = 4
LRU_C = 8.0
FFN_DIM = 2816
FFN_CONV_W = 3
EPS = 1e-6
OFF_KV = Q_LORA_RANK
OFF_KR = OFF_KV + KV_LORA_RANK
OFF_XB = OFF_KR + QK_ROPE_DIM
OFF_YB = OFF_XB + LRU_WIDTH
OFF_G = OFF_YB + LRU_WIDTH

SUBLANES = 8
LANES = 128
VMEM_LIMIT = 56 * 1024 * 1024

MOD_ROWS = 8
MOD_TN = 768
PROJ_TM = 256
LRU_TC = 64
ATTN_TQ = 256
ATTN_TK = 512
MERGE_TM = 256
FFN_TM = 256
FFN_FC = 256
HALO = SUBLANES


def _dot(a, b):
    return jnp.dot(a, b, preferred_element_type=F32)


def _dot_nt(a, b):
    return lax.dot_general(a, b, (((1,), (1,)), ((), ())), preferred_element_type=F32)


def _rms(x, g):
    return x * lax.rsqrt(jnp.mean(x * x, axis=-1, keepdims=True) + EPS) * g


def _const_spec(shape):
    nd = len(shape)
    return pl.BlockSpec(shape, lambda *_: (0,) * nd, pipeline_mode=pl.Buffered(1))


def _params(sem):
    return pltpu.CompilerParams(dimension_semantics=sem, vmem_limit_bytes=VMEM_LIMIT)


def _mod_body(c_ref, w_ref, b_ref, o_ref):
    c = c_ref[...]
    s = c * jax.nn.sigmoid(c)
    o_ref[...] = _dot(s.astype(BF16), w_ref[...].astype(BF16)) + b_ref[...]


def _mod_call(cc, w_mod, b_mod):
    n = w_mod.shape[-1]
    return pl.pallas_call(
        _mod_body,
        grid=(n // MOD_TN,),
        in_specs=[
            pl.BlockSpec((MOD_ROWS, D_MODEL), lambda j: (0, 0)),
            pl.BlockSpec((None, D_MODEL, MOD_TN), lambda j: (0, 0, j)),
            pl.BlockSpec((1, MOD_TN), lambda j: (0, j)),
        ],
        out_specs=pl.BlockSpec((MOD_ROWS, MOD_TN), lambda j: (0, j)),
        out_shape=jax.ShapeDtypeStruct((MOD_ROWS, n), F32),
        compiler_params=_params(("arbitrary",)),
        name="mod",
    )(cc, w_mod, b_mod)


def _rope128(t, c, s1, s2):
    return (t * c + pltpu.roll(t, HEAD_PAD - ROPE_HALF, 1) * s1
            + pltpu.roll(t, ROPE_HALF, 1) * s2)


def _proj_body(latent, x_ref, sh_ref, sc_ref, g1_ref, wkv_ref, wkr_ref, wxb_ref, gkv_ref,
               wuk_ref, wuv_ref, c_ref, s1_ref, s2_ref, *rest):
    if latent:
        (wq_ref, wyb_ref, wgl_ref, gq_ref, wuq_ref,
         k_ref, v_ref, xb_ref, q_ref, yb_ref, gl_ref) = rest
    else:
        k_ref, v_ref, xb_ref = rest
    x = x_ref[0]
    h = _rms(x, g1_ref[...]) * (1.0 + sc_ref[0]) + sh_ref[0]
    hb = h.astype(BF16)
    c, s1, s2 = c_ref[...], s1_ref[...], s2_ref[...]

    kvn = _rms(_dot(hb, wkv_ref[...]), gkv_ref[...]).astype(BF16)
    kr = _rope128(_dot(hb, wkr_ref[...]), c, s1, s2)
    kk = _dot(kvn, wuk_ref[...])
    for hd in range(N_HEADS):
        sl = slice(hd * HEAD_PAD, (hd + 1) * HEAD_PAD)
        k_ref[0, :, sl] = (kk[:, sl] + kr).astype(BF16)
    v_ref[0] = _dot(kvn, wuv_ref[...]).astype(BF16)
    xb_ref[0] = _dot(hb, wxb_ref[...])
    if latent:
        qn = _rms(_dot(hb, wq_ref[...]), gq_ref[...]).astype(BF16)
        qq = _dot(qn, wuq_ref[...])
        scale = QK_DIM ** -0.5
        for hd in range(N_HEADS):
            sl = slice(hd * HEAD_PAD, (hd + 1) * HEAD_PAD)
            q_ref[0, :, sl] = (_rope128(qq[:, sl], c, s1, s2) * scale).astype(BF16)
        yb_ref[0] = _dot(hb, wyb_ref[...])
        gl_ref[0] = _dot(hb, wgl_ref[...])


def _proj_call(latent, x, sh, sc, g1, w, tabs):
    b, s, _ = x.shape
    tm = min(PROJ_TM, s)
    row = lambda width: pl.BlockSpec((1, tm, width), lambda i, j: (j, i, 0))
    modrow = pl.BlockSpec((1, 1, D_MODEL), lambda i, j: (j, 0, 0))
    tab = pl.BlockSpec((tm, HEAD_PAD), lambda i, j: (i, 0))
    ins = [x, sh, sc, g1, w['kv'], w['kr'], w['xb'], w['gkv'], w['uk'], w['uv'], *tabs]
    in_specs = [row(D_MODEL), modrow, modrow] + [_const_spec(a.shape) for a in ins[3:10]] + [tab] * 3
    widths = [N_HEADS * HEAD_PAD, N_HEADS * V_HEAD_DIM, LRU_WIDTH]
    dtypes = [BF16, BF16, F32]
    if latent:
        extra = [w['q'], w['yb'], w['gl'], w['gq'], w['uq']]
        ins += extra
        in_specs += [_const_spec(a.shape) for a in extra]
        widths += [N_HEADS * HEAD_PAD, LRU_WIDTH, 2 * D_MODEL]
        dtypes += [BF16, F32, F32]
    return pl.pallas_call(
        functools.partial(_proj_body, latent),
        grid=(s // tm, b),
        in_specs=in_specs,
        out_specs=[row(wd) for wd in widths],
        out_shape=[jax.ShapeDtypeStruct((b, s, wd), dt) for wd, dt in zip(widths, dtypes)],
        compiler_params=_params(("parallel", "parallel")),
        name="proj_latent" if latent else "proj_ctx",
    )(*ins)


def _sigmoid(x):
    return jax.nn.sigmoid(x)


def _gelu_tanh(x):
    return 0.5 * x * (1.0 + jnp.tanh(0.7978845608028654 * (x + 0.044715 * (x * x * x))))


def _lru_body(latent, xb_ref, cw_ref, cb_ref, wg_ref, bg_ref, lam_ref, *rest):
    if latent:
        yb_ref, h0_ref, out_ref, xpad, af, uf, ab, ub = rest
    else:
        st_ref, xpad, af, uf, ab, ub = rest
    s = xb_ref.shape[1]
    tc = min(LRU_TC, s)
    nc = s // tc
    ng = tc // SUBLANES

    zero_halo = jnp.zeros((HALO, LANES), F32)
    xpad[0:HALO, :] = zero_halo
    xpad[s + HALO:s + 2 * HALO, :] = zero_halo

    def copy_chunk(ci, carry):
        r0 = pl.multiple_of(ci * tc, tc)
        xpad[pl.ds(r0 + HALO, tc), :] = xb_ref[0, pl.ds(r0, tc), :]
        return carry

    lax.fori_loop(0, nc, copy_chunk, 0)

    cw = cw_ref[...]
    cb = cb_ref[...]
    wg = wg_ref[0]
    bg = bg_ref[0]
    lam = lam_ref[0]
    sp = jnp.maximum(-lam, 0.0) + jnp.log1p(jnp.exp(-jnp.abs(lam)))
    win_rows = tc + 2 * HALO

    def gate_chunk(ci, carry):
        r0 = pl.multiple_of(ci * tc, tc)
        win = xpad[pl.ds(r0, win_rows), :]
        xm2 = pltpu.roll(win, 2, 0)[HALO:HALO + tc]
        xm1 = pltpu.roll(win, 1, 0)[HALO:HALO + tc]
        x0 = win[HALO:HALO + tc]
        xp1 = pltpu.roll(win, win_rows - 1, 0)[HALO:HALO + tc]
        xc = cb + xm2 * cw[0:1] + xm1 * cw[1:2] + x0 * cw[2:3] + xp1 * cw[3:4]
        g = _dot(xc.astype(BF16), wg) + bg
        for d, (a_ref, u_ref) in enumerate(((af, uf), (ab, ub))):
            r = _sigmoid(g[:, (2 * d) * LANES:(2 * d + 1) * LANES])
            i = _sigmoid(g[:, (2 * d + 1) * LANES:(2 * d + 2) * LANES])
            log_a = -LRU_C * r * sp[d:d + 1]
            a = jnp.exp(log_a)
            mult = jnp.sqrt(-jnp.tanh(log_a) * (1.0 + a * a))
            a_ref[pl.ds(r0, tc), :] = a
            u_ref[pl.ds(r0, tc), :] = mult * (i * xc)
        return carry

    lax.fori_loop(0, nc, gate_chunk, 0)

    if latent:
        h0 = h0_ref[0]
        h0f, h0b = h0[0:1], h0[1:2]
    else:
        h0f = h0b = jnp.zeros((1, LANES), F32)

    sub = lax.broadcasted_iota(jnp.int32, (tc, LANES), 0) % SUBLANES
    fmask = [sub >= d for d in (1, 2, 4)]
    bmask = [sub < SUBLANES - d for d in (1, 2, 4)]

    def scan_chunk(ci, carry):
        cf, cbk = carry
        rf = pl.multiple_of(ci * tc, tc)
        rb = pl.multiple_of((nc - 1 - ci) * tc, tc)
        a, u = af[pl.ds(rf, tc), :], uf[pl.ds(rf, tc), :]
        for k, d in enumerate((1, 2, 4)):
            a_s = jnp.where(fmask[k], pltpu.roll(a, d, 0), 1.0)
            u_s = jnp.where(fmask[k], pltpu.roll(u, d, 0), 0.0)
            u = a * u_s + u
            a = a * a_s
        hs = []
        for gi in range(ng):
            lo = gi * SUBLANES
            hs.append(u[lo:lo + SUBLANES] + a[lo:lo + SUBLANES] * cf)
            cf = u[lo + SUBLANES - 1:lo + SUBLANES] + a[lo + SUBLANES - 1:lo + SUBLANES] * cf
        uf[pl.ds(rf, tc), :] = jnp.concatenate(hs, axis=0)

        a, u = ab[pl.ds(rb, tc), :], ub[pl.ds(rb, tc), :]
        for k, d in enumerate((1, 2, 4)):
            a_s = jnp.where(bmask[k], pltpu.roll(a, tc - d, 0), 1.0)
            u_s = jnp.where(bmask[k], pltpu.roll(u, tc - d, 0), 0.0)
            u = a * u_s + u
            a = a * a_s
        hs = [None] * ng
        for gi in reversed(range(ng)):
            lo = gi * SUBLANES
            hs[gi] = u[lo:lo + SUBLANES] + a[lo:lo + SUBLANES] * cbk
            cbk = u[lo:lo + 1] + a[lo:lo + 1] * cbk
        ub[pl.ds(rb, tc), :] = jnp.concatenate(hs, axis=0)
        return cf, cbk

    cf, cbk = lax.fori_loop(0, nc, scan_chunk, (h0f, h0b))

    if latent:
        def out_chunk(ci, carry):
            r0 = pl.multiple_of(ci * tc, tc)
            hsum = uf[pl.ds(r0, tc), :] + ub[pl.ds(r0, tc), :]
            out_ref[0, pl.ds(r0, tc), :] = (hsum * _gelu_tanh(yb_ref[0, pl.ds(r0, tc), :])).astype(BF16)
            return carry

        lax.fori_loop(0, nc, out_chunk, 0)
    else:
        st_ref[0] = jnp.concatenate([cf, cbk], axis=0)


def _lru_call(latent, xb, w, yb=None, h0=None):
    b, s, _ = xb.shape
    seq = pl.BlockSpec((1, s, LANES), lambda i, j: (i, 0, j))
    st = pl.BlockSpec((1, 2, LANES), lambda i, j: (i, 0, j))
    in_specs = [
        seq,
        pl.BlockSpec((LRU_CONV_W, LANES), lambda i, j: (0, j)),
        pl.BlockSpec((1, LANES), lambda i, j: (0, j)),
        pl.BlockSpec((1, LANES, 4 * LANES), lambda i, j: (j, 0, 0)),
        pl.BlockSpec((1, 1, 4 * LANES), lambda i, j: (j, 0, 0)),
        pl.BlockSpec((1, 2, LANES), lambda i, j: (j, 0, 0)),
    ]
    ins = [xb, w['conv_w'], w['conv_b'], w['gate_w'], w['gate_b'], w['lam']]
    if latent:
        ins += [yb, h0]
        in_specs += [seq, st]
        out_specs = seq
        out_shape = jax.ShapeDtypeStruct((b, s, LRU_WIDTH), BF16)
    else:
        out_specs = st
        out_shape = jax.ShapeDtypeStruct((b, 2, LRU_WIDTH), F32)
    scratch = [pltpu.VMEM((s + 2 * HALO, LANES), F32)] + [pltpu.VMEM((s, LANES), F32)] * 4
    return pl.pallas_call(
        functools.partial(_lru_body, latent),
        grid=(b, LRU_BLOCKS),
        in_specs=in_specs,
        out_specs=out_specs,
        out_shape=out_shape,
        scratch_shapes=scratch,
        compiler_params=_params(("parallel", "parallel")),
        name="lru_latent" if latent else "lru_ctx",
    )(*ins)


def _attn_body(q_ref, kl_ref, kc_ref, vl_ref, vc_ref, o_ref, s_ref):
    s_len = kl_ref.shape[1]
    c_len = kc_ref.shape[1]
    tk = min(ATTN_TK, s_len)
    nk = s_len // tk
    tq = q_ref.shape[1]
    outs = []
    for hd in range(2):
        lanes = slice(hd * HEAD_PAD, (hd + 1) * HEAD_PAD)
        q = q_ref[0, :, lanes]
        m = None
        for ki in range(nk):
            sk = _dot_nt(q, kl_ref[0, ki * tk:(ki + 1) * tk, lanes])
            s_ref[ki] = sk
            mk = jnp.max(sk, axis=-1, keepdims=True)
            m = mk if m is None else jnp.maximum(m, mk)
        sc = _dot_nt(q, kc_ref[0, :, lanes])
        m = jnp.maximum(m, jnp.max(sc, axis=-1, keepdims=True))
        pc = jnp.exp(sc - m)
        l = jnp.sum(pc, axis=-1, keepdims=True)
        acc = _dot(pc.astype(BF16), vc_ref[0])
        for ki in range(nk):
            p = jnp.exp(s_ref[ki] - m)
            l = l + jnp.sum(p, axis=-1, keepdims=True)
            acc = acc + _dot(p.astype(BF16), vl_ref[0, ki * tk:(ki + 1) * tk, :])
        outs.append(acc / l)
    lane = lax.broadcasted_iota(jnp.int32, (tq, 2 * V_HEAD_DIM), 1)
    o_ref[0] = jnp.where(lane < V_HEAD_DIM, outs[0], outs[1]).astype(BF16)


def _attn_call(q, k_l, k_c, v_l, v_c):
    b, s, _ = q.shape
    c_len = k_c.shape[1]
    tq = min(ATTN_TQ, s)
    tk = min(ATTN_TK, s)
    pairs = N_HEADS // 2
    return pl.pallas_call(
        _attn_body,
        grid=(b, pairs, s // tq),
        in_specs=[
            pl.BlockSpec((1, tq, 2 * HEAD_PAD), lambda i, j, t: (i, t, j)),
            pl.BlockSpec((1, s, 2 * HEAD_PAD), lambda i, j, t: (i, 0, j)),
            pl.BlockSpec((1, c_len, 2 * HEAD_PAD), lambda i, j, t: (i, 0, j)),
            pl.BlockSpec((1, s, 2 * V_HEAD_DIM), lambda i, j, t: (i, 0, j)),
            pl.BlockSpec((1, c_len, 2 * V_HEAD_DIM), lambda i, j, t: (i, 0, j)),
        ],
        out_specs=pl.BlockSpec((1, tq, 2 * V_HEAD_DIM), lambda i, j, t: (i, t, j)),
        out_shape=jax.ShapeDtypeStruct((b, s, N_HEADS * V_HEAD_DIM), BF16),
        scratch_shapes=[pltpu.VMEM((s // tk, tq, tk), F32)],
        compiler_params=_params(("parallel", "parallel", "parallel")),
        name="attn",
    )(q, k_l, k_c, v_l, v_c)


def _merge_body(attn_ref, m_ref, gl_ref, x_ref, g1_ref, bg_ref, woa_ref, wol_ref, wout_ref, o_ref):
    y_a = _dot(attn_ref[0], woa_ref[...])
    y_b = _dot(m_ref[0], wol_ref[...])
    gates = jax.nn.sigmoid(gl_ref[0] + bg_ref[...])
    mix = gates[:, :D_MODEL] * y_a + gates[:, D_MODEL:] * y_b
    o_ref[0] = x_ref[0] + g1_ref[0] * _dot(mix.astype(BF16), wout_ref[...])


def _merge_call(attn, m, gl, x, g1, w):
    b, s, _ = x.shape
    tm = min(MERGE_TM, s)
    row = lambda width: pl.BlockSpec((1, tm, width), lambda i, j: (i, j, 0))
    modrow = pl.BlockSpec((1, 1, D_MODEL), lambda i, j: (i, 0, 0))
    consts = [w['b_gate'], w['o_attn'], w['o_lru'], w['out']]
    return pl.pallas_call(
        _merge_body,
        grid=(b, s // tm),
        in_specs=[row(N_HEADS * V_HEAD_DIM), row(LRU_WIDTH), row(2 * D_MODEL), row(D_MODEL), modrow]
        + [_const_spec(a.shape) for a in consts],
        out_specs=row(D_MODEL),
        out_shape=jax.ShapeDtypeStruct((b, s, D_MODEL), F32),
        compiler_params=_params(("parallel", "parallel")),
        name="merge",
    )(attn, m, gl, x, g1, *consts)


def _ffn_body(xp_ref, x_ref, xn_ref, sh_ref, sc_ref, g2_ref, n2_ref, fg_ref, wup_ref, cw_ref, cb_ref,
              wdn_ref, o_ref):
    j = pl.program_id(1)
    nj = pl.num_programs(1)
    tm = x_ref.shape[1]
    x = x_ref[0]
    xe = jnp.concatenate([xp_ref[0], x, xn_ref[0]], axis=0)
    hf = _rms(xe, n2_ref[...]) * (1.0 + sc_ref[0]) + sh_ref[0]
    he = hf.astype(BF16)
    ht = hf[HALO:HALO + tm].astype(BF16)
    rows = lax.broadcasted_iota(jnp.int32, (tm + 2 * HALO, 1), 0)
    valid = jnp.logical_and(jnp.logical_or(rows >= HALO, j > 0),
                            jnp.logical_or(rows < HALO + tm, j < nj - 1))
    ext = tm + 2 * HALO
    acc = jnp.zeros((tm, D_MODEL), F32)
    for ci in range(FFN_DIM // FFN_FC):
        cols = slice(ci * FFN_FC, (ci + 1) * FFN_FC)
        gcols = slice(FFN_DIM + ci * FFN_FC, FFN_DIM + (ci + 1) * FFN_FC)
        a = jnp.where(valid, _dot(he, wup_ref[:, cols]), 0.0)
        cw = cw_ref[:, cols]
        conv = (cb_ref[:, cols] + pltpu.roll(a, 1, 0)[HALO:HALO + tm] * cw[0:1]
                + a[HALO:HALO + tm] * cw[1:2]
                + pltpu.roll(a, ext - 1, 0)[HALO:HALO + tm] * cw[2:3])
        g = _dot(ht, wup_ref[:, gcols])
        f = (conv * jax.nn.sigmoid(conv) * g).astype(BF16)
        acc = acc + _dot(f, wdn_ref[cols, :])
    y = x + g2_ref[0] * acc
    o_ref[0] = _rms(y, fg_ref[...])


def _ffn_call(x1, sh2, sc2, g2, w):
    b, s, _ = x1.shape
    tm = min(FFN_TM, s)
    per = tm // HALO
    nblk = s // HALO
    row = pl.BlockSpec((1, tm, D_MODEL), lambda i, j: (i, j, 0))
    prev = pl.BlockSpec((1, HALO, D_MODEL), lambda i, j: (i, jnp.maximum(j * per - 1, 0), 0))
    nxt = pl.BlockSpec((1, HALO, D_MODEL), lambda i, j: (i, jnp.minimum((j + 1) * per, nblk - 1), 0))
    modrow = pl.BlockSpec((1, 1, D_MODEL), lambda i, j: (i, 0, 0))
    consts = [w['norm2_g'], w['final_g'], w['up'], w['ffn_conv_w'], w['ffn_conv_b'], w['down']]
    return pl.pallas_call(
        _ffn_body,
        grid=(b, s // tm),
        in_specs=[prev, row, nxt, modrow, modrow, modrow] + [_const_spec(a.shape) for a in consts],
        out_specs=row,
        out_shape=jax.ShapeDtypeStruct((b, s, D_MODEL), F32),
        compiler_params=_params(("parallel", "parallel")),
        name="ffn",
    )(x1, x1, x1, sh2, sc2, g2, *consts)


def _rope_tables(n):
    rows = n // GRID_W
    row_ids = jnp.repeat(jnp.arange(rows), GRID_W).astype(F32)
    col_ids = jnp.tile(jnp.arange(GRID_W), rows).astype(F32)
    axis_dim = QK_ROPE_DIM // 2
    inv = 1.0 / (ROPE_BASE ** (jnp.arange(0, axis_dim, 2, dtype=F32) / axis_dim))
    ang = jnp.concatenate([row_ids[:, None] * inv, col_ids[:, None] * inv], axis=-1)
    cos, sin = jnp.cos(ang), jnp.sin(ang)
    ones = lambda w_: jnp.ones((n, w_), F32)
    zeros = lambda w_: jnp.zeros((n, w_), F32)
    tail = HEAD_PAD - QK_DIM
    c = jnp.concatenate([ones(QK_NOPE_DIM), cos, cos, ones(tail)], axis=-1)
    s1 = jnp.concatenate([zeros(QK_NOPE_DIM), -sin, zeros(ROPE_HALF + tail)], axis=-1)
    s2 = jnp.concatenate([zeros(QK_NOPE_DIM + ROPE_HALF), sin, zeros(tail)], axis=-1)
    return c, s1, s2


def _identity_tables(n):
    return (jnp.ones((n, HEAD_PAD), F32), jnp.zeros((n, HEAD_PAD), F32), jnp.zeros((n, HEAD_PAD), F32))


def _prep_weights(w_in, q_norm_g, kv_norm_g, w_uq, w_ukv, w_o_attn, lru_conv_w, lru_conv_b, lru_w_a,
                  lru_b_a, lru_w_x, lru_b_x, lru_lambda, w_o_lru, w_out, b_gate, norm2_g, w_up,
                  ffn_conv_w, ffn_conv_b, w_down, final_g):
    w = {}
    w['q'] = w_in[:, :OFF_KV].astype(BF16)
    w['kv'] = w_in[:, OFF_KV:OFF_KR].astype(BF16)
    w['kr'] = jnp.pad(w_in[:, OFF_KR:OFF_XB], ((0, 0), (QK_NOPE_DIM, HEAD_PAD - QK_DIM))).astype(BF16)
    w['xb'] = w_in[:, OFF_XB:OFF_YB].astype(BF16)
    w['yb'] = w_in[:, OFF_YB:OFF_G].astype(BF16)
    w['gl'] = w_in[:, OFF_G:].astype(BF16)
    w['gq'] = q_norm_g[None, :]
    w['gkv'] = kv_norm_g[None, :]
    uq = w_uq.reshape(Q_LORA_RANK, N_HEADS, QK_DIM)
    w['uq'] = jnp.pad(uq, ((0, 0), (0, 0), (0, HEAD_PAD - QK_DIM))).reshape(Q_LORA_RANK, -1).astype(BF16)
    ukv = w_ukv.reshape(KV_LORA_RANK, N_HEADS, QK_NOPE_DIM + V_HEAD_DIM)
    w['uk'] = jnp.pad(ukv[..., :QK_NOPE_DIM],
                      ((0, 0), (0, 0), (0, HEAD_PAD - QK_NOPE_DIM))).reshape(KV_LORA_RANK, -1).astype(BF16)
    w['uv'] = ukv[..., QK_NOPE_DIM:].reshape(KV_LORA_RANK, -1).astype(BF16)
    w['conv_w'] = lru_conv_w
    w['conv_b'] = lru_conv_b[None, :]
    w['gate_w'] = jnp.concatenate([lru_w_a[0], lru_w_x[0], lru_w_a[1], lru_w_x[1]], axis=-1).astype(BF16)
    gb = jnp.stack([lru_b_a[0], lru_b_x[0], lru_b_a[1], lru_b_x[1]], axis=0)
    w['gate_b'] = gb.reshape(4, LRU_BLOCKS, LRU_BLOCK_W).transpose(1, 0, 2).reshape(LRU_BLOCKS, 1, -1)
    w['lam'] = lru_lambda.reshape(2, LRU_BLOCKS, LRU_BLOCK_W).transpose(1, 0, 2)
    w['o_attn'] = w_o_attn.astype(BF16)
    w['o_lru'] = w_o_lru.astype(BF16)
    w['out'] = w_out.astype(BF16)
    w['b_gate'] = b_gate[None, :]
    w['norm2_g'] = norm2_g[None, :]
    w['final_g'] = final_g[None, :]
    w['up'] = w_up.astype(BF16)
    w['ffn_conv_w'] = ffn_conv_w
    w['ffn_conv_b'] = ffn_conv_b[None, :]
    w['down'] = w_down.astype(BF16)
    return w


def kernel(x, c, ctx, c_ctx, w_mod, b_mod, norm1_g, w_in, b_gate, q_norm_g, kv_norm_g, w_uq, w_ukv,
           w_o_attn, lru_conv_w, lru_conv_b, lru_w_a, lru_b_a, lru_w_x, lru_b_x, lru_lambda, w_o_lru,
           w_out, norm2_g, w_up, ffn_conv_w, ffn_conv_b, w_down, final_g):
    assert w_mod.shape[0] == 1, "single-layer block"
    b, s, _ = x.shape
    c_len = ctx.shape[1]
    assert b + 1 <= MOD_ROWS

    w = _prep_weights(w_in[0], q_norm_g[0], kv_norm_g[0], w_uq[0], w_ukv[0], w_o_attn[0], lru_conv_w[0],
                      lru_conv_b[0], lru_w_a[0], lru_b_a[0], lru_w_x[0], lru_b_x[0], lru_lambda[0],
                      w_o_lru[0], w_out[0], b_gate[0], norm2_g[0], w_up[0], ffn_conv_w[0],
                      ffn_conv_b[0], w_down[0], final_g)
    g1n = norm1_g[0][None, :]

    cc = jnp.concatenate([c, c_ctx[None, :], jnp.zeros((MOD_ROWS - b - 1, D_MODEL), F32)], axis=0)
    mod = _mod_call(cc, w_mod, b_mod)
    mod_l = mod[:b].reshape(b, 1, 6, D_MODEL)
    sh1, sc1, g1, sh2, sc2, g2 = (mod_l[:, :, i] for i in range(6))
    mod_c = jnp.broadcast_to(mod[b].reshape(1, 1, 6, D_MODEL), (b, 1, 6, D_MODEL))
    sh1c, sc1c = mod_c[:, :, 0], mod_c[:, :, 1]

    k_c, v_c, xb_c = _proj_call(False, ctx, sh1c, sc1c, g1n, w, _identity_tables(c_len))
    states = _lru_call(False, xb_c, w)

    k_l, v_l, xb_l, q_l, yb_l, gl_l = _proj_call(True, x, sh1, sc1, g1n, w, _rope_tables(s))
    m = _lru_call(True, xb_l, w, yb=yb_l, h0=states)
    attn = _attn_call(q_l, k_l, k_c, v_l, v_c)
    x1 = _merge_call(attn, m, gl_l, x, g1, w)
    return _ffn_call(x1, sh2, sc2, g2, w)
```

```python
import functools

import jax
import jax.numpy as jnp
from jax import lax
from jax.experimental import pallas as pl
from jax.experimental.pallas import tpu as pltpu

F32 = jnp.float32
BF16 = jnp.bfloat16

D_MODEL = 1024
GRID_W = 64
N_HEADS = 8
QK_NOPE_DIM = 64
QK_ROPE_DIM = 32
ROPE_HALF = QK_ROPE_DIM // 2
V_HEAD_DIM = 64
QK_DIM = QK_NOPE_DIM + QK_ROPE_DIM
HEAD_PAD = 128
Q_LORA_RANK = 384
KV_LORA_RANK = 256
ROPE_BASE = 10000.0
LRU_WIDTH = 1280
LRU_BLOCKS = 10
LRU_BLOCK_W = LRU_WIDTH // LRU_BLOCKS
LRU_CONV_W = 4
LRU_C = 8.0
FFN_DIM = 2816
FFN_CONV_W = 3
EPS = 1e-6
TINY = 1e-30
OFF_KV = Q_LORA_RANK
OFF_KR = OFF_KV + KV_LORA_RANK
OFF_XB = OFF_KR + QK_ROPE_DIM
OFF_YB = OFF_XB + LRU_WIDTH
OFF_G = OFF_YB + LRU_WIDTH

SUBLANES = 8
LANES = 128
VMEM_LIMIT = 56 * 1024 * 1024

MOD_ROWS = 8
MOD_TN = 768
PROJ_TM = 256
LRU_TC = 256
LRU_UNROLL = 16
LRU_FIX_ROWS = 64
GATE_BIAS_ROWS = 3
VELTKAMP_8BIT = 65537.0
LOG2_E = 1.4426950408889634
ATTN_TQ = 256
ATTN_TK = 256
MERGE_TM = 256
FFN_TM = 256
FFN_FC = 256
HALO = SUBLANES


def _dot(a, b):
    return jnp.dot(a, b, preferred_element_type=F32)


def _dot_nt(a, b):
    return lax.dot_general(a, b, (((1,), (1,)), ((), ())), preferred_element_type=F32)


def _rms(x, g):
    return x * lax.rsqrt(jnp.mean(x * x, axis=-1, keepdims=True) + EPS) * g


def _const_spec(shape):
    nd = len(shape)
    return pl.BlockSpec(shape, lambda *_: (0,) * nd, pipeline_mode=pl.Buffered(1))


def _params(sem):
    return pltpu.CompilerParams(dimension_semantics=sem, vmem_limit_bytes=VMEM_LIMIT)


def _mod_body(c_ref, w_ref, b_ref, o_ref):
    c = c_ref[...]
    s = c * jax.nn.sigmoid(c)
    o_ref[...] = _dot(s.astype(BF16), w_ref[...].astype(BF16)) + b_ref[...]


def _mod_call(cc, w_mod, b_mod):
    n = w_mod.shape[-1]
    return pl.pallas_call(
        _mod_body,
        grid=(n // MOD_TN,),
        in_specs=[
            pl.BlockSpec((MOD_ROWS, D_MODEL), lambda j: (0, 0)),
            pl.BlockSpec((None, D_MODEL, MOD_TN), lambda j: (0, 0, j)),
            pl.BlockSpec((1, MOD_TN), lambda j: (0, j)),
        ],
        out_specs=pl.BlockSpec((MOD_ROWS, MOD_TN), lambda j: (0, j)),
        out_shape=jax.ShapeDtypeStruct((MOD_ROWS, n), F32),
        compiler_params=_params(("arbitrary",)),
        name="mod",
    )(cc, w_mod, b_mod)


def _rope128(t, c, s1, s2):
    return (t * c + pltpu.roll(t, HEAD_PAD - ROPE_HALF, 1) * s1
            + pltpu.roll(t, ROPE_HALF, 1) * s2)


def _proj_body(latent, x_ref, sh_ref, sc_ref, g1_ref, wkv_ref, wkr_ref, wxb_ref, gkv_ref,
               wuk_ref, wuv_ref, c_ref, s1_ref, s2_ref, *rest):
    if latent:
        (wq_ref, wyb_ref, wgl_ref, gq_ref, wuq_ref,
         k_ref, v_ref, xb_ref, q_ref, yb_ref, gl_ref) = rest
    else:
        k_ref, v_ref, xb_ref = rest
    x = x_ref[0]
    h = _rms(x, g1_ref[...]) * (1.0 + sc_ref[0]) + sh_ref[0]
    hb = h.astype(BF16)
    c, s1, s2 = c_ref[...], s1_ref[...], s2_ref[...]

    kvn = _rms(_dot(hb, wkv_ref[...]), gkv_ref[...]).astype(BF16)
    kr = _rope128(_dot(hb, wkr_ref[...]), c, s1, s2)
    kk = _dot(kvn, wuk_ref[...])
    for hd in range(N_HEADS):
        sl = slice(hd * HEAD_PAD, (hd + 1) * HEAD_PAD)
        k_ref[0, :, sl] = (kk[:, sl] + kr).astype(BF16)
    v_ref[0] = _dot_nt(wuv_ref[...], kvn).astype(BF16)
    xb_ref[0] = _dot(hb, wxb_ref[...])
    if latent:
        qn = _rms(_dot(hb, wq_ref[...]), gq_ref[...]).astype(BF16)
        qq = _dot(qn, wuq_ref[...])
        scale = QK_DIM ** -0.5 * LOG2_E
        for hd in range(N_HEADS):
            sl = slice(hd * HEAD_PAD, (hd + 1) * HEAD_PAD)
            q_ref[0, :, sl] = (_rope128(qq[:, sl], c, s1, s2) * scale).astype(BF16)
        yb_ref[0] = _dot(hb, wyb_ref[...])
        gl_ref[0] = _dot(hb, wgl_ref[...])


def _proj_call(latent, x, sh, sc, g1, w, tabs):
    b, s, _ = x.shape
    tm = min(PROJ_TM, s)
    row = lambda width: pl.BlockSpec((1, tm, width), lambda i, j: (j, i, 0))
    modrow = pl.BlockSpec((1, 1, D_MODEL), lambda i, j: (j, 0, 0))
    tab = pl.BlockSpec((tm, HEAD_PAD), lambda i, j: (i, 0))
    ins = [x, sh, sc, g1, w['kv'], w['kr'], w['xb'], w['gkv'], w['uk'], w['uv'], *tabs]
    in_specs = [row(D_MODEL), modrow, modrow] + [_const_spec(a.shape) for a in ins[3:10]] + [tab] * 3
    widths = [N_HEADS * HEAD_PAD, None, LRU_WIDTH]
    dtypes = [BF16, BF16, F32]
    if latent:
        extra = [w['q'], w['yb'], w['gl'], w['gq'], w['uq']]
        ins += extra
        in_specs += [_const_spec(a.shape) for a in extra]
        widths += [N_HEADS * HEAD_PAD, LRU_WIDTH, 2 * D_MODEL]
        dtypes += [BF16, F32, F32]
    hv = N_HEADS * V_HEAD_DIM
    out_specs = [row(wd) for wd in widths if wd is not None]
    out_shape = [jax.ShapeDtypeStruct((b, s, wd), dt) for wd, dt in zip(widths, dtypes) if wd is not None]
    out_specs.insert(1, pl.BlockSpec((1, hv, tm), lambda i, j: (j, 0, i)))
    out_shape.insert(1, jax.ShapeDtypeStruct((b, hv, s), BF16))
    return pl.pallas_call(
        functools.partial(_proj_body, latent),
        grid=(s // tm, b),
        in_specs=in_specs,
        out_specs=out_specs,
        out_shape=out_shape,
        compiler_params=_params(("parallel", "parallel")),
        name="proj_latent" if latent else "proj_ctx",
    )(*ins)


def _gelu_tanh(x):
    return 0.5 * x * (1.0 + jnp.tanh(0.7978845608028654 * (x + 0.044715 * (x * x * x))))


def _lru_body(latent, xb_ref, cw_ref, cb_ref, wg_ref, lam_ref, *rest):
    if latent:
        yb_ref, h0_ref, out_ref, xpad, af, uf, ab, ub, hfl, pfl, hbl, pbl = rest
    else:
        st_ref, xpad, af, uf, ab, ub, hfl, pfl, hbl, pbl = rest
    s = xb_ref.shape[1]
    seg = s // SUBLANES
    tc = min(LRU_TC, seg)
    nc = s // tc
    per_seg = seg // tc

    zero_halo = jnp.zeros((HALO, LANES), F32)
    xpad[0:HALO, :] = zero_halo
    xpad[s + HALO:s + 2 * HALO, :] = zero_halo

    def copy_chunk(ci, carry):
        r0 = pl.multiple_of(ci * tc, tc)
        xpad[pl.ds(r0 + HALO, tc), :] = xb_ref[0, pl.ds(r0, tc), :]
        return carry

    lax.fori_loop(0, nc, copy_chunk, 0)

    cw = cw_ref[...]
    cb = cb_ref[...]
    wg = wg_ref[0]
    lam = lam_ref[0]
    hcsp = (0.5 * LRU_C) * (jnp.maximum(-lam, 0.0) + jnp.log1p(jnp.exp(-jnp.abs(lam))))
    bias_lhs = jnp.where(lax.broadcasted_iota(jnp.int32, (tc, LANES), 1) < GATE_BIAS_ROWS, 1.0, 0.0).astype(BF16)

    def interleaved(ci):
        j = ci // per_seg
        return pl.ds((ci - j * per_seg) * (tc * SUBLANES) + j, tc, stride=SUBLANES)

    def gate_chunk(ci, carry):
        r0 = pl.multiple_of(ci * tc, tc)
        rd = interleaved(ci)
        xm2 = xpad[pl.ds(r0 + (HALO - 2), tc), :]
        xm1 = xpad[pl.ds(r0 + (HALO - 1), tc), :]
        x0 = xpad[pl.ds(r0 + HALO, tc), :]
        xp1 = xpad[pl.ds(r0 + (HALO + 1), tc), :]
        xc = cb + xm2 * cw[0:1] + xm1 * cw[1:2] + x0 * cw[2:3] + xp1 * cw[3:4]
        t = jnp.tanh(_dot(jnp.concatenate([xc.astype(BF16), bias_lhs], axis=1), wg))
        hx = 0.5 * xc
        for d, (a_ref, u_ref) in enumerate(((af, uf), (ab, ub))):
            t_r = t[:, (2 * d) * LANES:(2 * d + 1) * LANES]
            t_i = t[:, (2 * d + 1) * LANES:(2 * d + 2) * LANES]
            neg_log_a = hcsp[d:d + 1] + hcsp[d:d + 1] * t_r
            a = jnp.exp2(neg_log_a * (-LOG2_E))
            y = jnp.tanh(neg_log_a) * (1.0 + a * a)
            mult = y * lax.rsqrt(jnp.maximum(y, TINY))
            a_ref[rd, :] = a
            u_ref[rd, :] = (mult * hx) * (1.0 + t_i)
        return carry

    lax.fori_loop(0, nc, gate_chunk, 0, unroll=2)

    def scan_step(i, carry):
        hf, pf, hb, pb = carry
        fwd = pl.ds(pl.multiple_of(i * SUBLANES, SUBLANES), SUBLANES)
        bwd = pl.ds(pl.multiple_of((seg - 1 - i) * SUBLANES, SUBLANES), SUBLANES)
        a = af[fwd, :]
        hf = a * hf + uf[fwd, :]
        pf = a * pf
        hfl[fwd, :] = hf
        pfl[fwd, :] = pf
        a = ab[bwd, :]
        hb = a * hb + ub[bwd, :]
        pb = a * pb
        hbl[bwd, :] = hb
        pbl[bwd, :] = pb
        return hf, pf, hb, pb

    zeros = jnp.zeros((SUBLANES, LANES), F32)
    ones = jnp.ones((SUBLANES, LANES), F32)
    hf, pf, hb, pb = lax.fori_loop(0, seg, scan_step, (zeros, ones, zeros, ones), unroll=LRU_UNROLL)

    if latent:
        h0 = h0_ref[0]
        cf, cbk = h0[0:1], h0[1:2]
    else:
        cf = cbk = jnp.zeros((1, LANES), F32)
    cfs = []
    for j in range(SUBLANES):
        cfs.append(cf)
        cf = hf[j:j + 1] + pf[j:j + 1] * cf
    cbs = [None] * SUBLANES
    for j in reversed(range(SUBLANES)):
        cbs[j] = cbk
        cbk = hb[j:j + 1] + pb[j:j + 1] * cbk

    if latent:
        fix_rows = min(LRU_FIX_ROWS, s)
        cf_all = jnp.tile(jnp.concatenate(cfs, axis=0), (fix_rows // SUBLANES, 1))
        cb_all = jnp.tile(jnp.concatenate(cbs, axis=0), (fix_rows // SUBLANES, 1))

        def fix_chunk(ci, carry):
            rows = pl.ds(pl.multiple_of(ci * fix_rows, fix_rows), fix_rows)
            af[rows, :] = (hfl[rows, :] + pfl[rows, :] * cf_all) + (hbl[rows, :] + pbl[rows, :] * cb_all)
            return carry

        lax.fori_loop(0, s // fix_rows, fix_chunk, 0, unroll=2)

        def out_chunk(ci, carry):
            r0 = pl.multiple_of(ci * tc, tc)
            hsum = af[interleaved(ci), :]
            out_ref[0, pl.ds(r0, tc), :] = (hsum * _gelu_tanh(yb_ref[0, pl.ds(r0, tc), :])).astype(BF16)
            return carry

        lax.fori_loop(0, nc, out_chunk, 0)
    else:
        st_ref[0] = jnp.concatenate([cf, cbk], axis=0)


def _lru_call(latent, xb, w, yb=None, h0=None):
    b, s, _ = xb.shape
    seq = pl.BlockSpec((1, s, LANES), lambda i, j: (i, 0, j))
    st = pl.BlockSpec((1, 2, LANES), lambda i, j: (i, 0, j))
    in_specs = [
        seq,
        pl.BlockSpec((LRU_CONV_W, LANES), lambda i, j: (0, j)),
        pl.BlockSpec((1, LANES), lambda i, j: (0, j)),
        pl.BlockSpec((1, 2 * LANES, 4 * LANES), lambda i, j: (j, 0, 0)),
        pl.BlockSpec((1, 2, LANES), lambda i, j: (j, 0, 0)),
    ]
    ins = [xb, w['conv_w'], w['conv_b'], w['gate_w'], w['lam']]
    scratch = [pltpu.VMEM((s + 2 * HALO, LANES), F32)] + [pltpu.VMEM((s, LANES), F32)] * 8
    if latent:
        ins += [yb, h0]
        in_specs += [seq, st]
        out_specs = seq
        out_shape = jax.ShapeDtypeStruct((b, s, LRU_WIDTH), BF16)
    else:
        out_specs = st
        out_shape = jax.ShapeDtypeStruct((b, 2, LRU_WIDTH), F32)
    return pl.pallas_call(
        functools.partial(_lru_body, latent),
        grid=(b, LRU_BLOCKS),
        in_specs=in_specs,
        out_specs=out_specs,
        out_shape=out_shape,
        scratch_shapes=scratch,
        compiler_params=_params(("parallel", "parallel")),
        name="lru_latent" if latent else "lru_ctx",
    )(*ins)


def _col_groups(x, op):
    rows, cols = x.shape
    return op(x.reshape(rows // SUBLANES, SUBLANES, cols), axis=0)


def _attn_body(q_ref, kl_ref, kc_ref, vl_ref, vc_ref, o_ref, s_ref):
    s_len = kl_ref.shape[1]
    c_len = kc_ref.shape[1]
    tk = min(ATTN_TK, s_len)
    chunks = [(kc_ref, vc_ref, 0, c_len, 0)]
    chunks += [(kl_ref, vl_ref, ki * tk, tk, c_len + ki * tk) for ki in range(s_len // tk)]
    maxes = []
    for hd in range(2):
        lanes = slice(hd * HEAD_PAD, (hd + 1) * HEAD_PAD)
        q = q_ref[0, :, lanes]
        m8 = None
        for k_ref, _, r0, rows, off in chunks:
            sk = _dot_nt(k_ref[0, r0:r0 + rows, lanes], q)
            s_ref[hd, off:off + rows, :] = sk
            g = _col_groups(sk, jnp.max)
            m8 = g if m8 is None else jnp.maximum(m8, g)
        maxes.append(jnp.max(m8, axis=0, keepdims=True))
    outs = []
    for hd in range(2):
        vrows = slice(hd * V_HEAD_DIM, (hd + 1) * V_HEAD_DIM)
        l8 = acc = None
        for _, v_ref, r0, rows, off in chunks:
            p = jnp.exp2(s_ref[hd, off:off + rows, :] - maxes[hd])
            g = _col_groups(p, jnp.sum)
            d = _dot(v_ref[0, vrows, r0:r0 + rows], p.astype(BF16))
            l8 = g if l8 is None else l8 + g
            acc = d if acc is None else acc + d
        outs.append(acc / jnp.sum(l8, axis=0, keepdims=True))
    o_ref[0] = jnp.concatenate(outs, axis=0).T.astype(BF16)


def _attn_call(q, k_l, k_c, vt_l, vt_c):
    b, s, _ = q.shape
    c_len = k_c.shape[1]
    tq = min(ATTN_TQ, s)
    pairs = N_HEADS // 2
    return pl.pallas_call(
        _attn_body,
        grid=(b, pairs, s // tq),
        in_specs=[
            pl.BlockSpec((1, tq, 2 * HEAD_PAD), lambda i, j, t: (i, t, j)),
            pl.BlockSpec((1, s, 2 * HEAD_PAD), lambda i, j, t: (i, 0, j)),
            pl.BlockSpec((1, c_len, 2 * HEAD_PAD), lambda i, j, t: (i, 0, j)),
            pl.BlockSpec((1, 2 * V_HEAD_DIM, s), lambda i, j, t: (i, j, 0)),
            pl.BlockSpec((1, 2 * V_HEAD_DIM, c_len), lambda i, j, t: (i, j, 0)),
        ],
        out_specs=pl.BlockSpec((1, tq, 2 * V_HEAD_DIM), lambda i, j, t: (i, t, j)),
        out_shape=jax.ShapeDtypeStruct((b, s, N_HEADS * V_HEAD_DIM), BF16),
        scratch_shapes=[pltpu.VMEM((2, c_len + s, tq), F32)],
        compiler_params=_params(("parallel", "parallel", "parallel")),
        name="attn",
    )(q, k_l, k_c, vt_l, vt_c)


def _merge_body(attn_ref, m_ref, gl_ref, x_ref, g1_ref, bg_ref, woa_ref, wol_ref, wout_ref, o_ref):
    y_a = _dot(attn_ref[0], woa_ref[...])
    y_b = _dot(m_ref[0], wol_ref[...])
    gates = jax.nn.sigmoid(gl_ref[0] + bg_ref[...])
    mix = gates[:, :D_MODEL] * y_a + gates[:, D_MODEL:] * y_b
    o_ref[0] = x_ref[0] + g1_ref[0] * _dot(mix.astype(BF16), wout_ref[...])


def _merge_call(attn, m, gl, x, g1, w):
    b, s, _ = x.shape
    tm = min(MERGE_TM, s)
    row = lambda width: pl.BlockSpec((1, tm, width), lambda i, j: (i, j, 0))
    modrow = pl.BlockSpec((1, 1, D_MODEL), lambda i, j: (i, 0, 0))
    consts = [w['b_gate'], w['o_attn'], w['o_lru'], w['out']]
    return pl.pallas_call(
        _merge_body,
        grid=(b, s // tm),
        in_specs=[row(N_HEADS * V_HEAD_DIM), row(LRU_WIDTH), row(2 * D_MODEL), row(D_MODEL), modrow]
        + [_const_spec(a.shape) for a in consts],
        out_specs=row(D_MODEL),
        out_shape=jax.ShapeDtypeStruct((b, s, D_MODEL), F32),
        compiler_params=_params(("parallel", "parallel")),
        name="merge",
    )(attn, m, gl, x, g1, *consts)


def _ffn_body(xp_ref, x_ref, xn_ref, sh_ref, sc_ref, g2_ref, n2_ref, fg_ref, wup_ref, cw_ref, cb_ref,
              wdn_ref, o_ref):
    j = pl.program_id(1)
    nj = pl.num_programs(1)
    tm = x_ref.shape[1]
    x = x_ref[0]
    xe = jnp.concatenate([xp_ref[0], x, xn_ref[0]], axis=0)
    hf = _rms(xe, n2_ref[...]) * (1.0 + sc_ref[0]) + sh_ref[0]
    he = hf.astype(BF16)
    ht = hf[HALO:HALO + tm].astype(BF16)
    rows = lax.broadcasted_iota(jnp.int32, (tm + 2 * HALO, 1), 0)
    valid = jnp.logical_and(jnp.logical_or(rows >= HALO, j > 0),
                            jnp.logical_or(rows < HALO + tm, j < nj - 1))
    ext = tm + 2 * HALO
    acc = jnp.zeros((tm, D_MODEL), F32)
    for ci in range(FFN_DIM // FFN_FC):
        cols = slice(ci * FFN_FC, (ci + 1) * FFN_FC)
        gcols = slice(FFN_DIM + ci * FFN_FC, FFN_DIM + (ci + 1) * FFN_FC)
        a = jnp.where(valid, _dot(he, wup_ref[:, cols]), 0.0)
        cw = cw_ref[:, cols]
        conv = (cb_ref[:, cols] + pltpu.roll(a, 1, 0)[HALO:HALO + tm] * cw[0:1]
                + a[HALO:HALO + tm] * cw[1:2]
                + pltpu.roll(a, ext - 1, 0)[HALO:HALO + tm] * cw[2:3])
        g = _dot(ht, wup_ref[:, gcols])
        f = (conv * jax.nn.sigmoid(conv) * g).astype(BF16)
        acc = acc + _dot(f, wdn_ref[cols, :])
    y = x + g2_ref[0] * acc
    o_ref[0] = _rms(y, fg_ref[...])


def _ffn_call(x1, sh2, sc2, g2, w):
    b, s, _ = x1.shape
    tm = min(FFN_TM, s)
    per = tm // HALO
    nblk = s // HALO
    row = pl.BlockSpec((1, tm, D_MODEL), lambda i, j: (i, j, 0))
    prev = pl.BlockSpec((1, HALO, D_MODEL), lambda i, j: (i, jnp.maximum(j * per - 1, 0), 0))
    nxt = pl.BlockSpec((1, HALO, D_MODEL), lambda i, j: (i, jnp.minimum((j + 1) * per, nblk - 1), 0))
    modrow = pl.BlockSpec((1, 1, D_MODEL), lambda i, j: (i, 0, 0))
    consts = [w['norm2_g'], w['final_g'], w['up'], w['ffn_conv_w'], w['ffn_conv_b'], w['down']]
    return pl.pallas_call(
        _ffn_body,
        grid=(b, s // tm),
        in_specs=[prev, row, nxt, modrow, modrow, modrow] + [_const_spec(a.shape) for a in consts],
        out_specs=row,
        out_shape=jax.ShapeDtypeStruct((b, s, D_MODEL), F32),
        compiler_params=_params(("parallel", "parallel")),
        name="ffn",
    )(x1, x1, x1, sh2, sc2, g2, *consts)


def _rope_tables(n):
    rows = n // GRID_W
    row_ids = jnp.repeat(jnp.arange(rows), GRID_W).astype(F32)
    col_ids = jnp.tile(jnp.arange(GRID_W), rows).astype(F32)
    axis_dim = QK_ROPE_DIM // 2
    inv = 1.0 / (ROPE_BASE ** (jnp.arange(0, axis_dim, 2, dtype=F32) / axis_dim))
    ang = jnp.concatenate([row_ids[:, None] * inv, col_ids[:, None] * inv], axis=-1)
    cos, sin = jnp.cos(ang), jnp.sin(ang)
    ones = lambda w_: jnp.ones((n, w_), F32)
    zeros = lambda w_: jnp.zeros((n, w_), F32)
    tail = HEAD_PAD - QK_DIM
    c = jnp.concatenate([ones(QK_NOPE_DIM), cos, cos, ones(tail)], axis=-1)
    s1 = jnp.concatenate([zeros(QK_NOPE_DIM), -sin, zeros(ROPE_HALF + tail)], axis=-1)
    s2 = jnp.concatenate([zeros(QK_NOPE_DIM + ROPE_HALF), sin, zeros(tail)], axis=-1)
    return c, s1, s2


def _identity_tables(n):
    return (jnp.ones((n, HEAD_PAD), F32), jnp.zeros((n, HEAD_PAD), F32), jnp.zeros((n, HEAD_PAD), F32))


def _prep_weights(w_in, q_norm_g, kv_norm_g, w_uq, w_ukv, w_o_attn, lru_conv_w, lru_conv_b, lru_w_a,
                  lru_b_a, lru_w_x, lru_b_x, lru_lambda, w_o_lru, w_out, b_gate, norm2_g, w_up,
                  ffn_conv_w, ffn_conv_b, w_down, final_g):
    w = {}
    w['q'] = w_in[:, :OFF_KV].astype(BF16)
    w['kv'] = w_in[:, OFF_KV:OFF_KR].astype(BF16)
    w['kr'] = jnp.pad(w_in[:, OFF_KR:OFF_XB], ((0, 0), (QK_NOPE_DIM, HEAD_PAD - QK_DIM))).astype(BF16)
    w['xb'] = w_in[:, OFF_XB:OFF_YB].astype(BF16)
    w['yb'] = w_in[:, OFF_YB:OFF_G].astype(BF16)
    w['gl'] = w_in[:, OFF_G:].astype(BF16)
    w['gq'] = q_norm_g[None, :]
    w['gkv'] = kv_norm_g[None, :]
    uq = w_uq.reshape(Q_LORA_RANK, N_HEADS, QK_DIM)
    w['uq'] = jnp.pad(uq, ((0, 0), (0, 0), (0, HEAD_PAD - QK_DIM))).reshape(Q_LORA_RANK, -1).astype(BF16)
    ukv = w_ukv.reshape(KV_LORA_RANK, N_HEADS, QK_NOPE_DIM + V_HEAD_DIM)
    w['uk'] = jnp.pad(ukv[..., :QK_NOPE_DIM],
                      ((0, 0), (0, 0), (0, HEAD_PAD - QK_NOPE_DIM))).reshape(KV_LORA_RANK, -1).astype(BF16)
    w['uv'] = ukv[..., QK_NOPE_DIM:].reshape(KV_LORA_RANK, -1).T.astype(BF16)
    w['conv_w'] = lru_conv_w
    w['conv_b'] = lru_conv_b[None, :]
    gw = 0.5 * jnp.concatenate([lru_w_a[0], lru_w_x[0], lru_w_a[1], lru_w_x[1]], axis=-1)
    gb = 0.5 * jnp.stack([lru_b_a[0], lru_b_x[0], lru_b_a[1], lru_b_x[1]], axis=0)
    gb = gb.reshape(4, LRU_BLOCKS, LRU_BLOCK_W).transpose(1, 0, 2).reshape(LRU_BLOCKS, 1, -1)
    terms = []
    for _ in range(GATE_BIAS_ROWS):
        t = gb * VELTKAMP_8BIT
        hi = t - (t - gb)
        terms.append(hi)
        gb = gb - hi
    pad = jnp.zeros((LRU_BLOCKS, LANES - GATE_BIAS_ROWS, 4 * LRU_BLOCK_W), F32)
    w['gate_w'] = jnp.concatenate([gw] + terms + [pad], axis=1).astype(BF16)
    w['lam'] = lru_lambda.reshape(2, LRU_BLOCKS, LRU_BLOCK_W).transpose(1, 0, 2)
    w['o_attn'] = w_o_attn.astype(BF16)
    w['o_lru'] = w_o_lru.astype(BF16)
    w['out'] = w_out.astype(BF16)
    w['b_gate'] = b_gate[None, :]
    w['norm2_g'] = norm2_g[None, :]
    w['final_g'] = final_g[None, :]
    w['up'] = w_up.astype(BF16)
    w['ffn_conv_w'] = ffn_conv_w
    w['ffn_conv_b'] = ffn_conv_b[None, :]
    w['down'] = w_down.astype(BF16)
    return w


def kernel(x, c, ctx, c_ctx, w_mod, b_mod, norm1_g, w_in, b_gate, q_norm_g, kv_norm_g, w_uq, w_ukv,
           w_o_attn, lru_conv_w, lru_conv_b, lru_w_a, lru_b_a, lru_w_x, lru_b_x, lru_lambda, w_o_lru,
           w_out, norm2_g, w_up, ffn_conv_w, ffn_conv_b, w_down, final_g):
    assert w_mod.shape[0] == 1, "single-layer block"
    b, s, _ = x.shape
    c_len = ctx.shape[1]
    assert b + 1 <= MOD_ROWS

    w = _prep_weights(w_in[0], q_norm_g[0], kv_norm_g[0], w_uq[0], w_ukv[0], w_o_attn[0], lru_conv_w[0],
                      lru_conv_b[0], lru_w_a[0], lru_b_a[0], lru_w_x[0], lru_b_x[0], lru_lambda[0],
                      w_o_lru[0], w_out[0], b_gate[0], norm2_g[0], w_up[0], ffn_conv_w[0],
                      ffn_conv_b[0], w_down[0], final_g)
    g1n = norm1_g[0][None, :]

    cc = jnp.concatenate([c, c_ctx[None, :], jnp.zeros((MOD_ROWS - b - 1, D_MODEL), F32)], axis=0)
    mod = _mod_call(cc, w_mod, b_mod)
    mod_l = mod[:b].reshape(b, 1, 6, D_MODEL)
    sh1, sc1, g1, sh2, sc2, g2 = (mod_l[:, :, i] for i in range(6))
    mod_c = jnp.broadcast_to(mod[b].reshape(1, 1, 6, D_MODEL), (b, 1, 6, D_MODEL))
    sh1c, sc1c = mod_c[:, :, 0], mod_c[:, :, 1]

    k_c, v_c, xb_c = _proj_call(False, ctx, sh1c, sc1c, g1n, w, _identity_tables(c_len))
    states = _lru_call(False, xb_c, w)

    k_l, v_l, xb_l, q_l, yb_l, gl_l = _proj_call(True, x, sh1, sc1, g1n, w, _rope_tables(s))
    m = _lru_call(True, xb_l, w, yb=yb_l, h0=states)
    attn = _attn_call(q_l, k_l, k_c, v_l, v_c)
    x1 = _merge_call(attn, m, gl_l, x, g1, w)
    return _ffn_call(x1, sh2, sc2, g2, w)
```

```python
import functools

import jax
import jax.numpy as jnp
from jax import lax
from jax.experimental import pallas as pl
from jax.experimental.pallas import tpu as pltpu

F32 = jnp.float32
BF16 = jnp.bfloat16

D_MODEL = 1024
GRID_W = 64
N_HEADS = 8
QK_NOPE_DIM = 64
QK_ROPE_DIM = 32
ROPE_HALF = QK_ROPE_DIM // 2
V_HEAD_DIM = 64
QK_DIM = QK_NOPE_DIM + QK_ROPE_DIM
HEAD_PAD = 128
Q_LORA_RANK = 384
KV_LORA_RANK = 256
ROPE_BASE = 10000.0
LRU_WIDTH = 1280
LRU_BLOCKS = 10
LRU_BLOCK_W = LRU_WIDTH // LRU_BLOCKS
LRU_CONV_W = 4
LRU_C = 8.0
FFN_DIM = 2816
FFN_CONV_W = 3
EPS = 1e-6
TINY = 1e-30
OFF_KV = Q_LORA_RANK
OFF_KR = OFF_KV + KV_LORA_RANK
OFF_XB = OFF_KR + QK_ROPE_DIM
OFF_YB = OFF_XB + LRU_WIDTH
OFF_G = OFF_YB + LRU_WIDTH

SUBLANES = 8
LANES = 128
VMEM_LIMIT = 56 * 1024 * 1024

MOD_ROWS = 8
MOD_TN = 768
PROJ_TM = 512
LRU_TC = 256
LRU_UNROLL = 16
LRU_FIX_ROWS = 64
GATE_BIAS_ROWS = 3
VELTKAMP_8BIT = 65537.0
LOG2_E = 1.4426950408889634
ATTN_TQ = 256
ATTN_TK = 256
MERGE_TM = 512
MERGE_SUBTILES = 2
FFN_TM = 512
FFN_FC = 256
HALO = SUBLANES


def _dot(a, b):
    return jnp.dot(a, b, preferred_element_type=F32)


def _dot_nt(a, b):
    return lax.dot_general(a, b, (((1,), (1,)), ((), ())), preferred_element_type=F32)


def _rms(x, g):
    return x * lax.rsqrt(jnp.mean(x * x, axis=-1, keepdims=True) + EPS) * g


def _const_spec(shape):
    nd = len(shape)
    return pl.BlockSpec(shape, lambda *_: (0,) * nd, pipeline_mode=pl.Buffered(1))


def _params(sem):
    return pltpu.CompilerParams(dimension_semantics=sem, vmem_limit_bytes=VMEM_LIMIT)


def _mod_body(c_ref, w_ref, b_ref, o_ref):
    c = c_ref[...]
    s = c * jax.nn.sigmoid(c)
    o_ref[...] = _dot(s.astype(BF16), w_ref[...].astype(BF16)) + b_ref[...]


def _mod_call(cc, w_mod, b_mod):
    n = w_mod.shape[-1]
    return pl.pallas_call(
        _mod_body,
        grid=(n // MOD_TN,),
        in_specs=[
            pl.BlockSpec((MOD_ROWS, D_MODEL), lambda j: (0, 0)),
            pl.BlockSpec((None, D_MODEL, MOD_TN), lambda j: (0, 0, j)),
            pl.BlockSpec((1, MOD_TN), lambda j: (0, j)),
        ],
        out_specs=pl.BlockSpec((MOD_ROWS, MOD_TN), lambda j: (0, j)),
        out_shape=jax.ShapeDtypeStruct((MOD_ROWS, n), F32),
        compiler_params=_params(("arbitrary",)),
        name="mod",
    )(cc, w_mod, b_mod)


def _rope128(t, c, s1, s2):
    return (t * c + pltpu.roll(t, HEAD_PAD - ROPE_HALF, 1) * s1
            + pltpu.roll(t, ROPE_HALF, 1) * s2)


def _proj_body(latent, x_ref, sh_ref, sc_ref, g1_ref, wkv_ref, wkr_ref, wxb_ref, gkv_ref,
               wuk_ref, wuv_ref, c_ref, s1_ref, s2_ref, *rest):
    if latent:
        (wq_ref, wyb_ref, wgl_ref, gq_ref, wuq_ref,
         k_ref, v_ref, xb_ref, q_ref, yb_ref, gl_ref) = rest
    else:
        k_ref, v_ref, xb_ref = rest
    x = x_ref[0]
    h = _rms(x, g1_ref[...]) * (1.0 + sc_ref[0]) + sh_ref[0]
    hb = h.astype(BF16)
    c, s1, s2 = c_ref[...], s1_ref[...], s2_ref[...]

    kv_lat = _dot(hb, wkv_ref[...])
    if latent:
        q_lat = _dot(hb, wq_ref[...])
    kr_raw = _dot(hb, wkr_ref[...])
    xb_ref[0] = _dot(hb, wxb_ref[...])
    kvn = _rms(kv_lat, gkv_ref[...]).astype(BF16)
    kk = _dot(kvn, wuk_ref[...])
    v_ref[0] = _dot_nt(wuv_ref[...], kvn).astype(BF16)
    if latent:
        qn = _rms(q_lat, gq_ref[...]).astype(BF16)
        qq = _dot(qn, wuq_ref[...])
        yb_ref[0] = _dot(hb, wyb_ref[...])
        gl_ref[0] = _dot(hb, wgl_ref[...])
    kr = _rope128(kr_raw, c, s1, s2)
    for hd in range(N_HEADS):
        sl = slice(hd * HEAD_PAD, (hd + 1) * HEAD_PAD)
        k_ref[0, :, sl] = (kk[:, sl] + kr).astype(BF16)
    if latent:
        scale = QK_DIM ** -0.5 * LOG2_E
        for hd in range(N_HEADS):
            sl = slice(hd * HEAD_PAD, (hd + 1) * HEAD_PAD)
            q_ref[0, :, sl] = (_rope128(qq[:, sl], c, s1, s2) * scale).astype(BF16)


def _proj_call(latent, x, sh, sc, g1, w, tabs):
    b, s, _ = x.shape
    tm = min(PROJ_TM, s)
    row = lambda width: pl.BlockSpec((1, tm, width), lambda i, j: (j, i, 0))
    modrow = pl.BlockSpec((1, 1, D_MODEL), lambda i, j: (j, 0, 0))
    tab = pl.BlockSpec((tm, HEAD_PAD), lambda i, j: (i, 0))
    ins = [x, sh, sc, g1, w['kv'], w['kr'], w['xb'], w['gkv'], w['uk'], w['uv'], *tabs]
    in_specs = [row(D_MODEL), modrow, modrow] + [_const_spec(a.shape) for a in ins[3:10]] + [tab] * 3
    widths = [N_HEADS * HEAD_PAD, None, LRU_WIDTH]
    dtypes = [BF16, BF16, F32]
    if latent:
        extra = [w['q'], w['yb'], w['gl'], w['gq'], w['uq']]
        ins += extra
        in_specs += [_const_spec(a.shape) for a in extra]
        widths += [N_HEADS * HEAD_PAD, LRU_WIDTH, 2 * D_MODEL]
        dtypes += [BF16, F32, F32]
    hv = N_HEADS * V_HEAD_DIM
    out_specs = [row(wd) for wd in widths if wd is not None]
    out_shape = [jax.ShapeDtypeStruct((b, s, wd), dt) for wd, dt in zip(widths, dtypes) if wd is not None]
    out_specs.insert(1, pl.BlockSpec((1, hv, tm), lambda i, j: (j, 0, i)))
    out_shape.insert(1, jax.ShapeDtypeStruct((b, hv, s), BF16))
    return pl.pallas_call(
        functools.partial(_proj_body, latent),
        grid=(s // tm, b),
        in_specs=in_specs,
        out_specs=out_specs,
        out_shape=out_shape,
        compiler_params=_params(("parallel", "parallel")),
        name="proj_latent" if latent else "proj_ctx",
    )(*ins)


def _gelu_tanh(x):
    return 0.5 * x * (1.0 + jnp.tanh(0.7978845608028654 * (x + 0.044715 * (x * x * x))))


def _lru_body(latent, xb_ref, cw_ref, cb_ref, wg_ref, lam_ref, *rest):
    if latent:
        yb_ref, h0_ref, out_ref, xpad, af, uf, ab, ub, hfl, pfl, hbl, pbl = rest
    else:
        st_ref, xpad, af, uf, ab, ub, hfl, pfl, hbl, pbl = rest
    s = xb_ref.shape[1]
    seg = s // SUBLANES
    tc = min(LRU_TC, seg)
    nc = s // tc
    per_seg = seg // tc

    zero_halo = jnp.zeros((HALO, LANES), F32)
    xpad[0:HALO, :] = zero_halo
    xpad[s + HALO:s + 2 * HALO, :] = zero_halo

    def copy_chunk(ci, carry):
        r0 = pl.multiple_of(ci * tc, tc)
        xpad[pl.ds(r0 + HALO, tc), :] = xb_ref[0, pl.ds(r0, tc), :]
        return carry

    lax.fori_loop(0, nc, copy_chunk, 0)

    cw = cw_ref[...]
    cb = cb_ref[...]
    wg = wg_ref[0]
    lam = lam_ref[0]
    hcsp = (0.5 * LRU_C) * (jnp.maximum(-lam, 0.0) + jnp.log1p(jnp.exp(-jnp.abs(lam))))
    bias_lhs = jnp.where(lax.broadcasted_iota(jnp.int32, (tc, LANES), 1) < GATE_BIAS_ROWS, 1.0, 0.0).astype(BF16)

    def interleaved(ci):
        j = ci // per_seg
        return pl.ds((ci - j * per_seg) * (tc * SUBLANES) + j, tc, stride=SUBLANES)

    def gate_chunk(ci, carry):
        r0 = pl.multiple_of(ci * tc, tc)
        rd = interleaved(ci)
        xm2 = xpad[pl.ds(r0 + (HALO - 2), tc), :]
        xm1 = xpad[pl.ds(r0 + (HALO - 1), tc), :]
        x0 = xpad[pl.ds(r0 + HALO, tc), :]
        xp1 = xpad[pl.ds(r0 + (HALO + 1), tc), :]
        xc = cb + xm2 * cw[0:1] + xm1 * cw[1:2] + x0 * cw[2:3] + xp1 * cw[3:4]
        t = jnp.tanh(_dot(jnp.concatenate([xc.astype(BF16), bias_lhs], axis=1), wg))
        hx = 0.5 * xc
        for d, (a_ref, u_ref) in enumerate(((af, uf), (ab, ub))):
            t_r = t[:, (2 * d) * LANES:(2 * d + 1) * LANES]
            t_i = t[:, (2 * d + 1) * LANES:(2 * d + 2) * LANES]
            neg_log_a = hcsp[d:d + 1] + hcsp[d:d + 1] * t_r
            a = jnp.exp2(neg_log_a * (-LOG2_E))
            y = jnp.tanh(neg_log_a) * (1.0 + a * a)
            mult = y * lax.rsqrt(jnp.maximum(y, TINY))
            a_ref[rd, :] = a
            u_ref[rd, :] = (mult * hx) * (1.0 + t_i)
        return carry

    lax.fori_loop(0, nc, gate_chunk, 0, unroll=2)

    def scan_step(i, carry):
        hf, pf, hb, pb = carry
        fwd = pl.ds(pl.multiple_of(i * SUBLANES, SUBLANES), SUBLANES)
        bwd = pl.ds(pl.multiple_of((seg - 1 - i) * SUBLANES, SUBLANES), SUBLANES)
        a = af[fwd, :]
        hf = a * hf + uf[fwd, :]
        pf = a * pf
        hfl[fwd, :] = hf
        pfl[fwd, :] = pf
        a = ab[bwd, :]
        hb = a * hb + ub[bwd, :]
        pb = a * pb
        hbl[bwd, :] = hb
        pbl[bwd, :] = pb
        return hf, pf, hb, pb

    zeros = jnp.zeros((SUBLANES, LANES), F32)
    ones = jnp.ones((SUBLANES, LANES), F32)
    hf, pf, hb, pb = lax.fori_loop(0, seg, scan_step, (zeros, ones, zeros, ones), unroll=LRU_UNROLL)

    if latent:
        h0 = h0_ref[0]
        cf, cbk = h0[0:1], h0[1:2]
    else:
        cf = cbk = jnp.zeros((1, LANES), F32)
    cfs = []
    for j in range(SUBLANES):
        cfs.append(cf)
        cf = hf[j:j + 1] + pf[j:j + 1] * cf
    cbs = [None] * SUBLANES
    for j in reversed(range(SUBLANES)):
        cbs[j] = cbk
        cbk = hb[j:j + 1] + pb[j:j + 1] * cbk

    if latent:
        fix_rows = min(LRU_FIX_ROWS, s)
        cf_all = jnp.tile(jnp.concatenate(cfs, axis=0), (fix_rows // SUBLANES, 1))
        cb_all = jnp.tile(jnp.concatenate(cbs, axis=0), (fix_rows // SUBLANES, 1))

        def fix_chunk(ci, carry):
            rows = pl.ds(pl.multiple_of(ci * fix_rows, fix_rows), fix_rows)
            af[rows, :] = (hfl[rows, :] + pfl[rows, :] * cf_all) + (hbl[rows, :] + pbl[rows, :] * cb_all)
            return carry

        lax.fori_loop(0, s // fix_rows, fix_chunk, 0, unroll=2)

        def out_chunk(ci, carry):
            r0 = pl.multiple_of(ci * tc, tc)
            hsum = af[interleaved(ci), :]
            out_ref[0, pl.ds(r0, tc), :] = (hsum * _gelu_tanh(yb_ref[0, pl.ds(r0, tc), :])).astype(BF16)
            return carry

        lax.fori_loop(0, nc, out_chunk, 0)
    else:
        st_ref[0] = jnp.concatenate([cf, cbk], axis=0)


def _lru_call(latent, xb, w, yb=None, h0=None):
    b, s, _ = xb.shape
    seq = pl.BlockSpec((1, s, LANES), lambda i, j: (i, 0, j))
    st = pl.BlockSpec((1, 2, LANES), lambda i, j: (i, 0, j))
    in_specs = [
        seq,
        pl.BlockSpec((LRU_CONV_W, LANES), lambda i, j: (0, j)),
        pl.BlockSpec((1, LANES), lambda i, j: (0, j)),
        pl.BlockSpec((1, 2 * LANES, 4 * LANES), lambda i, j: (j, 0, 0)),
        pl.BlockSpec((1, 2, LANES), lambda i, j: (j, 0, 0)),
    ]
    ins = [xb, w['conv_w'], w['conv_b'], w['gate_w'], w['lam']]
    scratch = [pltpu.VMEM((s + 2 * HALO, LANES), F32)] + [pltpu.VMEM((s, LANES), F32)] * 8
    if latent:
        ins += [yb, h0]
        in_specs += [seq, st]
        out_specs = seq
        out_shape = jax.ShapeDtypeStruct((b, s, LRU_WIDTH), BF16)
    else:
        out_specs = st
        out_shape = jax.ShapeDtypeStruct((b, 2, LRU_WIDTH), F32)
    return pl.pallas_call(
        functools.partial(_lru_body, latent),
        grid=(b, LRU_BLOCKS),
        in_specs=in_specs,
        out_specs=out_specs,
        out_shape=out_shape,
        scratch_shapes=scratch,
        compiler_params=_params(("parallel", "parallel")),
        name="lru_latent" if latent else "lru_ctx",
    )(*ins)


def _col_groups(x, op):
    rows, cols = x.shape
    return op(x.reshape(rows // SUBLANES, SUBLANES, cols), axis=0)


def _attn_body(q_ref, kl_ref, kc_ref, vl_ref, vc_ref, o_ref, s_ref):
    s_len = kl_ref.shape[1]
    c_len = kc_ref.shape[1]
    tk = min(ATTN_TK, s_len)
    chunks = [(kc_ref, vc_ref, 0, c_len, 0)]
    chunks += [(kl_ref, vl_ref, ki * tk, tk, c_len + ki * tk) for ki in range(s_len // tk)]
    maxes = []
    for hd in range(2):
        lanes = slice(hd * HEAD_PAD, (hd + 1) * HEAD_PAD)
        q = q_ref[0, :, lanes]
        m8 = None
        for k_ref, _, r0, rows, off in chunks:
            sk = _dot_nt(k_ref[0, r0:r0 + rows, lanes], q)
            s_ref[hd, off:off + rows, :] = sk
            g = _col_groups(sk, jnp.max)
            m8 = g if m8 is None else jnp.maximum(m8, g)
        maxes.append(jnp.max(m8, axis=0, keepdims=True))
    outs = []
    for hd in range(2):
        vrows = slice(hd * V_HEAD_DIM, (hd + 1) * V_HEAD_DIM)
        l8 = acc = None
        for _, v_ref, r0, rows, off in chunks:
            p = jnp.exp2(s_ref[hd, off:off + rows, :] - maxes[hd])
            g = _col_groups(p, jnp.sum)
            d = _dot(v_ref[0, vrows, r0:r0 + rows], p.astype(BF16))
            l8 = g if l8 is None else l8 + g
            acc = d if acc is None else acc + d
        outs.append(acc / jnp.sum(l8, axis=0, keepdims=True))
    o_ref[0] = jnp.concatenate(outs, axis=0).T.astype(BF16)


def _attn_call(q, k_l, k_c, vt_l, vt_c):
    b, s, _ = q.shape
    c_len = k_c.shape[1]
    tq = min(ATTN_TQ, s)
    pairs = N_HEADS // 2
    return pl.pallas_call(
        _attn_body,
        grid=(b, pairs, s // tq),
        in_specs=[
            pl.BlockSpec((1, tq, 2 * HEAD_PAD), lambda i, j, t: (i, t, j)),
            pl.BlockSpec((1, s, 2 * HEAD_PAD), lambda i, j, t: (i, 0, j)),
            pl.BlockSpec((1, c_len, 2 * HEAD_PAD), lambda i, j, t: (i, 0, j)),
            pl.BlockSpec((1, 2 * V_HEAD_DIM, s), lambda i, j, t: (i, j, 0)),
            pl.BlockSpec((1, 2 * V_HEAD_DIM, c_len), lambda i, j, t: (i, j, 0)),
        ],
        out_specs=pl.BlockSpec((1, tq, 2 * V_HEAD_DIM), lambda i, j, t: (i, t, j)),
        out_shape=jax.ShapeDtypeStruct((b, s, N_HEADS * V_HEAD_DIM), BF16),
        scratch_shapes=[pltpu.VMEM((2, c_len + s, tq), F32)],
        compiler_params=_params(("parallel", "parallel", "parallel")),
        name="attn",
    )(q, k_l, k_c, vt_l, vt_c)


def _merge_body(attn_ref, m_ref, gl_ref, x_ref, g1_ref, bg_ref, woa_ref, wol_ref, wout_ref, o_ref):
    tm = x_ref.shape[1]
    sub = tm // MERGE_SUBTILES
    rows = [slice(i * sub, (i + 1) * sub) for i in range(MERGE_SUBTILES)]
    ys = [(_dot(attn_ref[0, r, :], woa_ref[...]), _dot(m_ref[0, r, :], wol_ref[...])) for r in rows]
    for r, (y_a, y_b) in zip(rows, ys):
        gates = 0.5 + 0.5 * jnp.tanh(0.5 * (gl_ref[0, r, :] + bg_ref[...]))
        mix = gates[:, :D_MODEL] * y_a + gates[:, D_MODEL:] * y_b
        o_ref[0, r, :] = x_ref[0, r, :] + g1_ref[0] * _dot(mix.astype(BF16), wout_ref[...])


def _merge_call(attn, m, gl, x, g1, w):
    b, s, _ = x.shape
    tm = min(MERGE_TM, s)
    row = lambda width: pl.BlockSpec((1, tm, width), lambda i, j: (i, j, 0))
    modrow = pl.BlockSpec((1, 1, D_MODEL), lambda i, j: (i, 0, 0))
    consts = [w['b_gate'], w['o_attn'], w['o_lru'], w['out']]
    return pl.pallas_call(
        _merge_body,
        grid=(b, s // tm),
        in_specs=[row(N_HEADS * V_HEAD_DIM), row(LRU_WIDTH), row(2 * D_MODEL), row(D_MODEL), modrow]
        + [_const_spec(a.shape) for a in consts],
        out_specs=row(D_MODEL),
        out_shape=jax.ShapeDtypeStruct((b, s, D_MODEL), F32),
        compiler_params=_params(("parallel", "parallel")),
        name="merge",
    )(attn, m, gl, x, g1, *consts)


def _ffn_body(xp_ref, x_ref, xn_ref, sh_ref, sc_ref, g2_ref, n2_ref, fg_ref, wup_ref, cw_ref, cb_ref,
              wdn_ref, o_ref, f_ref):
    j = pl.program_id(1)
    nj = pl.num_programs(1)
    tm = x_ref.shape[1]
    x = x_ref[0]
    xe = jnp.concatenate([xp_ref[0], x, xn_ref[0]], axis=0)
    hf = _rms(xe, n2_ref[...]) * (1.0 + sc_ref[0]) + sh_ref[0]
    keep_top = (j > 0).astype(F32)
    keep_bot = (j < nj - 1).astype(F32)
    ht = hf[HALO:HALO + tm]
    he = jnp.concatenate([hf[:HALO] * keep_top, ht, hf[HALO + tm:] * keep_bot], axis=0).astype(BF16)
    ht = ht.astype(BF16)
    ext = tm + 2 * HALO
    for ci in range(FFN_DIM // FFN_FC):
        cols = slice(ci * FFN_FC, (ci + 1) * FFN_FC)
        gcols = slice(FFN_DIM + ci * FFN_FC, FFN_DIM + (ci + 1) * FFN_FC)
        a = _dot(he, wup_ref[:, cols])
        cw = cw_ref[:, cols]
        conv = (cb_ref[:, cols] + pltpu.roll(a, 1, 0)[HALO:HALO + tm] * cw[0:1]
                + a[HALO:HALO + tm] * cw[1:2]
                + pltpu.roll(a, ext - 1, 0)[HALO:HALO + tm] * cw[2:3])
        g = _dot(ht, wup_ref[:, gcols])
        f_ref[:, cols] = (conv * jax.nn.sigmoid(conv) * g).astype(BF16)
    y = x + g2_ref[0] * _dot(f_ref[...], wdn_ref[...])
    o_ref[0] = _rms(y, fg_ref[...])


def _ffn_call(x1, sh2, sc2, g2, w):
    b, s, _ = x1.shape
    tm = min(FFN_TM, s)
    per = tm // HALO
    nblk = s // HALO
    row = pl.BlockSpec((1, tm, D_MODEL), lambda i, j: (i, j, 0))
    prev = pl.BlockSpec((1, HALO, D_MODEL), lambda i, j: (i, jnp.maximum(j * per - 1, 0), 0))
    nxt = pl.BlockSpec((1, HALO, D_MODEL), lambda i, j: (i, jnp.minimum((j + 1) * per, nblk - 1), 0))
    modrow = pl.BlockSpec((1, 1, D_MODEL), lambda i, j: (i, 0, 0))
    consts = [w['norm2_g'], w['final_g'], w['up'], w['ffn_conv_w'], w['ffn_conv_b'], w['down']]
    return pl.pallas_call(
        _ffn_body,
        grid=(b, s // tm),
        in_specs=[prev, row, nxt, modrow, modrow, modrow] + [_const_spec(a.shape) for a in consts],
        out_specs=row,
        out_shape=jax.ShapeDtypeStruct((b, s, D_MODEL), F32),
        scratch_shapes=[pltpu.VMEM((tm, FFN_DIM), BF16)],
        compiler_params=_params(("parallel", "parallel")),
        name="ffn",
    )(x1, x1, x1, sh2, sc2, g2, *consts)


def _rope_tables(n):
    rows = n // GRID_W
    row_ids = jnp.repeat(jnp.arange(rows), GRID_W).astype(F32)
    col_ids = jnp.tile(jnp.arange(GRID_W), rows).astype(F32)
    axis_dim = QK_ROPE_DIM // 2
    inv = 1.0 / (ROPE_BASE ** (jnp.arange(0, axis_dim, 2, dtype=F32) / axis_dim))
    ang = jnp.concatenate([row_ids[:, None] * inv, col_ids[:, None] * inv], axis=-1)
    cos, sin = jnp.cos(ang), jnp.sin(ang)
    ones = lambda w_: jnp.ones((n, w_), F32)
    zeros = lambda w_: jnp.zeros((n, w_), F32)
    tail = HEAD_PAD - QK_DIM
    c = jnp.concatenate([ones(QK_NOPE_DIM), cos, cos, ones(tail)], axis=-1)
    s1 = jnp.concatenate([zeros(QK_NOPE_DIM), -sin, zeros(ROPE_HALF + tail)], axis=-1)
    s2 = jnp.concatenate([zeros(QK_NOPE_DIM + ROPE_HALF), sin, zeros(tail)], axis=-1)
    return c, s1, s2


def _identity_tables(n):
    return (jnp.ones((n, HEAD_PAD), F32), jnp.zeros((n, HEAD_PAD), F32), jnp.zeros((n, HEAD_PAD), F32))


def _prep_weights(w_in, q_norm_g, kv_norm_g, w_uq, w_ukv, w_o_attn, lru_conv_w, lru_conv_b, lru_w_a,
                  lru_b_a, lru_w_x, lru_b_x, lru_lambda, w_o_lru, w_out, b_gate, norm2_g, w_up,
                  ffn_conv_w, ffn_conv_b, w_down, final_g):
    w = {}
    w['q'] = w_in[:, :OFF_KV].astype(BF16)
    w['kv'] = w_in[:, OFF_KV:OFF_KR].astype(BF16)
    w['kr'] = jnp.pad(w_in[:, OFF_KR:OFF_XB], ((0, 0), (QK_NOPE_DIM, HEAD_PAD - QK_DIM))).astype(BF16)
    w['xb'] = w_in[:, OFF_XB:OFF_YB].astype(BF16)
    w['yb'] = w_in[:, OFF_YB:OFF_G].astype(BF16)
    w['gl'] = w_in[:, OFF_G:].astype(BF16)
    w['gq'] = q_norm_g[None, :]
    w['gkv'] = kv_norm_g[None, :]
    uq = w_uq.reshape(Q_LORA_RANK, N_HEADS, QK_DIM)
    w['uq'] = jnp.pad(uq, ((0, 0), (0, 0), (0, HEAD_PAD - QK_DIM))).reshape(Q_LORA_RANK, -1).astype(BF16)
    ukv = w_ukv.reshape(KV_LORA_RANK, N_HEADS, QK_NOPE_DIM + V_HEAD_DIM)
    w['uk'] = jnp.pad(ukv[..., :QK_NOPE_DIM],
                      ((0, 0), (0, 0), (0, HEAD_PAD - QK_NOPE_DIM))).reshape(KV_LORA_RANK, -1).astype(BF16)
    w['uv'] = ukv[..., QK_NOPE_DIM:].reshape(KV_LORA_RANK, -1).T.astype(BF16)
    w['conv_w'] = lru_conv_w
    w['conv_b'] = lru_conv_b[None, :]
    gw = 0.5 * jnp.concatenate([lru_w_a[0], lru_w_x[0], lru_w_a[1], lru_w_x[1]], axis=-1)
    gb = 0.5 * jnp.stack([lru_b_a[0], lru_b_x[0], lru_b_a[1], lru_b_x[1]], axis=0)
    gb = gb.reshape(4, LRU_BLOCKS, LRU_BLOCK_W).transpose(1, 0, 2).reshape(LRU_BLOCKS, 1, -1)
    terms = []
    for _ in range(GATE_BIAS_ROWS):
        t = gb * VELTKAMP_8BIT
        hi = t - (t - gb)
        terms.append(hi)
        gb = gb - hi
    pad = jnp.zeros((LRU_BLOCKS, LANES - GATE_BIAS_ROWS, 4 * LRU_BLOCK_W), F32)
    w['gate_w'] = jnp.concatenate([gw] + terms + [pad], axis=1).astype(BF16)
    w['lam'] = lru_lambda.reshape(2, LRU_BLOCKS, LRU_BLOCK_W).transpose(1, 0, 2)
    w['o_attn'] = w_o_attn.astype(BF16)
    w['o_lru'] = w_o_lru.astype(BF16)
    w['out'] = w_out.astype(BF16)
    w['b_gate'] = b_gate[None, :]
    w['norm2_g'] = norm2_g[None, :]
    w['final_g'] = final_g[None, :]
    w['up'] = w_up.astype(BF16)
    w['ffn_conv_w'] = ffn_conv_w
    w['ffn_conv_b'] = ffn_conv_b[None, :]
    w['down'] = w_down.astype(BF16)
    return w


def kernel(x, c, ctx, c_ctx, w_mod, b_mod, norm1_g, w_in, b_gate, q_norm_g, kv_norm_g, w_uq, w_ukv,
           w_o_attn, lru_conv_w, lru_conv_b, lru_w_a, lru_b_a, lru_w_x, lru_b_x, lru_lambda, w_o_lru,
           w_out, norm2_g, w_up, ffn_conv_w, ffn_conv_b, w_down, final_g):
    assert w_mod.shape[0] == 1, "single-layer block"
    b, s, _ = x.shape
    c_len = ctx.shape[1]
    assert b + 1 <= MOD_ROWS

    w = _prep_weights(w_in[0], q_norm_g[0], kv_norm_g[0], w_uq[0], w_ukv[0], w_o_attn[0], lru_conv_w[0],
                      lru_conv_b[0], lru_w_a[0], lru_b_a[0], lru_w_x[0], lru_b_x[0], lru_lambda[0],
                      w_o_lru[0], w_out[0], b_gate[0], norm2_g[0], w_up[0], ffn_conv_w[0],
                      ffn_conv_b[0], w_down[0], final_g)
    g1n = norm1_g[0][None, :]

    cc = jnp.concatenate([c, c_ctx[None, :], jnp.zeros((MOD_ROWS - b - 1, D_MODEL), F32)], axis=0)
    mod = _mod_call(cc, w_mod, b_mod)
    mod_l = mod[:b].reshape(b, 1, 6, D_MODEL)
    sh1, sc1, g1, sh2, sc2, g2 = (mod_l[:, :, i] for i in range(6))
    mod_c = jnp.broadcast_to(mod[b].reshape(1, 1, 6, D_MODEL), (b, 1, 6, D_MODEL))
    sh1c, sc1c = mod_c[:, :, 0], mod_c[:, :, 1]

    k_c, v_c, xb_c = _proj_call(False, ctx, sh1c, sc1c, g1n, w, _identity_tables(c_len))
    states = _lru_call(False, xb_c, w)

    k_l, v_l, xb_l, q_l, yb_l, gl_l = _proj_call(True, x, sh1, sc1, g1n, w, _rope_tables(s))
    m = _lru_call(True, xb_l, w, yb=yb_l, h0=states)
    attn = _attn_call(q_l, k_l, k_c, v_l, v_c)
    x1 = _merge_call(attn, m, gl_l, x, g1, w)
    return _ffn_call(x1, sh2, sc2, g2, w)
```

```python
import functools

import jax
import jax.numpy as jnp
from jax import lax
from jax.experimental import pallas as pl
from jax.experimental.pallas import tpu as pltpu

F32 = jnp.float32
BF16 = jnp.bfloat16

D_MODEL = 1024
GRID_W = 64
N_HEADS = 8
QK_NOPE_DIM = 64
QK_ROPE_DIM = 32
ROPE_HALF = QK_ROPE_DIM // 2
V_HEAD_DIM = 64
QK_DIM = QK_NOPE_DIM + QK_ROPE_DIM
HEAD_PAD = 128
Q_LORA_RANK = 384
KV_LORA_RANK = 256
ROPE_BASE = 10000.0
LRU_WIDTH = 1280
LRU_BLOCKS = 10
LRU_BLOCK_W = LRU_WIDTH // LRU_BLOCKS
LRU_CONV_W = 4
LRU_C = 8.0
FFN_DIM = 2816
FFN_CONV_W = 3
EPS = 1e-6
TINY = 1e-30
OFF_KV = Q_LORA_RANK
OFF_KR = OFF_KV + KV_LORA_RANK
OFF_XB = OFF_KR + QK_ROPE_DIM
OFF_YB = OFF_XB + LRU_WIDTH
OFF_G = OFF_YB + LRU_WIDTH
IN_Q = 0
IN_KV = IN_Q + Q_LORA_RANK
IN_KR = IN_KV + KV_LORA_RANK
IN_XB = IN_KR + HEAD_PAD
IN_YB = IN_XB + LRU_WIDTH
IN_GL = IN_YB + LRU_WIDTH
IN_END = IN_GL + 2 * D_MODEL

SUBLANES = 8
LANES = 128
VMEM_LIMIT = 56 * 1024 * 1024

MOD_ROWS = 8
MOD_TN = 768
PROJ_TM = 512
LRU_TC = 256
LRU_UNROLL = 16
LRU_FIX_ROWS = 64
GATE_BIAS_ROWS = 3
VELTKAMP_8BIT = 65537.0
LOG2_E = 1.4426950408889634
ATTN_TQ = 512
ATTN_SUB = 256
ATTN_TK = 512
MERGE_TM = 512
MERGE_SUBTILES = 2
FFN_TM = 512
FFN_FC = 256
HALO = SUBLANES


def _dot(a, b):
    return jnp.dot(a, b, preferred_element_type=F32)


def _dot_nt(a, b):
    return lax.dot_general(a, b, (((1,), (1,)), ((), ())), preferred_element_type=F32)


def _rms(x, g):
    return x * lax.rsqrt(jnp.mean(x * x, axis=-1, keepdims=True) + EPS) * g


def _const_spec(shape):
    nd = len(shape)
    return pl.BlockSpec(shape, lambda *_: (0,) * nd, pipeline_mode=pl.Buffered(1))


def _params(sem):
    return pltpu.CompilerParams(dimension_semantics=sem, vmem_limit_bytes=VMEM_LIMIT)


def _mod_body(c_ref, w_ref, b_ref, o_ref):
    c = c_ref[...]
    s = c * jax.nn.sigmoid(c)
    o_ref[...] = _dot(s.astype(BF16), w_ref[...].astype(BF16)) + b_ref[...]


def _mod_call(cc, w_mod, b_mod):
    n = w_mod.shape[-1]
    return pl.pallas_call(
        _mod_body,
        grid=(n // MOD_TN,),
        in_specs=[
            pl.BlockSpec((MOD_ROWS, D_MODEL), lambda j: (0, 0)),
            pl.BlockSpec((None, D_MODEL, MOD_TN), lambda j: (0, 0, j)),
            pl.BlockSpec((1, MOD_TN), lambda j: (0, j)),
        ],
        out_specs=pl.BlockSpec((MOD_ROWS, MOD_TN), lambda j: (0, j)),
        out_shape=jax.ShapeDtypeStruct((MOD_ROWS, n), F32),
        compiler_params=_params(("arbitrary",)),
        name="mod",
    )(cc, w_mod, b_mod)


def _rope128(t, c, s1, s2):
    return (t * c + pltpu.roll(t, HEAD_PAD - ROPE_HALF, 1) * s1
            + pltpu.roll(t, ROPE_HALF, 1) * s2)


def _proj_body(latent, x_ref, sh_ref, sc_ref, g1_ref, win_ref, gkv_ref, wuk_ref, wuv_ref,
               c_ref, s1_ref, s2_ref, *rest):
    if latent:
        gq_ref, wuq_ref, k_ref, v_ref, xb_ref, q_ref, yb_ref, gl_ref = rest
    else:
        k_ref, v_ref, xb_ref = rest
    x = x_ref[0]
    h = _rms(x, g1_ref[...]) * (1.0 + sc_ref[0]) + sh_ref[0]
    hb = h.astype(BF16)
    c, s1, s2 = c_ref[...], s1_ref[...], s2_ref[...]

    if latent:
        lat = _dot(hb, win_ref[:, IN_Q:IN_XB])
        q_lat = lat[:, :IN_KV - IN_Q]
        lat = lat[:, IN_KV - IN_Q:]
    else:
        lat = _dot(hb, win_ref[:, IN_KV:IN_XB])
    kv_lat, kr_raw = lat[:, :IN_KR - IN_KV], lat[:, IN_KR - IN_KV:]
    xb_ref[0] = _dot(hb, win_ref[:, IN_XB:IN_YB])
    kvn = _rms(kv_lat, gkv_ref[...]).astype(BF16)
    kk = _dot(kvn, wuk_ref[...])
    v_ref[0] = _dot_nt(wuv_ref[...], kvn).astype(BF16)
    if latent:
        qn = _rms(q_lat, gq_ref[...]).astype(BF16)
        qq = _dot(qn, wuq_ref[...])
        yb_ref[0] = _dot(hb, win_ref[:, IN_YB:IN_GL])
        gl_ref[0] = _dot(hb, win_ref[:, IN_GL:IN_END])
    kr = _rope128(kr_raw, c, s1, s2)
    for hd in range(N_HEADS):
        sl = slice(hd * HEAD_PAD, (hd + 1) * HEAD_PAD)
        k_ref[0, :, sl] = (kk[:, sl] + kr).astype(BF16)
    if latent:
        scale = QK_DIM ** -0.5 * LOG2_E
        for hd in range(N_HEADS):
            sl = slice(hd * HEAD_PAD, (hd + 1) * HEAD_PAD)
            q_ref[0, :, sl] = (_rope128(qq[:, sl], c, s1, s2) * scale).astype(BF16)


def _proj_call(latent, x, sh, sc, g1, w, tabs):
    b, s, _ = x.shape
    tm = min(PROJ_TM, s)
    row = lambda width: pl.BlockSpec((1, tm, width), lambda i, j: (j, i, 0))
    modrow = pl.BlockSpec((1, 1, D_MODEL), lambda i, j: (j, 0, 0))
    tab = pl.BlockSpec((tm, HEAD_PAD), lambda i, j: (i, 0))
    ins = [x, sh, sc, g1, w['in'], w['gkv'], w['uk'], w['uv'], *tabs]
    in_specs = [row(D_MODEL), modrow, modrow] + [_const_spec(a.shape) for a in ins[3:8]] + [tab] * 3
    widths = [N_HEADS * HEAD_PAD, None, LRU_WIDTH]
    dtypes = [BF16, BF16, F32]
    if latent:
        extra = [w['gq'], w['uq']]
        ins += extra
        in_specs += [_const_spec(a.shape) for a in extra]
        widths += [N_HEADS * HEAD_PAD, LRU_WIDTH, 2 * D_MODEL]
        dtypes += [BF16, F32, F32]
    hv = N_HEADS * V_HEAD_DIM
    out_specs = [row(wd) for wd in widths if wd is not None]
    out_shape = [jax.ShapeDtypeStruct((b, s, wd), dt) for wd, dt in zip(widths, dtypes) if wd is not None]
    out_specs.insert(1, pl.BlockSpec((1, hv, tm), lambda i, j: (j, 0, i)))
    out_shape.insert(1, jax.ShapeDtypeStruct((b, hv, s), BF16))
    return pl.pallas_call(
        functools.partial(_proj_body, latent),
        grid=(s // tm, b),
        in_specs=in_specs,
        out_specs=out_specs,
        out_shape=out_shape,
        compiler_params=_params(("parallel", "parallel")),
        name="proj_latent" if latent else "proj_ctx",
    )(*ins)


def _gelu_tanh(x):
    return 0.5 * x * (1.0 + jnp.tanh(0.7978845608028654 * (x + 0.044715 * (x * x * x))))


def _lru_body(latent, xb_ref, cw_ref, cb_ref, wg_ref, lam_ref, *rest):
    if latent:
        yb_ref, h0_ref, out_ref, xpad, af, uf, ab, ub, hfl, pfl, hbl, pbl = rest
    else:
        st_ref, xpad, af, uf, ab, ub, hfl, pfl, hbl, pbl = rest
    s = xb_ref.shape[1]
    seg = s // SUBLANES
    tc = min(LRU_TC, s)
    nc = s // tc
    piece = min(tc, seg)

    zero_halo = jnp.zeros((HALO, LANES), F32)
    xpad[0:HALO, :] = zero_halo
    xpad[s + HALO:s + 2 * HALO, :] = zero_halo

    def copy_chunk(ci, carry):
        r0 = pl.multiple_of(ci * tc, tc)
        xpad[pl.ds(r0 + HALO, tc), :] = xb_ref[0, pl.ds(r0, tc), :]
        return carry

    lax.fori_loop(0, nc, copy_chunk, 0)

    cw = cw_ref[...]
    cb = cb_ref[...]
    wg = wg_ref[0]
    lam = lam_ref[0]
    hcsp = (0.5 * LRU_C) * (jnp.maximum(-lam, 0.0) + jnp.log1p(jnp.exp(-jnp.abs(lam))))
    bias_lhs = jnp.where(lax.broadcasted_iota(jnp.int32, (tc, LANES), 1) < GATE_BIAS_ROWS, 1.0, 0.0).astype(BF16)

    def interleaved(ci):
        pieces = []
        for k in range(tc // piece):
            t0 = ci * tc + k * piece
            j = t0 // seg
            pieces.append((slice(k * piece, (k + 1) * piece),
                           pl.ds((t0 - j * seg) * SUBLANES + j, piece, stride=SUBLANES)))
        return pieces

    def gate_chunk(ci, carry):
        r0 = pl.multiple_of(ci * tc, tc)
        xm2 = xpad[pl.ds(r0 + (HALO - 2), tc), :]
        xm1 = xpad[pl.ds(r0 + (HALO - 1), tc), :]
        x0 = xpad[pl.ds(r0 + HALO, tc), :]
        xp1 = xpad[pl.ds(r0 + (HALO + 1), tc), :]
        xc = cb + xm2 * cw[0:1] + xm1 * cw[1:2] + x0 * cw[2:3] + xp1 * cw[3:4]
        t = jnp.tanh(_dot(jnp.concatenate([xc.astype(BF16), bias_lhs], axis=1), wg))
        hx = 0.5 * xc
        for d, (a_ref, u_ref) in enumerate(((af, uf), (ab, ub))):
            t_r = t[:, (2 * d) * LANES:(2 * d + 1) * LANES]
            t_i = t[:, (2 * d + 1) * LANES:(2 * d + 2) * LANES]
            neg_log_a = hcsp[d:d + 1] + hcsp[d:d + 1] * t_r
            a = jnp.exp2(neg_log_a * (-LOG2_E))
            y = jnp.tanh(neg_log_a) * (1.0 + a * a)
            mult = y * lax.rsqrt(jnp.maximum(y, TINY))
            u = (mult * hx) * (1.0 + t_i)
            for rows, rd in interleaved(ci):
                a_ref[rd, :] = a[rows]
                u_ref[rd, :] = u[rows]
        return carry

    lax.fori_loop(0, nc, gate_chunk, 0, unroll=min(2, nc))

    def scan_step(i, carry):
        hf, pf, hb, pb = carry
        fwd = pl.ds(pl.multiple_of(i * SUBLANES, SUBLANES), SUBLANES)
        bwd = pl.ds(pl.multiple_of((seg - 1 - i) * SUBLANES, SUBLANES), SUBLANES)
        a = af[fwd, :]
        hf = a * hf + uf[fwd, :]
        pf = a * pf
        hfl[fwd, :] = hf
        pfl[fwd, :] = pf
        a = ab[bwd, :]
        hb = a * hb + ub[bwd, :]
        pb = a * pb
        hbl[bwd, :] = hb
        pbl[bwd, :] = pb
        return hf, pf, hb, pb

    zeros = jnp.zeros((SUBLANES, LANES), F32)
    ones = jnp.ones((SUBLANES, LANES), F32)
    hf, pf, hb, pb = lax.fori_loop(0, seg, scan_step, (zeros, ones, zeros, ones), unroll=LRU_UNROLL)

    if latent:
        h0 = h0_ref[0]
        cf, cbk = h0[0:1], h0[1:2]
    else:
        cf = cbk = jnp.zeros((1, LANES), F32)
    cfs = []
    for j in range(SUBLANES):
        cfs.append(cf)
        cf = hf[j:j + 1] + pf[j:j + 1] * cf
    cbs = [None] * SUBLANES
    for j in reversed(range(SUBLANES)):
        cbs[j] = cbk
        cbk = hb[j:j + 1] + pb[j:j + 1] * cbk

    if latent:
        fix_rows = min(LRU_FIX_ROWS, s)
        cf_all = jnp.tile(jnp.concatenate(cfs, axis=0), (fix_rows // SUBLANES, 1))
        cb_all = jnp.tile(jnp.concatenate(cbs, axis=0), (fix_rows // SUBLANES, 1))

        def fix_chunk(ci, carry):
            rows = pl.ds(pl.multiple_of(ci * fix_rows, fix_rows), fix_rows)
            af[rows, :] = (hfl[rows, :] + pfl[rows, :] * cf_all) + (hbl[rows, :] + pbl[rows, :] * cb_all)
            return carry

        lax.fori_loop(0, s // fix_rows, fix_chunk, 0, unroll=2)

        def out_chunk(ci, carry):
            r0 = pl.multiple_of(ci * tc, tc)
            hsum = jnp.concatenate([af[rd, :] for _, rd in interleaved(ci)], axis=0)
            out_ref[0, pl.ds(r0, tc), :] = (hsum * _gelu_tanh(yb_ref[0, pl.ds(r0, tc), :])).astype(BF16)
            return carry

        lax.fori_loop(0, nc, out_chunk, 0)
    else:
        st_ref[0] = jnp.concatenate([cf, cbk], axis=0)


def _lru_call(latent, xb, w, yb=None, h0=None):
    b, s, _ = xb.shape
    seq = pl.BlockSpec((1, s, LANES), lambda i, j: (i, 0, j))
    st = pl.BlockSpec((1, 2, LANES), lambda i, j: (i, 0, j))
    in_specs = [
        seq,
        pl.BlockSpec((LRU_CONV_W, LANES), lambda i, j: (0, j)),
        pl.BlockSpec((1, LANES), lambda i, j: (0, j)),
        pl.BlockSpec((1, 2 * LANES, 4 * LANES), lambda i, j: (j, 0, 0)),
        pl.BlockSpec((1, 2, LANES), lambda i, j: (j, 0, 0)),
    ]
    ins = [xb, w['conv_w'], w['conv_b'], w['gate_w'], w['lam']]
    scratch = [pltpu.VMEM((s + 2 * HALO, LANES), F32)] + [pltpu.VMEM((s, LANES), F32)] * 8
    if latent:
        ins += [yb, h0]
        in_specs += [seq, st]
        out_specs = seq
        out_shape = jax.ShapeDtypeStruct((b, s, LRU_WIDTH), BF16)
    else:
        out_specs = st
        out_shape = jax.ShapeDtypeStruct((b, 2, LRU_WIDTH), F32)
    return pl.pallas_call(
        functools.partial(_lru_body, latent),
        grid=(b, LRU_BLOCKS),
        in_specs=in_specs,
        out_specs=out_specs,
        out_shape=out_shape,
        scratch_shapes=scratch,
        compiler_params=_params(("parallel", "parallel")),
        name="lru_latent" if latent else "lru_ctx",
    )(*ins)


def _col_groups(x, op):
    rows, cols = x.shape
    return op(x.reshape(rows // SUBLANES, SUBLANES, cols), axis=0)


def _attn_body(q_ref, kl_ref, kc_ref, vl_ref, vc_ref, o_ref, *s_refs):
    s_len = kl_ref.shape[1]
    c_len = kc_ref.shape[1]
    tk = min(ATTN_TK, s_len)
    chunks = [(kc_ref, vc_ref, 0, c_len, 0)]
    chunks += [(kl_ref, vl_ref, ki * tk, tk, c_len + ki * tk) for ki in range(s_len // tk)]
    sub = s_refs[0].shape[1]
    n_sub = q_ref.shape[1] // sub
    units = [dict(qrows=slice(qt * sub, (qt + 1) * sub), hd=hd, m8=None, acc=None)
             for qt in range(n_sub) for hd in range(2)]
    for u, s_ref in zip(units, s_refs):
        u['s'] = s_ref
        u['lanes'] = slice(u['hd'] * HEAD_PAD, (u['hd'] + 1) * HEAD_PAD)
        u['vrows'] = slice(u['hd'] * V_HEAD_DIM, (u['hd'] + 1) * V_HEAD_DIM)
        u['q'] = q_ref[0, u['qrows'], u['lanes']]

    for u in units:
        for k_ref, _, r0, rows, off in chunks:
            sk = _dot_nt(k_ref[0, r0:r0 + rows, u['lanes']], u['q'])
            u['s'][off:off + rows, :] = sk
            g = _col_groups(sk, jnp.max)
            u['m8'] = g if u['m8'] is None else jnp.maximum(u['m8'], g)
        u['m'] = jnp.max(u['m8'], axis=0, keepdims=True)
    outs = []
    for u in units:
        l8 = None
        for _, v_ref, r0, rows, off in chunks:
            p = jnp.exp2(u['s'][off:off + rows, :] - u['m'])
            g = _col_groups(p, jnp.sum)
            d = _dot(v_ref[0, u['vrows'], r0:r0 + rows], p.astype(BF16))
            l8 = g if l8 is None else l8 + g
            u['acc'] = d if u['acc'] is None else u['acc'] + d
        outs.append(u['acc'] / jnp.sum(l8, axis=0, keepdims=True))
    for qt in range(n_sub):
        o_ref[0, qt * sub:(qt + 1) * sub, :] = (
            jnp.concatenate(outs[2 * qt:2 * qt + 2], axis=0).T.astype(BF16))


def _attn_call(q, k_l, k_c, vt_l, vt_c):
    b, s, _ = q.shape
    c_len = k_c.shape[1]
    tq = min(ATTN_TQ, s)
    sub = min(ATTN_SUB, tq)
    pairs = N_HEADS // 2
    return pl.pallas_call(
        _attn_body,
        grid=(b, pairs, s // tq),
        in_specs=[
            pl.BlockSpec((1, tq, 2 * HEAD_PAD), lambda i, j, t: (i, t, j)),
            pl.BlockSpec((1, s, 2 * HEAD_PAD), lambda i, j, t: (i, 0, j)),
            pl.BlockSpec((1, c_len, 2 * HEAD_PAD), lambda i, j, t: (i, 0, j)),
            pl.BlockSpec((1, 2 * V_HEAD_DIM, s), lambda i, j, t: (i, j, 0)),
            pl.BlockSpec((1, 2 * V_HEAD_DIM, c_len), lambda i, j, t: (i, j, 0)),
        ],
        out_specs=pl.BlockSpec((1, tq, 2 * V_HEAD_DIM), lambda i, j, t: (i, t, j)),
        out_shape=jax.ShapeDtypeStruct((b, s, N_HEADS * V_HEAD_DIM), BF16),
        scratch_shapes=[pltpu.VMEM((c_len + s, sub), F32)] * (2 * (tq // sub)),
        compiler_params=_params(("parallel", "parallel", "parallel")),
        name="attn",
    )(q, k_l, k_c, vt_l, vt_c)


def _merge_body(attn_ref, m_ref, gl_ref, x_ref, g1_ref, bg_ref, woa_ref, wol_ref, wout_ref, o_ref):
    tm = x_ref.shape[1]
    sub = tm // MERGE_SUBTILES
    rows = [slice(i * sub, (i + 1) * sub) for i in range(MERGE_SUBTILES)]
    ys = [(_dot(attn_ref[0, r, :], woa_ref[...]), _dot(m_ref[0, r, :], wol_ref[...])) for r in rows]
    for r, (y_a, y_b) in zip(rows, ys):
        gates = 0.5 + 0.5 * jnp.tanh(0.5 * (gl_ref[0, r, :] + bg_ref[...]))
        mix = gates[:, :D_MODEL] * y_a + gates[:, D_MODEL:] * y_b
        o_ref[0, r, :] = x_ref[0, r, :] + g1_ref[0] * _dot(mix.astype(BF16), wout_ref[...])


def _merge_call(attn, m, gl, x, g1, w):
    b, s, _ = x.shape
    tm = min(MERGE_TM, s)
    row = lambda width: pl.BlockSpec((1, tm, width), lambda i, j: (i, j, 0))
    modrow = pl.BlockSpec((1, 1, D_MODEL), lambda i, j: (i, 0, 0))
    consts = [w['b_gate'], w['o_attn'], w['o_lru'], w['out']]
    return pl.pallas_call(
        _merge_body,
        grid=(b, s // tm),
        in_specs=[row(N_HEADS * V_HEAD_DIM), row(LRU_WIDTH), row(2 * D_MODEL), row(D_MODEL), modrow]
        + [_const_spec(a.shape) for a in consts],
        out_specs=row(D_MODEL),
        out_shape=jax.ShapeDtypeStruct((b, s, D_MODEL), F32),
        compiler_params=_params(("parallel", "parallel")),
        name="merge",
    )(attn, m, gl, x, g1, *consts)


def _ffn_body(xp_ref, x_ref, xn_ref, sh_ref, sc_ref, g2_ref, n2_ref, fg_ref, wup_ref, cw_ref, cb_ref,
              wdn_ref, o_ref, f_ref):
    j = pl.program_id(1)
    nj = pl.num_programs(1)
    tm = x_ref.shape[1]
    x = x_ref[0]
    xe = jnp.concatenate([xp_ref[0], x, xn_ref[0]], axis=0)
    hf = _rms(xe, n2_ref[...]) * (1.0 + sc_ref[0]) + sh_ref[0]
    keep_top = (j > 0).astype(F32)
    keep_bot = (j < nj - 1).astype(F32)
    ht = hf[HALO:HALO + tm]
    he = jnp.concatenate([hf[:HALO] * keep_top, ht, hf[HALO + tm:] * keep_bot], axis=0).astype(BF16)
    ht = ht.astype(BF16)
    ext = tm + 2 * HALO
    for ci in range(FFN_DIM // FFN_FC):
        cols = slice(ci * FFN_FC, (ci + 1) * FFN_FC)
        gcols = slice(FFN_DIM + ci * FFN_FC, FFN_DIM + (ci + 1) * FFN_FC)
        a = _dot(he, wup_ref[:, cols])
        cw = cw_ref[:, cols]
        conv = (cb_ref[:, cols] + pltpu.roll(a, 1, 0)[HALO:HALO + tm] * cw[0:1]
                + a[HALO:HALO + tm] * cw[1:2]
                + pltpu.roll(a, ext - 1, 0)[HALO:HALO + tm] * cw[2:3])
        g = _dot(ht, wup_ref[:, gcols])
        f_ref[:, cols] = (conv * jax.nn.sigmoid(conv) * g).astype(BF16)
    y = x + g2_ref[0] * _dot(f_ref[...], wdn_ref[...])
    o_ref[0] = _rms(y, fg_ref[...])


def _ffn_call(x1, sh2, sc2, g2, w):
    b, s, _ = x1.shape
    tm = min(FFN_TM, s)
    per = tm // HALO
    nblk = s // HALO
    row = pl.BlockSpec((1, tm, D_MODEL), lambda i, j: (i, j, 0))
    prev = pl.BlockSpec((1, HALO, D_MODEL), lambda i, j: (i, jnp.maximum(j * per - 1, 0), 0))
    nxt = pl.BlockSpec((1, HALO, D_MODEL), lambda i, j: (i, jnp.minimum((j + 1) * per, nblk - 1), 0))
    modrow = pl.BlockSpec((1, 1, D_MODEL), lambda i, j: (i, 0, 0))
    consts = [w['norm2_g'], w['final_g'], w['up'], w['ffn_conv_w'], w['ffn_conv_b'], w['down']]
    return pl.pallas_call(
        _ffn_body,
        grid=(b, s // tm),
        in_specs=[prev, row, nxt, modrow, modrow, modrow] + [_const_spec(a.shape) for a in consts],
        out_specs=row,
        out_shape=jax.ShapeDtypeStruct((b, s, D_MODEL), F32),
        scratch_shapes=[pltpu.VMEM((tm, FFN_DIM), BF16)],
        compiler_params=_params(("parallel", "parallel")),
        name="ffn",
    )(x1, x1, x1, sh2, sc2, g2, *consts)


def _rope_tables(n):
    rows = n // GRID_W
    row_ids = jnp.repeat(jnp.arange(rows), GRID_W).astype(F32)
    col_ids = jnp.tile(jnp.arange(GRID_W), rows).astype(F32)
    axis_dim = QK_ROPE_DIM // 2
    inv = 1.0 / (ROPE_BASE ** (jnp.arange(0, axis_dim, 2, dtype=F32) / axis_dim))
    ang = jnp.concatenate([row_ids[:, None] * inv, col_ids[:, None] * inv], axis=-1)
    cos, sin = jnp.cos(ang), jnp.sin(ang)
    ones = lambda w_: jnp.ones((n, w_), F32)
    zeros = lambda w_: jnp.zeros((n, w_), F32)
    tail = HEAD_PAD - QK_DIM
    c = jnp.concatenate([ones(QK_NOPE_DIM), cos, cos, ones(tail)], axis=-1)
    s1 = jnp.concatenate([zeros(QK_NOPE_DIM), -sin, zeros(ROPE_HALF + tail)], axis=-1)
    s2 = jnp.concatenate([zeros(QK_NOPE_DIM + ROPE_HALF), sin, zeros(tail)], axis=-1)
    return c, s1, s2


def _identity_tables(n):
    return (jnp.ones((n, HEAD_PAD), F32), jnp.zeros((n, HEAD_PAD), F32), jnp.zeros((n, HEAD_PAD), F32))


def _prep_weights(w_in, q_norm_g, kv_norm_g, w_uq, w_ukv, w_o_attn, lru_conv_w, lru_conv_b, lru_w_a,
                  lru_b_a, lru_w_x, lru_b_x, lru_lambda, w_o_lru, w_out, b_gate, norm2_g, w_up,
                  ffn_conv_w, ffn_conv_b, w_down, final_g):
    w = {}
    kr = jnp.pad(w_in[:, OFF_KR:OFF_XB], ((0, 0), (QK_NOPE_DIM, HEAD_PAD - QK_DIM)))
    w['in'] = jnp.concatenate([w_in[:, :OFF_KR], kr, w_in[:, OFF_XB:]], axis=1).astype(BF16)
    w['gq'] = q_norm_g[None, :]
    w['gkv'] = kv_norm_g[None, :]
    uq = w_uq.reshape(Q_LORA_RANK, N_HEADS, QK_DIM)
    w['uq'] = jnp.pad(uq, ((0, 0), (0, 0), (0, HEAD_PAD - QK_DIM))).reshape(Q_LORA_RANK, -1).astype(BF16)
    ukv = w_ukv.reshape(KV_LORA_RANK, N_HEADS, QK_NOPE_DIM + V_HEAD_DIM)
    w['uk'] = jnp.pad(ukv[..., :QK_NOPE_DIM],
                      ((0, 0), (0, 0), (0, HEAD_PAD - QK_NOPE_DIM))).reshape(KV_LORA_RANK, -1).astype(BF16)
    w['uv'] = ukv[..., QK_NOPE_DIM:].reshape(KV_LORA_RANK, -1).T.astype(BF16)
    w['conv_w'] = lru_conv_w
    w['conv_b'] = lru_conv_b[None, :]
    gw = 0.5 * jnp.concatenate([lru_w_a[0], lru_w_x[0], lru_w_a[1], lru_w_x[1]], axis=-1)
    gb = 0.5 * jnp.stack([lru_b_a[0], lru_b_x[0], lru_b_a[1], lru_b_x[1]], axis=0)
    gb = gb.reshape(4, LRU_BLOCKS, LRU_BLOCK_W).transpose(1, 0, 2).reshape(LRU_BLOCKS, 1, -1)
    terms = []
    for _ in range(GATE_BIAS_ROWS):
        t = gb * VELTKAMP_8BIT
        hi = t - (t - gb)
        terms.append(hi)
        gb = gb - hi
    bias_rows = jnp.pad(jnp.concatenate(terms, axis=1), ((0, 0), (0, LANES - GATE_BIAS_ROWS), (0, 0)))
    w['gate_w'] = jnp.concatenate([gw, bias_rows], axis=1).astype(BF16)
    w['lam'] = lru_lambda.reshape(2, LRU_BLOCKS, LRU_BLOCK_W).transpose(1, 0, 2)
    w['o_attn'] = w_o_attn.astype(BF16)
    w['o_lru'] = w_o_lru.astype(BF16)
    w['out'] = w_out.astype(BF16)
    w['b_gate'] = b_gate[None, :]
    w['norm2_g'] = norm2_g[None, :]
    w['final_g'] = final_g[None, :]
    w['up'] = w_up.astype(BF16)
    w['ffn_conv_w'] = ffn_conv_w
    w['ffn_conv_b'] = ffn_conv_b[None, :]
    w['down'] = w_down.astype(BF16)
    return w


def kernel(x, c, ctx, c_ctx, w_mod, b_mod, norm1_g, w_in, b_gate, q_norm_g, kv_norm_g, w_uq, w_ukv,
           w_o_attn, lru_conv_w, lru_conv_b, lru_w_a, lru_b_a, lru_w_x, lru_b_x, lru_lambda, w_o_lru,
           w_out, norm2_g, w_up, ffn_conv_w, ffn_conv_b, w_down, final_g):
    assert w_mod.shape[0] == 1, "single-layer block"
    b, s, _ = x.shape
    c_len = ctx.shape[1]
    assert b + 1 <= MOD_ROWS

    w = _prep_weights(w_in[0], q_norm_g[0], kv_norm_g[0], w_uq[0], w_ukv[0], w_o_attn[0], lru_conv_w[0],
                      lru_conv_b[0], lru_w_a[0], lru_b_a[0], lru_w_x[0], lru_b_x[0], lru_lambda[0],
                      w_o_lru[0], w_out[0], b_gate[0], norm2_g[0], w_up[0], ffn_conv_w[0],
                      ffn_conv_b[0], w_down[0], final_g)
    g1n = norm1_g[0][None, :]

    cc = jnp.concatenate([c, c_ctx[None, :], jnp.zeros((MOD_ROWS - b - 1, D_MODEL), F32)], axis=0)
    mod = _mod_call(cc, w_mod, b_mod)
    mod_l = mod[:b].reshape(b, 1, 6, D_MODEL)
    sh1, sc1, g1, sh2, sc2, g2 = (mod_l[:, :, i] for i in range(6))
    mod_c = jnp.broadcast_to(mod[b].reshape(1, 1, 6, D_MODEL), (b, 1, 6, D_MODEL))
    sh1c, sc1c = mod_c[:, :, 0], mod_c[:, :, 1]

    k_c, v_c, xb_c = _proj_call(False, ctx, sh1c, sc1c, g1n, w, _identity_tables(c_len))
    states = _lru_call(False, xb_c, w)

    k_l, v_l, xb_l, q_l, yb_l, gl_l = _proj_call(True, x, sh1, sc1, g1n, w, _rope_tables(s))
    m = _lru_call(True, xb_l, w, yb=yb_l, h0=states)
    attn = _attn_call(q_l, k_l, k_c, v_l, v_c)
    x1 = _merge_call(attn, m, gl_l, x, g1, w)
    return _ffn_call(x1, sh2, sc2, g2, w)
```

```python
import functools

import jax
import jax.numpy as jnp
from jax import lax
from jax.experimental import pallas as pl
from jax.experimental.pallas import tpu as pltpu

F32 = jnp.float32
BF16 = jnp.bfloat16

D_MODEL = 1024
GRID_W = 64
N_HEADS = 8
QK_NOPE_DIM = 64
QK_ROPE_DIM = 32
ROPE_HALF = QK_ROPE_DIM // 2
V_HEAD_DIM = 64
QK_DIM = QK_NOPE_DIM + QK_ROPE_DIM
HEAD_PAD = 128
Q_LORA_RANK = 384
KV_LORA_RANK = 256
ROPE_BASE = 10000.0
LRU_WIDTH = 1280
LRU_BLOCKS = 10
LRU_BLOCK_W = LRU_WIDTH // LRU_BLOCKS
LRU_CONV_W = 4
LRU_C = 8.0
FFN_DIM = 2816
FFN_CONV_W = 3
EPS = 1e-6
TINY = 1e-30
OFF_KV = Q_LORA_RANK
OFF_KR = OFF_KV + KV_LORA_RANK
OFF_XB = OFF_KR + QK_ROPE_DIM
OFF_YB = OFF_XB + LRU_WIDTH
OFF_G = OFF_YB + LRU_WIDTH
IN_Q = 0
IN_KV = IN_Q + Q_LORA_RANK
IN_KR = IN_KV + KV_LORA_RANK
IN_A_END = IN_KR + HEAD_PAD
IN_XB = 0
IN_YB = IN_XB + LRU_WIDTH
IN_GL = IN_YB + LRU_WIDTH
IN_END = IN_GL + 2 * D_MODEL

SUBLANES = 8
VT_ROWS = 128
LANES = 128
VMEM_LIMIT = 56 * 1024 * 1024

MOD_ROWS = 8
MOD_TN = 768
PROJ_TM = 512
LRU_TC = 256
LRU_UNROLL = 16
LRU_FIX_ROWS = 64
GATE_BIAS_ROWS = 3
VELTKAMP_8BIT = 65537.0
LOG2_E = 1.4426950408889634
ATTN_TQ = 512
ATTN_SUB = 256
ATTN_TK = 256
MERGE_TM = 512
MERGE_SUBTILES = 2
FFN_TM = 512
FFN_FC = 256
HALO = SUBLANES


def _dot(a, b):
    return jnp.dot(a, b, preferred_element_type=F32)


def _dot_nt(a, b):
    return lax.dot_general(a, b, (((1,), (1,)), ((), ())), preferred_element_type=F32)


def _rms(x, g):
    return x * lax.rsqrt(jnp.mean(x * x, axis=-1, keepdims=True) + EPS) * g


def _const_spec(shape):
    nd = len(shape)
    return pl.BlockSpec(shape, lambda *_: (0,) * nd, pipeline_mode=pl.Buffered(1))


def _params(sem):
    return pltpu.CompilerParams(dimension_semantics=sem, vmem_limit_bytes=VMEM_LIMIT)


def _mod_body(c_ref, w_ref, b_ref, o_ref):
    c = c_ref[...]
    s = c * jax.nn.sigmoid(c)
    o_ref[...] = _dot(s.astype(BF16), w_ref[...].astype(BF16)) + b_ref[...]


def _mod_call(cc, w_mod, b_mod):
    n = w_mod.shape[-1]
    return pl.pallas_call(
        _mod_body,
        grid=(n // MOD_TN,),
        in_specs=[
            pl.BlockSpec((MOD_ROWS, D_MODEL), lambda j: (0, 0)),
            pl.BlockSpec((None, D_MODEL, MOD_TN), lambda j: (0, 0, j)),
            pl.BlockSpec((1, MOD_TN), lambda j: (0, j)),
        ],
        out_specs=pl.BlockSpec((MOD_ROWS, MOD_TN), lambda j: (0, j)),
        out_shape=jax.ShapeDtypeStruct((MOD_ROWS, n), F32),
        compiler_params=_params(("arbitrary",)),
        name="mod",
    )(cc, w_mod, b_mod)


def _rope128(t, c, s1, s2):
    return (t * c + pltpu.roll(t, HEAD_PAD - ROPE_HALF, 1) * s1
            + pltpu.roll(t, ROPE_HALF, 1) * s2)


def _proj_body(latent, x_ref, sh_ref, sc_ref, g1_ref, wa_ref, wb_ref, gkv_ref, wuk_ref, wuv_ref,
               c_ref, s1_ref, s2_ref, *rest):
    if latent:
        gq_ref, wuq_ref, k_ref, v_ref, xb_ref, q_ref, yb_ref, gl_ref = rest
    else:
        k_ref, v_ref, xb_ref = rest
    x = x_ref[0]
    h = _rms(x, g1_ref[...]) * (1.0 + sc_ref[0]) + sh_ref[0]
    hb = h.astype(BF16)
    c, s1, s2 = c_ref[...], s1_ref[...], s2_ref[...]

    if latent:
        lat = _dot(hb, wa_ref[...])
        q_lat = lat[:, :IN_KV - IN_Q]
        lat = lat[:, IN_KV - IN_Q:]
    else:
        lat = _dot(hb, wa_ref[:, IN_KV:IN_A_END])
    kv_lat, kr_raw = lat[:, :IN_KR - IN_KV], lat[:, IN_KR - IN_KV:]
    xb_ref[0] = _dot(hb, wb_ref[:, IN_XB:IN_YB])
    kvn = _rms(kv_lat, gkv_ref[...]).astype(BF16)
    kk = _dot(kvn, wuk_ref[...])
    vt = _dot_nt(wuv_ref[...], kvn)
    ones = jnp.ones((VT_ROWS - V_HEAD_DIM, vt.shape[1]), F32)
    v_ref[0] = jnp.concatenate([piece for hd in range(N_HEADS)
                                for piece in (vt[hd * V_HEAD_DIM:(hd + 1) * V_HEAD_DIM], ones)],
                               axis=0).astype(BF16)
    if latent:
        qn = _rms(q_lat, gq_ref[...]).astype(BF16)
        qq = _dot(qn, wuq_ref[...])
        yb_ref[0] = _dot(hb, wb_ref[:, IN_YB:IN_GL])
        gl_ref[0] = _dot(hb, wb_ref[:, IN_GL:IN_END])
    kr = _rope128(kr_raw, c, s1, s2)
    for hd in range(N_HEADS):
        sl = slice(hd * HEAD_PAD, (hd + 1) * HEAD_PAD)
        k_ref[0, :, sl] = (kk[:, sl] + kr).astype(BF16)
    if latent:
        scale = QK_DIM ** -0.5 * LOG2_E
        for hd in range(N_HEADS):
            sl = slice(hd * HEAD_PAD, (hd + 1) * HEAD_PAD)
            q_ref[0, :, sl] = (_rope128(qq[:, sl], c, s1, s2) * scale).astype(BF16)


def _proj_call(latent, x, sh, sc, g1, w, tabs):
    b, s, _ = x.shape
    tm = min(PROJ_TM, s)
    row = lambda width: pl.BlockSpec((1, tm, width), lambda i, j: (j, i, 0))
    modrow = pl.BlockSpec((1, 1, D_MODEL), lambda i, j: (j, 0, 0))
    tab = pl.BlockSpec((tm, HEAD_PAD), lambda i, j: (i, 0))
    ins = [x, sh, sc, g1, w['in_a'], w['in_b'], w['gkv'], w['uk'], w['uv'], *tabs]
    in_specs = [row(D_MODEL), modrow, modrow] + [_const_spec(a.shape) for a in ins[3:9]] + [tab] * 3
    widths = [N_HEADS * HEAD_PAD, None, LRU_WIDTH]
    dtypes = [BF16, BF16, F32]
    if latent:
        extra = [w['gq'], w['uq']]
        ins += extra
        in_specs += [_const_spec(a.shape) for a in extra]
        widths += [N_HEADS * HEAD_PAD, LRU_WIDTH, 2 * D_MODEL]
        dtypes += [BF16, F32, F32]
    hv = N_HEADS * VT_ROWS
    out_specs = [row(wd) for wd in widths if wd is not None]
    out_shape = [jax.ShapeDtypeStruct((b, s, wd), dt) for wd, dt in zip(widths, dtypes) if wd is not None]
    out_specs.insert(1, pl.BlockSpec((1, hv, tm), lambda i, j: (j, 0, i)))
    out_shape.insert(1, jax.ShapeDtypeStruct((b, hv, s), BF16))
    return pl.pallas_call(
        functools.partial(_proj_body, latent),
        grid=(s // tm, b),
        in_specs=in_specs,
        out_specs=out_specs,
        out_shape=out_shape,
        compiler_params=_params(("parallel", "parallel")),
        name="proj_latent" if latent else "proj_ctx",
    )(*ins)


def _gelu_tanh(x):
    return 0.5 * x * (1.0 + jnp.tanh(0.7978845608028654 * (x + 0.044715 * (x * x * x))))


def _lru_body(latent, xb_ref, cw_ref, cb_ref, wg_ref, lam_ref, *rest):
    if latent:
        yb_ref, h0_ref, out_ref, xpad, af, uf, ab, ub, hfl, pfl, hbl, pbl = rest
    else:
        st_ref, xpad, af, uf, ab, ub, hfl, pfl, hbl, pbl = rest
    s = xb_ref.shape[1]
    seg = s // SUBLANES
    tc = min(LRU_TC, s)
    nc = s // tc
    piece = min(tc, seg)

    zero_halo = jnp.zeros((HALO, LANES), F32)
    xpad[0:HALO, :] = zero_halo
    xpad[s + HALO:s + 2 * HALO, :] = zero_halo

    def copy_chunk(ci, carry):
        r0 = pl.multiple_of(ci * tc, tc)
        xpad[pl.ds(r0 + HALO, tc), :] = xb_ref[0, pl.ds(r0, tc), :]
        return carry

    lax.fori_loop(0, nc, copy_chunk, 0)

    cw = cw_ref[...]
    cb = cb_ref[...]
    wg = wg_ref[0]
    lam = lam_ref[0]
    hcsp = (0.5 * LRU_C) * (jnp.maximum(-lam, 0.0) + jnp.log1p(jnp.exp(-jnp.abs(lam))))
    bias_lhs = jnp.where(lax.broadcasted_iota(jnp.int32, (tc, LANES), 1) < GATE_BIAS_ROWS, 1.0, 0.0).astype(BF16)

    def interleaved(ci):
        pieces = []
        for k in range(tc // piece):
            t0 = ci * tc + k * piece
            j = t0 // seg
            pieces.append((slice(k * piece, (k + 1) * piece),
                           pl.ds((t0 - j * seg) * SUBLANES + j, piece, stride=SUBLANES)))
        return pieces

    def gate_chunk(ci, carry):
        r0 = pl.multiple_of(ci * tc, tc)
        xm2 = xpad[pl.ds(r0 + (HALO - 2), tc), :]
        xm1 = xpad[pl.ds(r0 + (HALO - 1), tc), :]
        x0 = xpad[pl.ds(r0 + HALO, tc), :]
        xp1 = xpad[pl.ds(r0 + (HALO + 1), tc), :]
        xc = cb + xm2 * cw[0:1] + xm1 * cw[1:2] + x0 * cw[2:3] + xp1 * cw[3:4]
        t = jnp.tanh(_dot(jnp.concatenate([xc.astype(BF16), bias_lhs], axis=1), wg))
        hx = 0.5 * xc
        for d, (a_ref, u_ref) in enumerate(((af, uf), (ab, ub))):
            t_r = t[:, (2 * d) * LANES:(2 * d + 1) * LANES]
            t_i = t[:, (2 * d + 1) * LANES:(2 * d + 2) * LANES]
            neg_log_a = hcsp[d:d + 1] + hcsp[d:d + 1] * t_r
            a = jnp.exp2(neg_log_a * (-LOG2_E))
            y = jnp.tanh(neg_log_a) * (1.0 + a * a)
            mult = y * lax.rsqrt(jnp.maximum(y, TINY))
            u = (mult * hx) * (1.0 + t_i)
            for rows, rd in interleaved(ci):
                a_ref[rd, :] = a[rows]
                u_ref[rd, :] = u[rows]
        return carry

    lax.fori_loop(0, nc, gate_chunk, 0, unroll=min(8, nc))

    def scan_step(i, carry):
        hf, pf, hb, pb = carry
        fwd = pl.ds(pl.multiple_of(i * SUBLANES, SUBLANES), SUBLANES)
        bwd = pl.ds(pl.multiple_of((seg - 1 - i) * SUBLANES, SUBLANES), SUBLANES)
        a = af[fwd, :]
        hf = a * hf + uf[fwd, :]
        pf = a * pf
        hfl[fwd, :] = hf
        pfl[fwd, :] = pf
        a = ab[bwd, :]
        hb = a * hb + ub[bwd, :]
        pb = a * pb
        hbl[bwd, :] = hb
        pbl[bwd, :] = pb
        return hf, pf, hb, pb

    zeros = jnp.zeros((SUBLANES, LANES), F32)
    ones = jnp.ones((SUBLANES, LANES), F32)
    hf, pf, hb, pb = lax.fori_loop(0, seg, scan_step, (zeros, ones, zeros, ones), unroll=LRU_UNROLL)

    if latent:
        h0 = h0_ref[0]
        cf, cbk = h0[0:1], h0[1:2]
    else:
        cf = cbk = jnp.zeros((1, LANES), F32)
    cfs = []
    for j in range(SUBLANES):
        cfs.append(cf)
        cf = hf[j:j + 1] + pf[j:j + 1] * cf
    cbs = [None] * SUBLANES
    for j in reversed(range(SUBLANES)):
        cbs[j] = cbk
        cbk = hb[j:j + 1] + pb[j:j + 1] * cbk

    if latent:
        fix_rows = min(LRU_FIX_ROWS, s)
        cf_all = jnp.tile(jnp.concatenate(cfs, axis=0), (fix_rows // SUBLANES, 1))
        cb_all = jnp.tile(jnp.concatenate(cbs, axis=0), (fix_rows // SUBLANES, 1))

        def fix_chunk(ci, carry):
            rows = pl.ds(pl.multiple_of(ci * fix_rows, fix_rows), fix_rows)
            af[rows, :] = (hfl[rows, :] + pfl[rows, :] * cf_all) + (hbl[rows, :] + pbl[rows, :] * cb_all)
            return carry

        lax.fori_loop(0, s // fix_rows, fix_chunk, 0, unroll=2)

        def out_chunk(ci, carry):
            r0 = pl.multiple_of(ci * tc, tc)
            hsum = jnp.concatenate([af[rd, :] for _, rd in interleaved(ci)], axis=0)
            out_ref[0, pl.ds(r0, tc), :] = (hsum * _gelu_tanh(yb_ref[0, pl.ds(r0, tc), :])).astype(BF16)
            return carry

        lax.fori_loop(0, nc, out_chunk, 0)
    else:
        st_ref[0] = jnp.concatenate([cf, cbk], axis=0)


def _lru_call(latent, xb, w, yb=None, h0=None):
    b, s, _ = xb.shape
    seq = pl.BlockSpec((1, s, LANES), lambda i, j: (i, 0, j))
    st = pl.BlockSpec((1, 2, LANES), lambda i, j: (i, 0, j))
    in_specs = [
        seq,
        pl.BlockSpec((LRU_CONV_W, LANES), lambda i, j: (0, j)),
        pl.BlockSpec((1, LANES), lambda i, j: (0, j)),
        pl.BlockSpec((1, 2 * LANES, 4 * LANES), lambda i, j: (j, 0, 0)),
        pl.BlockSpec((1, 2, LANES), lambda i, j: (j, 0, 0)),
    ]
    ins = [xb, w['conv_w'], w['conv_b'], w['gate_w'], w['lam']]
    scratch = [pltpu.VMEM((s + 2 * HALO, LANES), F32)] + [pltpu.VMEM((s, LANES), F32)] * 8
    if latent:
        ins += [yb, h0]
        in_specs += [seq, st]
        out_specs = seq
        out_shape = jax.ShapeDtypeStruct((b, s, LRU_WIDTH), BF16)
    else:
        out_specs = st
        out_shape = jax.ShapeDtypeStruct((b, 2, LRU_WIDTH), F32)
    return pl.pallas_call(
        functools.partial(_lru_body, latent),
        grid=(b, LRU_BLOCKS),
        in_specs=in_specs,
        out_specs=out_specs,
        out_shape=out_shape,
        scratch_shapes=scratch,
        compiler_params=_params(("parallel", "parallel")),
        name="lru_latent" if latent else "lru_ctx",
    )(*ins)


def _col_groups(x, op):
    rows, cols = x.shape
    return op(x.reshape(rows // SUBLANES, SUBLANES, cols), axis=0)


def _attn_body(q_ref, kl_ref, kc_ref, vl_ref, vc_ref, o_ref, p_ref, *s_refs):
    s_len = kl_ref.shape[1]
    c_len = kc_ref.shape[1]
    tk = min(ATTN_TK, s_len)
    chunks = [(kc_ref, vc_ref, 0, c_len, 0)]
    chunks += [(kl_ref, vl_ref, ki * tk, tk, c_len + ki * tk) for ki in range(s_len // tk)]
    sub = s_refs[0].shape[1]
    n_sub = q_ref.shape[1] // sub
    units = [dict(qrows=slice(qt * sub, (qt + 1) * sub), hd=hd, m8=None, l8=None, acc=None)
             for qt in range(n_sub) for hd in range(2)]
    for u, s_ref in zip(units, s_refs):
        u['s'] = s_ref
        u['lanes'] = slice(u['hd'] * HEAD_PAD, (u['hd'] + 1) * HEAD_PAD)
        u['vrows'] = slice(u['hd'] * VT_ROWS, (u['hd'] + 1) * VT_ROWS)
        u['q'] = q_ref[0, u['qrows'], u['lanes']]

    for u in units:
        for k_ref, _, r0, rows, off in chunks:
            sk = _dot_nt(k_ref[0, r0:r0 + rows, u['lanes']], u['q'])
            u['s'][off:off + rows, :] = sk
            g = _col_groups(sk, jnp.max)
            u['m8'] = g if u['m8'] is None else jnp.maximum(u['m8'], g)
        u['m'] = jnp.max(u['m8'], axis=0, keepdims=True)
    outs = []
    for u in units:
        for _, _, _, rows, off in chunks:
            p_ref[off:off + rows, :] = jnp.exp2(u['s'][off:off + rows, :] - u['m']).astype(BF16)
        acc = (_dot(vc_ref[0, u['vrows'], :], p_ref[0:c_len, :])
               + _dot(vl_ref[0, u['vrows'], :], p_ref[c_len:c_len + s_len, :]))
        outs.append(acc[:V_HEAD_DIM] / acc[V_HEAD_DIM:V_HEAD_DIM + 1])
    for qt in range(n_sub):
        o_ref[0, qt * sub:(qt + 1) * sub, :] = (
            jnp.concatenate(outs[2 * qt:2 * qt + 2], axis=0).T.astype(BF16))


def _attn_call(q, k_l, k_c, vt_l, vt_c):
    b, s, _ = q.shape
    c_len = k_c.shape[1]
    tq = min(ATTN_TQ, s)
    sub = min(ATTN_SUB, tq)
    pairs = N_HEADS // 2
    return pl.pallas_call(
        _attn_body,
        grid=(b, pairs, s // tq),
        in_specs=[
            pl.BlockSpec((1, tq, 2 * HEAD_PAD), lambda i, j, t: (i, t, j)),
            pl.BlockSpec((1, s, 2 * HEAD_PAD), lambda i, j, t: (i, 0, j)),
            pl.BlockSpec((1, c_len, 2 * HEAD_PAD), lambda i, j, t: (i, 0, j)),
            pl.BlockSpec((1, 2 * VT_ROWS, s), lambda i, j, t: (i, j, 0)),
            pl.BlockSpec((1, 2 * VT_ROWS, c_len), lambda i, j, t: (i, j, 0)),
        ],
        out_specs=pl.BlockSpec((1, tq, 2 * V_HEAD_DIM), lambda i, j, t: (i, t, j)),
        out_shape=jax.ShapeDtypeStruct((b, s, N_HEADS * V_HEAD_DIM), BF16),
        scratch_shapes=([pltpu.VMEM((c_len + s, sub), BF16)]
                        + [pltpu.VMEM((c_len + s, sub), F32)] * (2 * (tq // sub))),
        compiler_params=_params(("parallel", "parallel", "parallel")),
        name="attn",
    )(q, k_l, k_c, vt_l, vt_c)


def _merge_body(attn_ref, m_ref, gl_ref, x_ref, g1_ref, bg_ref, woa_ref, wol_ref, wout_ref, o_ref):
    tm = x_ref.shape[1]
    sub = tm // MERGE_SUBTILES
    rows = [slice(i * sub, (i + 1) * sub) for i in range(MERGE_SUBTILES)]
    ys = [(_dot(attn_ref[0, r, :], woa_ref[...]), _dot(m_ref[0, r, :], wol_ref[...])) for r in rows]
    for r, (y_a, y_b) in zip(rows, ys):
        gates = 0.5 + 0.5 * jnp.tanh(0.5 * (gl_ref[0, r, :] + bg_ref[...]))
        mix = gates[:, :D_MODEL] * y_a + gates[:, D_MODEL:] * y_b
        o_ref[0, r, :] = x_ref[0, r, :] + g1_ref[0] * _dot(mix.astype(BF16), wout_ref[...])


def _merge_call(attn, m, gl, x, g1, w):
    b, s, _ = x.shape
    tm = min(MERGE_TM, s)
    row = lambda width: pl.BlockSpec((1, tm, width), lambda i, j: (i, j, 0))
    modrow = pl.BlockSpec((1, 1, D_MODEL), lambda i, j: (i, 0, 0))
    consts = [w['b_gate'], w['o_attn'], w['o_lru'], w['out']]
    return pl.pallas_call(
        _merge_body,
        grid=(b, s // tm),
        in_specs=[row(N_HEADS * V_HEAD_DIM), row(LRU_WIDTH), row(2 * D_MODEL), row(D_MODEL), modrow]
        + [_const_spec(a.shape) for a in consts],
        out_specs=row(D_MODEL),
        out_shape=jax.ShapeDtypeStruct((b, s, D_MODEL), F32),
        compiler_params=_params(("parallel", "parallel")),
        name="merge",
    )(attn, m, gl, x, g1, *consts)


def _ffn_body(xp_ref, x_ref, xn_ref, sh_ref, sc_ref, g2_ref, n2_ref, fg_ref, wup_ref, cw_ref, cb_ref,
              wdn_ref, o_ref, f_ref):
    j = pl.program_id(1)
    nj = pl.num_programs(1)
    tm = x_ref.shape[1]
    x = x_ref[0]
    xe = jnp.concatenate([xp_ref[0], x, xn_ref[0]], axis=0)
    hf = _rms(xe, n2_ref[...]) * (1.0 + sc_ref[0]) + sh_ref[0]
    keep_top = (j > 0).astype(F32)
    keep_bot = (j < nj - 1).astype(F32)
    ht = hf[HALO:HALO + tm]
    he = jnp.concatenate([hf[:HALO] * keep_top, ht, hf[HALO + tm:] * keep_bot], axis=0).astype(BF16)
    ht = ht.astype(BF16)
    ext = tm + 2 * HALO
    for ci in range(FFN_DIM // FFN_FC):
        cols = slice(ci * FFN_FC, (ci + 1) * FFN_FC)
        gcols = slice(FFN_DIM + ci * FFN_FC, FFN_DIM + (ci + 1) * FFN_FC)
        a = _dot(he, wup_ref[:, cols])
        cw = cw_ref[:, cols]
        conv = (cb_ref[:, cols] + pltpu.roll(a, 1, 0)[HALO:HALO + tm] * cw[0:1]
                + a[HALO:HALO + tm] * cw[1:2]
                + pltpu.roll(a, ext - 1, 0)[HALO:HALO + tm] * cw[2:3])
        g = _dot(ht, wup_ref[:, gcols])
        f_ref[:, cols] = (conv * jax.nn.sigmoid(conv) * g).astype(BF16)
    y = x + g2_ref[0] * _dot(f_ref[...], wdn_ref[...])
    o_ref[0] = _rms(y, fg_ref[...])


def _ffn_call(x1, sh2, sc2, g2, w):
    b, s, _ = x1.shape
    tm = min(FFN_TM, s)
    per = tm // HALO
    nblk = s // HALO
    row = pl.BlockSpec((1, tm, D_MODEL), lambda i, j: (i, j, 0))
    prev = pl.BlockSpec((1, HALO, D_MODEL), lambda i, j: (i, jnp.maximum(j * per - 1, 0), 0))
    nxt = pl.BlockSpec((1, HALO, D_MODEL), lambda i, j: (i, jnp.minimum((j + 1) * per, nblk - 1), 0))
    modrow = pl.BlockSpec((1, 1, D_MODEL), lambda i, j: (i, 0, 0))
    consts = [w['norm2_g'], w['final_g'], w['up'], w['ffn_conv_w'], w['ffn_conv_b'], w['down']]
    return pl.pallas_call(
        _ffn_body,
        grid=(b, s // tm),
        in_specs=[prev, row, nxt, modrow, modrow, modrow] + [_const_spec(a.shape) for a in consts],
        out_specs=row,
        out_shape=jax.ShapeDtypeStruct((b, s, D_MODEL), F32),
        scratch_shapes=[pltpu.VMEM((tm, FFN_DIM), BF16)],
        compiler_params=_params(("parallel", "parallel")),
        name="ffn",
    )(x1, x1, x1, sh2, sc2, g2, *consts)


def _rope_tables(n):
    rows = n // GRID_W
    row_ids = jnp.repeat(jnp.arange(rows), GRID_W).astype(F32)
    col_ids = jnp.tile(jnp.arange(GRID_W), rows).astype(F32)
    axis_dim = QK_ROPE_DIM // 2
    inv = 1.0 / (ROPE_BASE ** (jnp.arange(0, axis_dim, 2, dtype=F32) / axis_dim))
    ang = jnp.concatenate([row_ids[:, None] * inv, col_ids[:, None] * inv], axis=-1)
    cos, sin = jnp.cos(ang), jnp.sin(ang)
    ones = lambda w_: jnp.ones((n, w_), F32)
    zeros = lambda w_: jnp.zeros((n, w_), F32)
    tail = HEAD_PAD - QK_DIM
    c = jnp.concatenate([ones(QK_NOPE_DIM), cos, cos, ones(tail)], axis=-1)
    s1 = jnp.concatenate([zeros(QK_NOPE_DIM), -sin, zeros(ROPE_HALF + tail)], axis=-1)
    s2 = jnp.concatenate([zeros(QK_NOPE_DIM + ROPE_HALF), sin, zeros(tail)], axis=-1)
    return c, s1, s2


def _identity_tables(n):
    return (jnp.ones((n, HEAD_PAD), F32), jnp.zeros((n, HEAD_PAD), F32), jnp.zeros((n, HEAD_PAD), F32))


def _prep_weights(w_in, q_norm_g, kv_norm_g, w_uq, w_ukv, w_o_attn, lru_conv_w, lru_conv_b, lru_w_a,
                  lru_b_a, lru_w_x, lru_b_x, lru_lambda, w_o_lru, w_out, b_gate, norm2_g, w_up,
                  ffn_conv_w, ffn_conv_b, w_down, final_g):
    w = {}
    kr = jnp.pad(w_in[:, OFF_KR:OFF_XB], ((0, 0), (QK_NOPE_DIM, HEAD_PAD - QK_DIM)))
    w['in_a'] = jnp.concatenate([w_in[:, :OFF_KR], kr], axis=1).astype(BF16)
    w['in_b'] = w_in[:, OFF_XB:].astype(BF16)
    w['gq'] = q_norm_g[None, :]
    w['gkv'] = kv_norm_g[None, :]
    uq = w_uq.reshape(Q_LORA_RANK, N_HEADS, QK_DIM)
    w['uq'] = jnp.pad(uq, ((0, 0), (0, 0), (0, HEAD_PAD - QK_DIM))).reshape(Q_LORA_RANK, -1).astype(BF16)
    ukv = w_ukv.reshape(KV_LORA_RANK, N_HEADS, QK_NOPE_DIM + V_HEAD_DIM)
    w['uk'] = jnp.pad(ukv[..., :QK_NOPE_DIM],
                      ((0, 0), (0, 0), (0, HEAD_PAD - QK_NOPE_DIM))).reshape(KV_LORA_RANK, -1).astype(BF16)
    w['uv'] = ukv[..., QK_NOPE_DIM:].reshape(KV_LORA_RANK, -1).T.astype(BF16)
    w['conv_w'] = lru_conv_w
    w['conv_b'] = lru_conv_b[None, :]
    gw = 0.5 * jnp.concatenate([lru_w_a[0], lru_w_x[0], lru_w_a[1], lru_w_x[1]], axis=-1)
    gb = 0.5 * jnp.stack([lru_b_a[0], lru_b_x[0], lru_b_a[1], lru_b_x[1]], axis=0)
    gb = gb.reshape(4, LRU_BLOCKS, LRU_BLOCK_W).transpose(1, 0, 2).reshape(LRU_BLOCKS, 1, -1)
    terms = []
    for _ in range(GATE_BIAS_ROWS):
        t = gb * VELTKAMP_8BIT
        hi = t - (t - gb)
        terms.append(hi)
        gb = gb - hi
    bias_rows = jnp.pad(jnp.concatenate(terms, axis=1), ((0, 0), (0, LANES - GATE_BIAS_ROWS), (0, 0)))
    w['gate_w'] = jnp.concatenate([gw, bias_rows], axis=1).astype(BF16)
    w['lam'] = lru_lambda.reshape(2, LRU_BLOCKS, LRU_BLOCK_W).transpose(1, 0, 2)
    w['o_attn'] = w_o_attn.astype(BF16)
    w['o_lru'] = w_o_lru.astype(BF16)
    w['out'] = w_out.astype(BF16)
    w['b_gate'] = b_gate[None, :]
    w['norm2_g'] = norm2_g[None, :]
    w['final_g'] = final_g[None, :]
    w['up'] = w_up.astype(BF16)
    w['ffn_conv_w'] = ffn_conv_w
    w['ffn_conv_b'] = ffn_conv_b[None, :]
    w['down'] = w_down.astype(BF16)
    return w


def kernel(x, c, ctx, c_ctx, w_mod, b_mod, norm1_g, w_in, b_gate, q_norm_g, kv_norm_g, w_uq, w_ukv,
           w_o_attn, lru_conv_w, lru_conv_b, lru_w_a, lru_b_a, lru_w_x, lru_b_x, lru_lambda, w_o_lru,
           w_out, norm2_g, w_up, ffn_conv_w, ffn_conv_b, w_down, final_g):
    assert w_mod.shape[0] == 1, "single-layer block"
    b, s, _ = x.shape
    c_len = ctx.shape[1]
    assert b + 1 <= MOD_ROWS

    w = _prep_weights(w_in[0], q_norm_g[0], kv_norm_g[0], w_uq[0], w_ukv[0], w_o_attn[0], lru_conv_w[0],
                      lru_conv_b[0], lru_w_a[0], lru_b_a[0], lru_w_x[0], lru_b_x[0], lru_lambda[0],
                      w_o_lru[0], w_out[0], b_gate[0], norm2_g[0], w_up[0], ffn_conv_w[0],
                      ffn_conv_b[0], w_down[0], final_g)
    g1n = norm1_g[0][None, :]

    cc = jnp.concatenate([c, c_ctx[None, :], jnp.zeros((MOD_ROWS - b - 1, D_MODEL), F32)], axis=0)
    mod = _mod_call(cc, w_mod, b_mod)
    mod_l = mod[:b].reshape(b, 1, 6, D_MODEL)
    sh1, sc1, g1, sh2, sc2, g2 = (mod_l[:, :, i] for i in range(6))
    mod_c = jnp.broadcast_to(mod[b].reshape(1, 1, 6, D_MODEL), (b, 1, 6, D_MODEL))
    sh1c, sc1c = mod_c[:, :, 0], mod_c[:, :, 1]

    k_c, v_c, xb_c = _proj_call(False, ctx, sh1c, sc1c, g1n, w, _identity_tables(c_len))
    states = _lru_call(False, xb_c, w)

    k_l, v_l, xb_l, q_l, yb_l, gl_l = _proj_call(True, x, sh1, sc1, g1n, w, _rope_tables(s))
    m = _lru_call(True, xb_l, w, yb=yb_l, h0=states)
    attn = _attn_call(q_l, k_l, k_c, v_l, v_c)
    x1 = _merge_call(attn, m, gl_l, x, g1, w)
    return _ffn_call(x1, sh2, sc2, g2, w)
```

```python
import functools

import jax
import jax.numpy as jnp
from jax import lax
from jax.experimental import pallas as pl
from jax.experimental.pallas import tpu as pltpu

F32 = jnp.float32
BF16 = jnp.bfloat16

D_MODEL = 1024
GRID_W = 64
N_HEADS = 8
QK_NOPE_DIM = 64
QK_ROPE_DIM = 32
ROPE_HALF = QK_ROPE_DIM // 2
V_HEAD_DIM = 64
QK_DIM = QK_NOPE_DIM + QK_ROPE_DIM
HEAD_PAD = 128
Q_LORA_RANK = 384
KV_LORA_RANK = 256
ROPE_BASE = 10000.0
LRU_WIDTH = 1280
LRU_BLOCKS = 10
LRU_BLOCK_W = LRU_WIDTH // LRU_BLOCKS
LRU_CONV_W = 4
LRU_C = 8.0
FFN_DIM = 2816
FFN_CONV_W = 3
EPS = 1e-6
TINY = 1e-30
OFF_KV = Q_LORA_RANK
OFF_KR = OFF_KV + KV_LORA_RANK
OFF_XB = OFF_KR + QK_ROPE_DIM
OFF_YB = OFF_XB + LRU_WIDTH
OFF_G = OFF_YB + LRU_WIDTH
IN_Q = 0
IN_KV = IN_Q + Q_LORA_RANK
IN_KR = IN_KV + KV_LORA_RANK
IN_A_END = IN_KR + HEAD_PAD
IN_XB = 0
IN_YB = IN_XB + LRU_WIDTH
IN_GL = IN_YB + LRU_WIDTH
IN_END = IN_GL + 2 * D_MODEL

SUBLANES = 8
VT_ROWS = 80
LANES = 128
VMEM_LIMIT = 56 * 1024 * 1024

MOD_ROWS = 8
MOD_TN = 768
PROJ_TM = 512
LRU_TC = 256
LRU_UNROLL = 16
LRU_FIX_ROWS = 64
GATE_BIAS_ROWS = 3
VELTKAMP_8BIT = 65537.0
LOG2_E = 1.4426950408889634
ATTN_TQ = 2048
ATTN_SUB = 256
ATTN_TK = 256
MERGE_TM = 512
MERGE_SUBTILES = 2
FFN_TM = 512
FFN_FC = 256
HALO = SUBLANES


def _dot(a, b):
    return jnp.dot(a, b, preferred_element_type=F32)


def _dot_nt(a, b):
    return lax.dot_general(a, b, (((1,), (1,)), ((), ())), preferred_element_type=F32)


def _rms(x, g):
    return x * lax.rsqrt(jnp.mean(x * x, axis=-1, keepdims=True) + EPS) * g


def _const_spec(shape):
    nd = len(shape)
    return pl.BlockSpec(shape, lambda *_: (0,) * nd, pipeline_mode=pl.Buffered(1))


def _params(sem):
    return pltpu.CompilerParams(dimension_semantics=sem, vmem_limit_bytes=VMEM_LIMIT)


def _mod_body(c_ref, w_ref, b_ref, o_ref):
    c = c_ref[...]
    s = c * jax.nn.sigmoid(c)
    o_ref[...] = _dot(s.astype(BF16), w_ref[...].astype(BF16)) + b_ref[...]


def _mod_call(cc, w_mod, b_mod):
    n = w_mod.shape[-1]
    return pl.pallas_call(
        _mod_body,
        grid=(n // MOD_TN,),
        in_specs=[
            pl.BlockSpec((MOD_ROWS, D_MODEL), lambda j: (0, 0)),
            pl.BlockSpec((None, D_MODEL, MOD_TN), lambda j: (0, 0, j)),
            pl.BlockSpec((1, MOD_TN), lambda j: (0, j)),
        ],
        out_specs=pl.BlockSpec((MOD_ROWS, MOD_TN), lambda j: (0, j)),
        out_shape=jax.ShapeDtypeStruct((MOD_ROWS, n), F32),
        compiler_params=_params(("arbitrary",)),
        name="mod",
    )(cc, w_mod, b_mod)


def _rope128(t, c, s1, s2):
    return (t * c + pltpu.roll(t, HEAD_PAD - ROPE_HALF, 1) * s1
            + pltpu.roll(t, ROPE_HALF, 1) * s2)


def _proj_body(latent, x_ref, sh_ref, sc_ref, g1_ref, wa_ref, wb_ref, gkv_ref, wuk_ref, wuv_ref,
               c_ref, s1_ref, s2_ref, *rest):
    if latent:
        gq_ref, wuq_ref, k_ref, v_ref, xb_ref, q_ref, yb_ref, gl_ref = rest
    else:
        k_ref, v_ref, xb_ref = rest
    x = x_ref[0]
    h = _rms(x, g1_ref[...]) * (1.0 + sc_ref[0]) + sh_ref[0]
    hb = h.astype(BF16)
    c, s1, s2 = c_ref[...], s1_ref[...], s2_ref[...]

    if latent:
        lat = _dot(hb, wa_ref[...])
        q_lat = lat[:, :IN_KV - IN_Q]
        lat = lat[:, IN_KV - IN_Q:]
    else:
        lat = _dot(hb, wa_ref[:, IN_KV:IN_A_END])
    kv_lat, kr_raw = lat[:, :IN_KR - IN_KV], lat[:, IN_KR - IN_KV:]
    xb_ref[0] = _dot(hb, wb_ref[:, IN_XB:IN_YB])
    kvn = _rms(kv_lat, gkv_ref[...]).astype(BF16)
    kk = _dot(kvn, wuk_ref[...])
    vt = _dot_nt(wuv_ref[...], kvn)
    ones = jnp.ones((VT_ROWS - V_HEAD_DIM, vt.shape[1]), F32)
    v_ref[0] = jnp.concatenate([piece for hd in range(N_HEADS)
                                for piece in (vt[hd * V_HEAD_DIM:(hd + 1) * V_HEAD_DIM], ones)],
                               axis=0).astype(BF16)
    if latent:
        qn = _rms(q_lat, gq_ref[...]).astype(BF16)
        qq = _dot(qn, wuq_ref[...])
        yb_ref[0] = _dot(hb, wb_ref[:, IN_YB:IN_GL])
        gl_ref[0] = _dot(hb, wb_ref[:, IN_GL:IN_END])
    kr = _rope128(kr_raw, c, s1, s2)
    for hd in range(N_HEADS):
        sl = slice(hd * HEAD_PAD, (hd + 1) * HEAD_PAD)
        k_ref[0, :, sl] = (kk[:, sl] + kr).astype(BF16)
    if latent:
        scale = QK_DIM ** -0.5 * LOG2_E
        for hd in range(N_HEADS):
            sl = slice(hd * HEAD_PAD, (hd + 1) * HEAD_PAD)
            q_ref[0, :, sl] = (_rope128(qq[:, sl], c, s1, s2) * scale).astype(BF16)


def _proj_call(latent, x, sh, sc, g1, w, tabs):
    b, s, _ = x.shape
    tm = min(PROJ_TM, s)
    row = lambda width: pl.BlockSpec((1, tm, width), lambda i, j: (j, i, 0))
    modrow = pl.BlockSpec((1, 1, D_MODEL), lambda i, j: (j, 0, 0))
    tab = pl.BlockSpec((tm, HEAD_PAD), lambda i, j: (i, 0))
    ins = [x, sh, sc, g1, w['in_a'], w['in_b'], w['gkv'], w['uk'], w['uv'], *tabs]
    in_specs = [row(D_MODEL), modrow, modrow] + [_const_spec(a.shape) for a in ins[3:9]] + [tab] * 3
    widths = [N_HEADS * HEAD_PAD, None, LRU_WIDTH]
    dtypes = [BF16, BF16, F32]
    if latent:
        extra = [w['gq'], w['uq']]
        ins += extra
        in_specs += [_const_spec(a.shape) for a in extra]
        widths += [N_HEADS * HEAD_PAD, LRU_WIDTH, 2 * D_MODEL]
        dtypes += [BF16, F32, F32]
    hv = N_HEADS * VT_ROWS
    out_specs = [row(wd) for wd in widths if wd is not None]
    out_shape = [jax.ShapeDtypeStruct((b, s, wd), dt) for wd, dt in zip(widths, dtypes) if wd is not None]
    out_specs.insert(1, pl.BlockSpec((1, hv, tm), lambda i, j: (j, 0, i)))
    out_shape.insert(1, jax.ShapeDtypeStruct((b, hv, s), BF16))
    return pl.pallas_call(
        functools.partial(_proj_body, latent),
        grid=(s // tm, b),
        in_specs=in_specs,
        out_specs=out_specs,
        out_shape=out_shape,
        compiler_params=_params(("parallel", "parallel")),
        name="proj_latent" if latent else "proj_ctx",
    )(*ins)


def _gelu_tanh(x):
    return 0.5 * x * (1.0 + jnp.tanh(0.7978845608028654 * (x + 0.044715 * (x * x * x))))


def _lru_body(latent, xb_ref, cw_ref, cb_ref, wg_ref, lam_ref, *rest):
    if latent:
        yb_ref, h0_ref, out_ref, xpad, af, uf, ab, ub, hfl, pfl, hbl, pbl = rest
    else:
        st_ref, xpad, af, uf, ab, ub, hfl, pfl, hbl, pbl = rest
    s = xb_ref.shape[1]
    seg = s // SUBLANES
    tc = min(LRU_TC, s)
    nc = s // tc
    piece = min(tc, seg)

    zero_halo = jnp.zeros((HALO, LANES), F32)
    xpad[0:HALO, :] = zero_halo
    xpad[s + HALO:s + 2 * HALO, :] = zero_halo

    def copy_chunk(ci, carry):
        r0 = pl.multiple_of(ci * tc, tc)
        xpad[pl.ds(r0 + HALO, tc), :] = xb_ref[0, pl.ds(r0, tc), :]
        return carry

    lax.fori_loop(0, nc, copy_chunk, 0)

    cw = cw_ref[...]
    cb = cb_ref[...]
    wg = wg_ref[0]
    lam = lam_ref[0]
    hcsp = (0.5 * LRU_C) * (jnp.maximum(-lam, 0.0) + jnp.log1p(jnp.exp(-jnp.abs(lam))))
    bias_lhs = jnp.where(lax.broadcasted_iota(jnp.int32, (tc, LANES), 1) < GATE_BIAS_ROWS, 1.0, 0.0).astype(BF16)

    def interleaved(ci):
        pieces = []
        for k in range(tc // piece):
            t0 = ci * tc + k * piece
            j = t0 // seg
            pieces.append((slice(k * piece, (k + 1) * piece),
                           pl.ds((t0 - j * seg) * SUBLANES + j, piece, stride=SUBLANES)))
        return pieces

    def gate_chunk(ci, carry):
        r0 = pl.multiple_of(ci * tc, tc)
        xm2 = xpad[pl.ds(r0 + (HALO - 2), tc), :]
        xm1 = xpad[pl.ds(r0 + (HALO - 1), tc), :]
        x0 = xpad[pl.ds(r0 + HALO, tc), :]
        xp1 = xpad[pl.ds(r0 + (HALO + 1), tc), :]
        xc = cb + xm2 * cw[0:1] + xm1 * cw[1:2] + x0 * cw[2:3] + xp1 * cw[3:4]
        t = jnp.tanh(_dot(jnp.concatenate([xc.astype(BF16), bias_lhs], axis=1), wg))
        hx = 0.5 * xc
        for d, (a_ref, u_ref) in enumerate(((af, uf), (ab, ub))):
            t_r = t[:, (2 * d) * LANES:(2 * d + 1) * LANES]
            t_i = t[:, (2 * d + 1) * LANES:(2 * d + 2) * LANES]
            neg_log_a = hcsp[d:d + 1] + hcsp[d:d + 1] * t_r
            a = jnp.exp2(neg_log_a * (-LOG2_E))
            y = jnp.tanh(neg_log_a) * (1.0 + a * a)
            mult = y * lax.rsqrt(jnp.maximum(y, TINY))
            u = (mult * hx) * (1.0 + t_i)
            for rows, rd in interleaved(ci):
                a_ref[rd, :] = a[rows]
                u_ref[rd, :] = u[rows]
        return carry

    lax.fori_loop(0, nc, gate_chunk, 0, unroll=min(8, nc))

    def scan_step(i, carry):
        hf, pf, hb, pb = carry
        fwd = pl.ds(pl.multiple_of(i * SUBLANES, SUBLANES), SUBLANES)
        bwd = pl.ds(pl.multiple_of((seg - 1 - i) * SUBLANES, SUBLANES), SUBLANES)
        a = af[fwd, :]
        hf = a * hf + uf[fwd, :]
        pf = a * pf
        hfl[fwd, :] = hf
        pfl[fwd, :] = pf
        a = ab[bwd, :]
        hb = a * hb + ub[bwd, :]
        pb = a * pb
        hbl[bwd, :] = hb
        pbl[bwd, :] = pb
        return hf, pf, hb, pb

    zeros = jnp.zeros((SUBLANES, LANES), F32)
    ones = jnp.ones((SUBLANES, LANES), F32)
    hf, pf, hb, pb = lax.fori_loop(0, seg, scan_step, (zeros, ones, zeros, ones), unroll=LRU_UNROLL)

    if latent:
        h0 = h0_ref[0]
        cf, cbk = h0[0:1], h0[1:2]
    else:
        cf = cbk = jnp.zeros((1, LANES), F32)
    cfs = []
    for j in range(SUBLANES):
        cfs.append(cf)
        cf = hf[j:j + 1] + pf[j:j + 1] * cf
    cbs = [None] * SUBLANES
    for j in reversed(range(SUBLANES)):
        cbs[j] = cbk
        cbk = hb[j:j + 1] + pb[j:j + 1] * cbk

    if latent:
        fix_rows = min(LRU_FIX_ROWS, s)
        cf_all = jnp.tile(jnp.concatenate(cfs, axis=0), (fix_rows // SUBLANES, 1))
        cb_all = jnp.tile(jnp.concatenate(cbs, axis=0), (fix_rows // SUBLANES, 1))

        def fix_chunk(ci, carry):
            rows = pl.ds(pl.multiple_of(ci * fix_rows, fix_rows), fix_rows)
            af[rows, :] = (hfl[rows, :] + pfl[rows, :] * cf_all) + (hbl[rows, :] + pbl[rows, :] * cb_all)
            return carry

        lax.fori_loop(0, s // fix_rows, fix_chunk, 0, unroll=2)

        def out_chunk(ci, carry):
            r0 = pl.multiple_of(ci * tc, tc)
            hsum = jnp.concatenate([af[rd, :] for _, rd in interleaved(ci)], axis=0)
            out_ref[0, pl.ds(r0, tc), :] = (hsum * _gelu_tanh(yb_ref[0, pl.ds(r0, tc), :])).astype(BF16)
            return carry

        lax.fori_loop(0, nc, out_chunk, 0)
    else:
        st_ref[0] = jnp.concatenate([cf, cbk], axis=0)


def _lru_call(latent, xb, w, yb=None, h0=None):
    b, s, _ = xb.shape
    seq = pl.BlockSpec((1, s, LANES), lambda i, j: (i, 0, j))
    st = pl.BlockSpec((1, 2, LANES), lambda i, j: (i, 0, j))
    in_specs = [
        seq,
        pl.BlockSpec((LRU_CONV_W, LANES), lambda i, j: (0, j)),
        pl.BlockSpec((1, LANES), lambda i, j: (0, j)),
        pl.BlockSpec((1, 2 * LANES, 4 * LANES), lambda i, j: (j, 0, 0)),
        pl.BlockSpec((1, 2, LANES), lambda i, j: (j, 0, 0)),
    ]
    ins = [xb, w['conv_w'], w['conv_b'], w['gate_w'], w['lam']]
    scratch = [pltpu.VMEM((s + 2 * HALO, LANES), F32)] + [pltpu.VMEM((s, LANES), F32)] * 8
    if latent:
        ins += [yb, h0]
        in_specs += [seq, st]
        out_specs = seq
        out_shape = jax.ShapeDtypeStruct((b, s, LRU_WIDTH), BF16)
    else:
        out_specs = st
        out_shape = jax.ShapeDtypeStruct((b, 2, LRU_WIDTH), F32)
    return pl.pallas_call(
        functools.partial(_lru_body, latent),
        grid=(b, LRU_BLOCKS),
        in_specs=in_specs,
        out_specs=out_specs,
        out_shape=out_shape,
        scratch_shapes=scratch,
        compiler_params=_params(("parallel", "parallel")),
        name="lru_latent" if latent else "lru_ctx",
    )(*ins)


def _col_groups(x, op):
    rows, cols = x.shape
    return op(x.reshape(rows // SUBLANES, SUBLANES, cols), axis=0)


def _attn_body(q_ref, kl_ref, kc_ref, vl_ref, vc_ref, o_ref, s0_ref, s1_ref, p0_ref, p1_ref, o0_ref):
    s_len = kl_ref.shape[1]
    c_len = kc_ref.shape[1]
    tk = min(ATTN_TK, s_len)
    sub = s0_ref.shape[1]
    n_sub = q_ref.shape[1] // sub
    chunks = [(kc_ref, 0, c_len, 0)] + [(kl_ref, ki * tk, tk, c_len + ki * tk) for ki in range(s_len // tk)]
    s_refs, p_refs = (s0_ref, s1_ref), (p0_ref, p1_ref)

    def qrows(qt):
        return pl.ds(pl.multiple_of(qt * sub, sub), sub)

    def score_pass(qt, hd):
        lanes = slice(hd * HEAD_PAD, (hd + 1) * HEAD_PAD)
        q = q_ref[0, qrows(qt), lanes]
        m8 = None
        for k_ref, r0, rows, off in chunks:
            sk = _dot_nt(k_ref[0, r0:r0 + rows, lanes], q)
            s_refs[hd][off:off + rows, :] = sk
            g = _col_groups(sk, jnp.max)
            m8 = g if m8 is None else jnp.maximum(m8, g)
        return jnp.max(m8, axis=0, keepdims=True)

    def exp_pass(hd, m):
        for _, _, rows, off in chunks:
            p_refs[hd][off:off + rows, :] = jnp.exp2(s_refs[hd][off:off + rows, :] - m).astype(BF16)

    def value_pass(hd):
        vrows = slice(hd * VT_ROWS, (hd + 1) * VT_ROWS)
        acc = (_dot(vc_ref[0, vrows, :], p_refs[hd][0:c_len, :])
               + _dot(vl_ref[0, vrows, :], p_refs[hd][c_len:c_len + s_len, :]))
        return acc[:V_HEAD_DIM] / acc[V_HEAD_DIM:V_HEAD_DIM + 1]

    def store(qt, o0, o1):
        o_ref[0, qrows(qt), :] = jnp.concatenate([o0, o1], axis=0).T.astype(BF16)

    def finish(hd, m):
        exp_pass(hd, m)
        return value_pass(hd)

    def head1_trip(qt, m_prev):
        o0_ref[...] = finish(0, m_prev)
        return score_pass(qt, 1)

    def head0_trip(qt, m_prev):
        store(qt - 1, o0_ref[...], finish(1, m_prev))
        return score_pass(qt, 0)

    def trip(u, m_prev):
        return lax.cond(u % 2 == 1, head1_trip, head0_trip, u // 2, m_prev)

    m = lax.fori_loop(1, 2 * n_sub, trip, score_pass(0, 0))
    store(n_sub - 1, o0_ref[...], finish(1, m))


def _attn_call(q, k_l, k_c, vt_l, vt_c):
    b, s, _ = q.shape
    c_len = k_c.shape[1]
    tq = min(ATTN_TQ, s)
    sub = min(ATTN_SUB, tq)
    pairs = N_HEADS // 2
    return pl.pallas_call(
        _attn_body,
        grid=(b, pairs, s // tq),
        in_specs=[
            pl.BlockSpec((1, tq, 2 * HEAD_PAD), lambda i, j, t: (i, t, j)),
            pl.BlockSpec((1, s, 2 * HEAD_PAD), lambda i, j, t: (i, 0, j)),
            pl.BlockSpec((1, c_len, 2 * HEAD_PAD), lambda i, j, t: (i, 0, j)),
            pl.BlockSpec((1, 2 * VT_ROWS, s), lambda i, j, t: (i, j, 0)),
            pl.BlockSpec((1, 2 * VT_ROWS, c_len), lambda i, j, t: (i, j, 0)),
        ],
        out_specs=pl.BlockSpec((1, tq, 2 * V_HEAD_DIM), lambda i, j, t: (i, t, j)),
        out_shape=jax.ShapeDtypeStruct((b, s, N_HEADS * V_HEAD_DIM), BF16),
        scratch_shapes=([pltpu.VMEM((c_len + s, sub), F32)] * 2 + [pltpu.VMEM((c_len + s, sub), BF16)] * 2
                        + [pltpu.VMEM((V_HEAD_DIM, sub), F32)]),
        compiler_params=_params(("parallel", "parallel", "parallel")),
        name="attn",
    )(q, k_l, k_c, vt_l, vt_c)


def _merge_body(attn_ref, m_ref, gl_ref, x_ref, g1_ref, bg_ref, woa_ref, wol_ref, wout_ref, o_ref):
    tm = x_ref.shape[1]
    sub = tm // MERGE_SUBTILES
    rows = [slice(i * sub, (i + 1) * sub) for i in range(MERGE_SUBTILES)]
    ys = [(_dot(attn_ref[0, r, :], woa_ref[...]), _dot(m_ref[0, r, :], wol_ref[...])) for r in rows]
    for r, (y_a, y_b) in zip(rows, ys):
        gates = 0.5 + 0.5 * jnp.tanh(0.5 * (gl_ref[0, r, :] + bg_ref[...]))
        mix = gates[:, :D_MODEL] * y_a + gates[:, D_MODEL:] * y_b
        o_ref[0, r, :] = x_ref[0, r, :] + g1_ref[0] * _dot(mix.astype(BF16), wout_ref[...])


def _merge_call(attn, m, gl, x, g1, w):
    b, s, _ = x.shape
    tm = min(MERGE_TM, s)
    row = lambda width: pl.BlockSpec((1, tm, width), lambda i, j: (i, j, 0))
    modrow = pl.BlockSpec((1, 1, D_MODEL), lambda i, j: (i, 0, 0))
    consts = [w['b_gate'], w['o_attn'], w['o_lru'], w['out']]
    return pl.pallas_call(
        _merge_body,
        grid=(b, s // tm),
        in_specs=[row(N_HEADS * V_HEAD_DIM), row(LRU_WIDTH), row(2 * D_MODEL), row(D_MODEL), modrow]
        + [_const_spec(a.shape) for a in consts],
        out_specs=row(D_MODEL),
        out_shape=jax.ShapeDtypeStruct((b, s, D_MODEL), F32),
        compiler_params=_params(("parallel", "parallel")),
        name="merge",
    )(attn, m, gl, x, g1, *consts)


def _ffn_body(xp_ref, x_ref, xn_ref, sh_ref, sc_ref, g2_ref, n2_ref, fg_ref, wup_ref, cw_ref, cb_ref,
              wdn_ref, o_ref, f_ref):
    j = pl.program_id(1)
    nj = pl.num_programs(1)
    tm = x_ref.shape[1]
    x = x_ref[0]
    xe = jnp.concatenate([xp_ref[0], x, xn_ref[0]], axis=0)
    hf = _rms(xe, n2_ref[...]) * (1.0 + sc_ref[0]) + sh_ref[0]
    keep_top = (j > 0).astype(F32)
    keep_bot = (j < nj - 1).astype(F32)
    ht = hf[HALO:HALO + tm]
    he = jnp.concatenate([hf[:HALO] * keep_top, ht, hf[HALO + tm:] * keep_bot], axis=0).astype(BF16)
    ht = ht.astype(BF16)
    ext = tm + 2 * HALO
    for ci in range(FFN_DIM // FFN_FC):
        cols = slice(ci * FFN_FC, (ci + 1) * FFN_FC)
        gcols = slice(FFN_DIM + ci * FFN_FC, FFN_DIM + (ci + 1) * FFN_FC)
        a = _dot(he, wup_ref[:, cols])
        cw = cw_ref[:, cols]
        conv = (cb_ref[:, cols] + pltpu.roll(a, 1, 0)[HALO:HALO + tm] * cw[0:1]
                + a[HALO:HALO + tm] * cw[1:2]
                + pltpu.roll(a, ext - 1, 0)[HALO:HALO + tm] * cw[2:3])
        g = _dot(ht, wup_ref[:, gcols])
        f_ref[:, cols] = (conv * jax.nn.sigmoid(conv) * g).astype(BF16)
    y = x + g2_ref[0] * _dot(f_ref[...], wdn_ref[...])
    o_ref[0] = _rms(y, fg_ref[...])


def _ffn_call(x1, sh2, sc2, g2, w):
    b, s, _ = x1.shape
    tm = min(FFN_TM, s)
    per = tm // HALO
    nblk = s // HALO
    row = pl.BlockSpec((1, tm, D_MODEL), lambda i, j: (i, j, 0))
    prev = pl.BlockSpec((1, HALO, D_MODEL), lambda i, j: (i, jnp.maximum(j * per - 1, 0), 0))
    nxt = pl.BlockSpec((1, HALO, D_MODEL), lambda i, j: (i, jnp.minimum((j + 1) * per, nblk - 1), 0))
    modrow = pl.BlockSpec((1, 1, D_MODEL), lambda i, j: (i, 0, 0))
    consts = [w['norm2_g'], w['final_g'], w['up'], w['ffn_conv_w'], w['ffn_conv_b'], w['down']]
    return pl.pallas_call(
        _ffn_body,
        grid=(b, s // tm),
        in_specs=[prev, row, nxt, modrow, modrow, modrow] + [_const_spec(a.shape) for a in consts],
        out_specs=row,
        out_shape=jax.ShapeDtypeStruct((b, s, D_MODEL), F32),
        scratch_shapes=[pltpu.VMEM((tm, FFN_DIM), BF16)],
        compiler_params=_params(("parallel", "parallel")),
        name="ffn",
    )(x1, x1, x1, sh2, sc2, g2, *consts)


def _rope_tables(n):
    rows = n // GRID_W
    row_ids = jnp.repeat(jnp.arange(rows), GRID_W).astype(F32)
    col_ids = jnp.tile(jnp.arange(GRID_W), rows).astype(F32)
    axis_dim = QK_ROPE_DIM // 2
    inv = 1.0 / (ROPE_BASE ** (jnp.arange(0, axis_dim, 2, dtype=F32) / axis_dim))
    ang = jnp.concatenate([row_ids[:, None] * inv, col_ids[:, None] * inv], axis=-1)
    cos, sin = jnp.cos(ang), jnp.sin(ang)
    ones = lambda w_: jnp.ones((n, w_), F32)
    zeros = lambda w_: jnp.zeros((n, w_), F32)
    tail = HEAD_PAD - QK_DIM
    c = jnp.concatenate([ones(QK_NOPE_DIM), cos, cos, ones(tail)], axis=-1)
    s1 = jnp.concatenate([zeros(QK_NOPE_DIM), -sin, zeros(ROPE_HALF + tail)], axis=-1)
    s2 = jnp.concatenate([zeros(QK_NOPE_DIM + ROPE_HALF), sin, zeros(tail)], axis=-1)
    return c, s1, s2


def _identity_tables(n):
    return (jnp.ones((n, HEAD_PAD), F32), jnp.zeros((n, HEAD_PAD), F32), jnp.zeros((n, HEAD_PAD), F32))


def _prep_weights(w_in, q_norm_g, kv_norm_g, w_uq, w_ukv, w_o_attn, lru_conv_w, lru_conv_b, lru_w_a,
                  lru_b_a, lru_w_x, lru_b_x, lru_lambda, w_o_lru, w_out, b_gate, norm2_g, w_up,
                  ffn_conv_w, ffn_conv_b, w_down, final_g):
    w = {}
    kr = jnp.pad(w_in[:, OFF_KR:OFF_XB], ((0, 0), (QK_NOPE_DIM, HEAD_PAD - QK_DIM)))
    w['in_a'] = jnp.concatenate([w_in[:, :OFF_KR], kr], axis=1).astype(BF16)
    w['in_b'] = w_in[:, OFF_XB:].astype(BF16)
    w['gq'] = q_norm_g[None, :]
    w['gkv'] = kv_norm_g[None, :]
    uq = w_uq.reshape(Q_LORA_RANK, N_HEADS, QK_DIM)
    w['uq'] = jnp.pad(uq, ((0, 0), (0, 0), (0, HEAD_PAD - QK_DIM))).reshape(Q_LORA_RANK, -1).astype(BF16)
    ukv = w_ukv.reshape(KV_LORA_RANK, N_HEADS, QK_NOPE_DIM + V_HEAD_DIM)
    w['uk'] = jnp.pad(ukv[..., :QK_NOPE_DIM],
                      ((0, 0), (0, 0), (0, HEAD_PAD - QK_NOPE_DIM))).reshape(KV_LORA_RANK, -1).astype(BF16)
    w['uv'] = ukv[..., QK_NOPE_DIM:].reshape(KV_LORA_RANK, -1).T.astype(BF16)
    w['conv_w'] = lru_conv_w
    w['conv_b'] = lru_conv_b[None, :]
    gw = 0.5 * jnp.concatenate([lru_w_a[0], lru_w_x[0], lru_w_a[1], lru_w_x[1]], axis=-1)
    gb = 0.5 * jnp.stack([lru_b_a[0], lru_b_x[0], lru_b_a[1], lru_b_x[1]], axis=0)
    gb = gb.reshape(4, LRU_BLOCKS, LRU_BLOCK_W).transpose(1, 0, 2).reshape(LRU_BLOCKS, 1, -1)
    terms = []
    for _ in range(GATE_BIAS_ROWS):
        t = gb * VELTKAMP_8BIT
        hi = t - (t - gb)
        terms.append(hi)
        gb = gb - hi
    bias_rows = jnp.pad(jnp.concatenate(terms, axis=1), ((0, 0), (0, LANES - GATE_BIAS_ROWS), (0, 0)))
    w['gate_w'] = jnp.concatenate([gw, bias_rows], axis=1).astype(BF16)
    w['lam'] = lru_lambda.reshape(2, LRU_BLOCKS, LRU_BLOCK_W).transpose(1, 0, 2)
    w['o_attn'] = w_o_attn.astype(BF16)
    w['o_lru'] = w_o_lru.astype(BF16)
    w['out'] = w_out.astype(BF16)
    w['b_gate'] = b_gate[None, :]
    w['norm2_g'] = norm2_g[None, :]
    w['final_g'] = final_g[None, :]
    w['up'] = w_up.astype(BF16)
    w['ffn_conv_w'] = ffn_conv_w
    w['ffn_conv_b'] = ffn_conv_b[None, :]
    w['down'] = w_down.astype(BF16)
    return w


def kernel(x, c, ctx, c_ctx, w_mod, b_mod, norm1_g, w_in, b_gate, q_norm_g, kv_norm_g, w_uq, w_ukv,
           w_o_attn, lru_conv_w, lru_conv_b, lru_w_a, lru_b_a, lru_w_x, lru_b_x, lru_lambda, w_o_lru,
           w_out, norm2_g, w_up, ffn_conv_w, ffn_conv_b, w_down, final_g):
    assert w_mod.shape[0] == 1, "single-layer block"
    b, s, _ = x.shape
    c_len = ctx.shape[1]
    assert b + 1 <= MOD_ROWS

    w = _prep_weights(w_in[0], q_norm_g[0], kv_norm_g[0], w_uq[0], w_ukv[0], w_o_attn[0], lru_conv_w[0],
                      lru_conv_b[0], lru_w_a[0], lru_b_a[0], lru_w_x[0], lru_b_x[0], lru_lambda[0],
                      w_o_lru[0], w_out[0], b_gate[0], norm2_g[0], w_up[0], ffn_conv_w[0],
                      ffn_conv_b[0], w_down[0], final_g)
    g1n = norm1_g[0][None, :]

    cc = jnp.concatenate([c, c_ctx[None, :], jnp.zeros((MOD_ROWS - b - 1, D_MODEL), F32)], axis=0)
    mod = _mod_call(cc, w_mod, b_mod)
    mod_l = mod[:b].reshape(b, 1, 6, D_MODEL)
    sh1, sc1, g1, sh2, sc2, g2 = (mod_l[:, :, i] for i in range(6))
    mod_c = jnp.broadcast_to(mod[b].reshape(1, 1, 6, D_MODEL), (b, 1, 6, D_MODEL))
    sh1c, sc1c = mod_c[:, :, 0], mod_c[:, :, 1]

    k_c, v_c, xb_c = _proj_call(False, ctx, sh1c, sc1c, g1n, w, _identity_tables(c_len))
    states = _lru_call(False, xb_c, w)

    k_l, v_l, xb_l, q_l, yb_l, gl_l = _proj_call(True, x, sh1, sc1, g1n, w, _rope_tables(s))
    m = _lru_call(True, xb_l, w, yb=yb_l, h0=states)
    attn = _attn_call(q_l, k_l, k_c, v_l, v_c)
    x1 = _merge_call(attn, m, gl_l, x, g1, w)
    return _ffn_call(x1, sh2, sc2, g2, w)
```

```python
import functools

import jax
import jax.numpy as jnp
from jax import lax
from jax.experimental import pallas as pl
from jax.experimental.pallas import tpu as pltpu

F32 = jnp.float32
BF16 = jnp.bfloat16

D_MODEL = 1024
GRID_W = 64
N_HEADS = 8
QK_NOPE_DIM = 64
QK_ROPE_DIM = 32
ROPE_HALF = QK_ROPE_DIM // 2
V_HEAD_DIM = 64
QK_DIM = QK_NOPE_DIM + QK_ROPE_DIM
HEAD_PAD = 128
Q_LORA_RANK = 384
KV_LORA_RANK = 256
ROPE_BASE = 10000.0
LRU_WIDTH = 1280
LRU_BLOCKS = 10
LRU_BLOCK_W = LRU_WIDTH // LRU_BLOCKS
LRU_CONV_W = 4
LRU_C = 8.0
FFN_DIM = 2816
FFN_CONV_W = 3
EPS = 1e-6
TINY = 1e-30
OFF_KV = Q_LORA_RANK
OFF_KR = OFF_KV + KV_LORA_RANK
OFF_XB = OFF_KR + QK_ROPE_DIM
OFF_YB = OFF_XB + LRU_WIDTH
OFF_G = OFF_YB + LRU_WIDTH
IN_Q = 0
IN_KV = IN_Q + Q_LORA_RANK
IN_KR = IN_KV + KV_LORA_RANK
IN_A_END = IN_KR + HEAD_PAD
IN_XB = 0
IN_YB = IN_XB + LRU_WIDTH
IN_GL = IN_YB + LRU_WIDTH
IN_END = IN_GL + 2 * D_MODEL

SUBLANES = 8
VT_ROWS = 80
LANES = 128
VMEM_LIMIT = 56 * 1024 * 1024

MOD_ROWS = 8
MOD_TN = 768
PROJ_TM = 512
LRU_TC = 256
LRU_UNROLL = 16
LRU_FIX_ROWS = 64
GATE_BIAS_ROWS = 3
VELTKAMP_8BIT = 65537.0
LOG2_E = 1.4426950408889634
ATTN_TQ = 4096
ATTN_SUB = 256
ATTN_TK = 256
MERGE_TM = 512
MERGE_SUBTILES = 2
FFN_TM = 512
FFN_FC = 256
HALO = SUBLANES


def _dot(a, b):
    return jnp.dot(a, b, preferred_element_type=F32)


def _dot_nt(a, b):
    return lax.dot_general(a, b, (((1,), (1,)), ((), ())), preferred_element_type=F32)


def _rms(x, g):
    return x * lax.rsqrt(jnp.mean(x * x, axis=-1, keepdims=True) + EPS) * g


def _const_spec(shape):
    nd = len(shape)
    return pl.BlockSpec(shape, lambda *_: (0,) * nd, pipeline_mode=pl.Buffered(1))


def _params(sem):
    return pltpu.CompilerParams(dimension_semantics=sem, vmem_limit_bytes=VMEM_LIMIT)


def _mod_body(c_ref, w_ref, b_ref, o_ref):
    c = c_ref[...]
    s = c * jax.nn.sigmoid(c)
    o_ref[...] = _dot(s.astype(BF16), w_ref[...].astype(BF16)) + b_ref[...]


def _mod_call(cc, w_mod, b_mod):
    n = w_mod.shape[-1]
    return pl.pallas_call(
        _mod_body,
        grid=(n // MOD_TN,),
        in_specs=[
            pl.BlockSpec((MOD_ROWS, D_MODEL), lambda j: (0, 0)),
            pl.BlockSpec((None, D_MODEL, MOD_TN), lambda j: (0, 0, j)),
            pl.BlockSpec((1, MOD_TN), lambda j: (0, j)),
        ],
        out_specs=pl.BlockSpec((MOD_ROWS, MOD_TN), lambda j: (0, j)),
        out_shape=jax.ShapeDtypeStruct((MOD_ROWS, n), F32),
        compiler_params=_params(("arbitrary",)),
        name="mod",
    )(cc, w_mod, b_mod)


def _rope128(t, c, s1, s2):
    return (t * c + pltpu.roll(t, HEAD_PAD - ROPE_HALF, 1) * s1
            + pltpu.roll(t, ROPE_HALF, 1) * s2)


def _proj_body(latent, x_ref, sh_ref, sc_ref, g1_ref, wa_ref, wb_ref, gkv_ref, wuk_ref, wuv_ref,
               c_ref, s1_ref, s2_ref, *rest):
    if latent:
        gq_ref, wuq_ref, k_ref, v_ref, xb_ref, q_ref, yb_ref, gl_ref = rest
    else:
        k_ref, v_ref, xb_ref = rest
    x = x_ref[0]
    h = _rms(x, g1_ref[...]) * (1.0 + sc_ref[0]) + sh_ref[0]
    hb = h.astype(BF16)
    c, s1, s2 = c_ref[...], s1_ref[...], s2_ref[...]

    if latent:
        lat = _dot(hb, wa_ref[...])
        q_lat = lat[:, :IN_KV - IN_Q]
        lat = lat[:, IN_KV - IN_Q:]
    else:
        lat = _dot(hb, wa_ref[:, IN_KV:IN_A_END])
    kv_lat, kr_raw = lat[:, :IN_KR - IN_KV], lat[:, IN_KR - IN_KV:]
    xb_ref[0] = _dot(hb, wb_ref[:, IN_XB:IN_YB])
    kvn = _rms(kv_lat, gkv_ref[...]).astype(BF16)
    kk = _dot(kvn, wuk_ref[...])
    vt = _dot_nt(wuv_ref[...], kvn)
    ones = jnp.ones((VT_ROWS - V_HEAD_DIM, vt.shape[1]), F32)
    v_ref[0] = jnp.concatenate([piece for hd in range(N_HEADS)
                                for piece in (vt[hd * V_HEAD_DIM:(hd + 1) * V_HEAD_DIM], ones)],
                               axis=0).astype(BF16)
    if latent:
        qn = _rms(q_lat, gq_ref[...]).astype(BF16)
        qq = _dot(qn, wuq_ref[...])
        yb_ref[0] = _gelu_tanh(_dot(hb, wb_ref[:, IN_YB:IN_GL]))
        gl_ref[0] = _dot(hb, wb_ref[:, IN_GL:IN_END])
    kr = _rope128(kr_raw, c, s1, s2)
    for hd in range(N_HEADS):
        sl = slice(hd * HEAD_PAD, (hd + 1) * HEAD_PAD)
        k_ref[0, :, sl] = (kk[:, sl] + kr).astype(BF16)
    if latent:
        scale = QK_DIM ** -0.5 * LOG2_E
        for hd in range(N_HEADS):
            sl = slice(hd * HEAD_PAD, (hd + 1) * HEAD_PAD)
            q_ref[0, :, sl] = (_rope128(qq[:, sl], c, s1, s2) * scale).astype(BF16)


def _proj_call(latent, x, sh, sc, g1, w, tabs):
    b, s, _ = x.shape
    tm = min(PROJ_TM, s)
    row = lambda width: pl.BlockSpec((1, tm, width), lambda i, j: (j, i, 0))
    modrow = pl.BlockSpec((1, 1, D_MODEL), lambda i, j: (j, 0, 0))
    tab = pl.BlockSpec((tm, HEAD_PAD), lambda i, j: (i, 0))
    ins = [x, sh, sc, g1, w['in_a'], w['in_b'], w['gkv'], w['uk'], w['uv'], *tabs]
    in_specs = [row(D_MODEL), modrow, modrow] + [_const_spec(a.shape) for a in ins[3:9]] + [tab] * 3
    widths = [N_HEADS * HEAD_PAD, None, LRU_WIDTH]
    dtypes = [BF16, BF16, F32]
    if latent:
        extra = [w['gq'], w['uq']]
        ins += extra
        in_specs += [_const_spec(a.shape) for a in extra]
        widths += [N_HEADS * HEAD_PAD, LRU_WIDTH, 2 * D_MODEL]
        dtypes += [BF16, F32, F32]
    hv = N_HEADS * VT_ROWS
    out_specs = [row(wd) for wd in widths if wd is not None]
    out_shape = [jax.ShapeDtypeStruct((b, s, wd), dt) for wd, dt in zip(widths, dtypes) if wd is not None]
    out_specs.insert(1, pl.BlockSpec((1, hv, tm), lambda i, j: (j, 0, i)))
    out_shape.insert(1, jax.ShapeDtypeStruct((b, hv, s), BF16))
    return pl.pallas_call(
        functools.partial(_proj_body, latent),
        grid=(s // tm, b),
        in_specs=in_specs,
        out_specs=out_specs,
        out_shape=out_shape,
        compiler_params=_params(("parallel", "parallel")),
        name="proj_latent" if latent else "proj_ctx",
    )(*ins)


def _gelu_tanh(x):
    return 0.5 * x * (1.0 + jnp.tanh(0.7978845608028654 * (x + 0.044715 * (x * x * x))))


def _lru_body(latent, xb_ref, cw_ref, cb_ref, wg_ref, lam_ref, *rest):
    if latent:
        yb_ref, h0_ref, out_ref, xpad, gates, scans = rest
    else:
        st_ref, xpad, gates, scans = rest
    s = xb_ref.shape[1]
    af, uf, ab, ub = (gates.at[pl.ds(k * s, s)] for k in range(4))
    hfl, pfl, hbl, pbl = (scans.at[pl.ds(k * s, s)] for k in range(4))
    seg = s // SUBLANES
    tc = min(LRU_TC, s)
    nc = s // tc
    piece = min(tc, seg)

    zero_halo = jnp.zeros((HALO, LANES), F32)
    xpad[0:HALO, :] = zero_halo
    xpad[s + HALO:s + 2 * HALO, :] = zero_halo

    def copy_chunk(ci, carry):
        r0 = pl.multiple_of(ci * tc, tc)
        xpad[pl.ds(r0 + HALO, tc), :] = xb_ref[0, pl.ds(r0, tc), :]
        return carry

    lax.fori_loop(0, nc, copy_chunk, 0)

    cw = cw_ref[...]
    cb = cb_ref[...]
    wg = wg_ref[0]
    lam = lam_ref[0]
    hcsp = (0.5 * LRU_C) * (jnp.maximum(-lam, 0.0) + jnp.log1p(jnp.exp(-jnp.abs(lam))))
    bias_lhs = jnp.where(lax.broadcasted_iota(jnp.int32, (tc, LANES), 1) < GATE_BIAS_ROWS, 1.0, 0.0).astype(BF16)

    def interleaved(ci):
        pieces = []
        for k in range(tc // piece):
            t0 = ci * tc + k * piece
            j = t0 // seg
            pieces.append((slice(k * piece, (k + 1) * piece),
                           pl.ds((t0 - j * seg) * SUBLANES + j, piece, stride=SUBLANES)))
        return pieces

    def gate_chunk(ci, carry):
        r0 = pl.multiple_of(ci * tc, tc)
        xm2 = xpad[pl.ds(r0 + (HALO - 2), tc), :]
        xm1 = xpad[pl.ds(r0 + (HALO - 1), tc), :]
        x0 = xpad[pl.ds(r0 + HALO, tc), :]
        xp1 = xpad[pl.ds(r0 + (HALO + 1), tc), :]
        xc = cb + xm2 * cw[0:1] + xm1 * cw[1:2] + x0 * cw[2:3] + xp1 * cw[3:4]
        t = jnp.tanh(_dot(jnp.concatenate([xc.astype(BF16), bias_lhs], axis=1), wg))
        hx = 0.5 * xc
        for d, (a_ref, u_ref) in enumerate(((af, uf), (ab, ub))):
            t_r = t[:, (2 * d) * LANES:(2 * d + 1) * LANES]
            t_i = t[:, (2 * d + 1) * LANES:(2 * d + 2) * LANES]
            neg_log_a = hcsp[d:d + 1] + hcsp[d:d + 1] * t_r
            a = jnp.exp2(neg_log_a * (-LOG2_E))
            y = jnp.tanh(neg_log_a) * (1.0 + a * a)
            mult = y * lax.rsqrt(jnp.maximum(y, TINY))
            u = (mult * hx) * (1.0 + t_i)
            for rows, rd in interleaved(ci):
                a_ref[rd, :] = a[rows]
                u_ref[rd, :] = u[rows]
        return carry

    lax.fori_loop(0, nc, gate_chunk, 0, unroll=min(8, nc))

    def scan_step(i, carry):
        hf, pf, hb, pb = carry
        fwd = pl.ds(pl.multiple_of(i * SUBLANES, SUBLANES), SUBLANES)
        bwd = pl.ds(pl.multiple_of((seg - 1 - i) * SUBLANES, SUBLANES), SUBLANES)
        a = af[fwd, :]
        hf = a * hf + uf[fwd, :]
        pf = a * pf
        hfl[fwd, :] = hf
        pfl[fwd, :] = pf
        a = ab[bwd, :]
        hb = a * hb + ub[bwd, :]
        pb = a * pb
        hbl[bwd, :] = hb
        pbl[bwd, :] = pb
        return hf, pf, hb, pb

    zeros = jnp.zeros((SUBLANES, LANES), F32)
    ones = jnp.ones((SUBLANES, LANES), F32)
    hf, pf, hb, pb = lax.fori_loop(0, seg, scan_step, (zeros, ones, zeros, ones), unroll=LRU_UNROLL)

    if latent:
        h0 = h0_ref[0]
        cf, cbk = h0[0:1], h0[1:2]
    else:
        cf = cbk = jnp.zeros((1, LANES), F32)
    cfs = []
    for j in range(SUBLANES):
        cfs.append(cf)
        cf = hf[j:j + 1] + pf[j:j + 1] * cf
    cbs = [None] * SUBLANES
    for j in reversed(range(SUBLANES)):
        cbs[j] = cbk
        cbk = hb[j:j + 1] + pb[j:j + 1] * cbk

    if latent:
        fix_rows = min(LRU_FIX_ROWS, s)
        cf_all = jnp.tile(jnp.concatenate(cfs, axis=0), (fix_rows // SUBLANES, 1))
        cb_all = jnp.tile(jnp.concatenate(cbs, axis=0), (fix_rows // SUBLANES, 1))

        def fix_chunk(ci, carry):
            rows = pl.ds(pl.multiple_of(ci * fix_rows, fix_rows), fix_rows)
            af[rows, :] = (hfl[rows, :] + pfl[rows, :] * cf_all) + (hbl[rows, :] + pbl[rows, :] * cb_all)
            return carry

        lax.fori_loop(0, s // fix_rows, fix_chunk, 0, unroll=2)

        def out_chunk(ci, carry):
            r0 = pl.multiple_of(ci * tc, tc)
            hsum = jnp.concatenate([af[rd, :] for _, rd in interleaved(ci)], axis=0)
            out_ref[0, pl.ds(r0, tc), :] = (hsum * yb_ref[0, pl.ds(r0, tc), :]).astype(BF16)
            return carry

        lax.fori_loop(0, nc, out_chunk, 0)
    else:
        st_ref[0] = jnp.concatenate([cf, cbk], axis=0)


def _lru_call(latent, xb, w, yb=None, h0=None):
    b, s, _ = xb.shape
    seq = pl.BlockSpec((1, s, LANES), lambda i, j: (i, 0, j))
    st = pl.BlockSpec((1, 2, LANES), lambda i, j: (i, 0, j))
    in_specs = [
        seq,
        pl.BlockSpec((LRU_CONV_W, LANES), lambda i, j: (0, j)),
        pl.BlockSpec((1, LANES), lambda i, j: (0, j)),
        pl.BlockSpec((1, 2 * LANES, 4 * LANES), lambda i, j: (j, 0, 0)),
        pl.BlockSpec((1, 2, LANES), lambda i, j: (j, 0, 0)),
    ]
    ins = [xb, w['conv_w'], w['conv_b'], w['gate_w'], w['lam']]
    scratch = [pltpu.VMEM((s + 2 * HALO, LANES), F32)] + [pltpu.VMEM((4 * s, LANES), F32)] * 2
    if latent:
        ins += [yb, h0]
        in_specs += [seq, st]
        out_specs = seq
        out_shape = jax.ShapeDtypeStruct((b, s, LRU_WIDTH), BF16)
    else:
        out_specs = st
        out_shape = jax.ShapeDtypeStruct((b, 2, LRU_WIDTH), F32)
    return pl.pallas_call(
        functools.partial(_lru_body, latent),
        grid=(b, LRU_BLOCKS),
        in_specs=in_specs,
        out_specs=out_specs,
        out_shape=out_shape,
        scratch_shapes=scratch,
        compiler_params=_params(("parallel", "parallel")),
        name="lru_latent" if latent else "lru_ctx",
    )(*ins)


def _col_groups(x, op):
    rows, cols = x.shape
    return op(x.reshape(rows // SUBLANES, SUBLANES, cols), axis=0)


def _attn_body(q_ref, kl_ref, kc_ref, vl_ref, vc_ref, o_ref, s0_ref, s1_ref, p0_ref, p1_ref, o0_ref):
    s_len = kl_ref.shape[1]
    c_len = kc_ref.shape[1]
    tk = min(ATTN_TK, s_len)
    sub = s0_ref.shape[1]
    n_sub = q_ref.shape[1] // sub
    chunks = [(kc_ref, 0, c_len, 0)] + [(kl_ref, ki * tk, tk, c_len + ki * tk) for ki in range(s_len // tk)]
    s_refs, p_refs = (s0_ref, s1_ref), (p0_ref, p1_ref)

    def qrows(qt):
        return pl.ds(pl.multiple_of(qt * sub, sub), sub)

    def score_pass(qt, hd):
        lanes = slice(hd * HEAD_PAD, (hd + 1) * HEAD_PAD)
        q = q_ref[0, qrows(qt), lanes]
        m8 = None
        for k_ref, r0, rows, off in chunks:
            sk = _dot_nt(k_ref[0, r0:r0 + rows, lanes], q)
            s_refs[hd][off:off + rows, :] = sk
            g = _col_groups(sk, jnp.max)
            m8 = g if m8 is None else jnp.maximum(m8, g)
        return jnp.max(m8, axis=0, keepdims=True)

    def exp_pass(hd, m):
        for _, _, rows, off in chunks:
            p_refs[hd][off:off + rows, :] = jnp.exp2(s_refs[hd][off:off + rows, :] - m).astype(BF16)

    def value_pass(hd):
        vrows = slice(hd * VT_ROWS, (hd + 1) * VT_ROWS)
        acc = (_dot(vc_ref[0, vrows, :], p_refs[hd][0:c_len, :])
               + _dot(vl_ref[0, vrows, :], p_refs[hd][c_len:c_len + s_len, :]))
        return acc[:V_HEAD_DIM] / acc[V_HEAD_DIM:V_HEAD_DIM + 1]

    def store(qt, o0, o1):
        o_ref[0, qrows(qt), :] = jnp.concatenate([o0, o1], axis=0).T.astype(BF16)

    def finish(hd, m):
        exp_pass(hd, m)
        return value_pass(hd)

    def head1_trip(qt, m_prev):
        o0_ref[...] = finish(0, m_prev)
        return score_pass(qt, 1)

    def head0_trip(qt, m_prev):
        store(qt - 1, o0_ref[...], finish(1, m_prev))
        return score_pass(qt, 0)

    def trip(u, m_prev):
        return lax.cond(u % 2 == 1, head1_trip, head0_trip, u // 2, m_prev)

    m = lax.fori_loop(1, 2 * n_sub, trip, score_pass(0, 0))
    store(n_sub - 1, o0_ref[...], finish(1, m))


def _attn_call(q, k_l, k_c, vt_l, vt_c):
    b, s, _ = q.shape
    c_len = k_c.shape[1]
    tq = min(ATTN_TQ, s)
    sub = min(ATTN_SUB, tq)
    pairs = N_HEADS // 2
    return pl.pallas_call(
        _attn_body,
        grid=(b, pairs, s // tq),
        in_specs=[
            pl.BlockSpec((1, tq, 2 * HEAD_PAD), lambda i, j, t: (i, t, j)),
            pl.BlockSpec((1, s, 2 * HEAD_PAD), lambda i, j, t: (i, 0, j)),
            pl.BlockSpec((1, c_len, 2 * HEAD_PAD), lambda i, j, t: (i, 0, j)),
            pl.BlockSpec((1, 2 * VT_ROWS, s), lambda i, j, t: (i, j, 0)),
            pl.BlockSpec((1, 2 * VT_ROWS, c_len), lambda i, j, t: (i, j, 0)),
        ],
        out_specs=pl.BlockSpec((1, tq, 2 * V_HEAD_DIM), lambda i, j, t: (i, t, j)),
        out_shape=jax.ShapeDtypeStruct((b, s, N_HEADS * V_HEAD_DIM), BF16),
        scratch_shapes=([pltpu.VMEM((c_len + s, sub), F32)] * 2 + [pltpu.VMEM((c_len + s, sub), BF16)] * 2
                        + [pltpu.VMEM((V_HEAD_DIM, sub), F32)]),
        compiler_params=_params(("parallel", "parallel", "parallel")),
        name="attn",
    )(q, k_l, k_c, vt_l, vt_c)


def _merge_body(attn_ref, m_ref, gl_ref, x_ref, g1_ref, bg_ref, woa_ref, wol_ref, wout_ref, o_ref):
    tm = x_ref.shape[1]
    sub = tm // MERGE_SUBTILES
    rows = [slice(i * sub, (i + 1) * sub) for i in range(MERGE_SUBTILES)]
    ys = [(_dot(attn_ref[0, r, :], woa_ref[...]), _dot(m_ref[0, r, :], wol_ref[...])) for r in rows]
    for r, (y_a, y_b) in zip(rows, ys):
        gates = 0.5 + 0.5 * jnp.tanh(0.5 * (gl_ref[0, r, :] + bg_ref[...]))
        mix = gates[:, :D_MODEL] * y_a + gates[:, D_MODEL:] * y_b
        o_ref[0, r, :] = x_ref[0, r, :] + g1_ref[0] * _dot(mix.astype(BF16), wout_ref[...])


def _merge_call(attn, m, gl, x, g1, w):
    b, s, _ = x.shape
    tm = min(MERGE_TM, s)
    row = lambda width: pl.BlockSpec((1, tm, width), lambda i, j: (i, j, 0))
    modrow = pl.BlockSpec((1, 1, D_MODEL), lambda i, j: (i, 0, 0))
    consts = [w['b_gate'], w['o_attn'], w['o_lru'], w['out']]
    return pl.pallas_call(
        _merge_body,
        grid=(b, s // tm),
        in_specs=[row(N_HEADS * V_HEAD_DIM), row(LRU_WIDTH), row(2 * D_MODEL), row(D_MODEL), modrow]
        + [_const_spec(a.shape) for a in consts],
        out_specs=row(D_MODEL),
        out_shape=jax.ShapeDtypeStruct((b, s, D_MODEL), F32),
        compiler_params=_params(("parallel", "parallel")),
        name="merge",
    )(attn, m, gl, x, g1, *consts)


def _ffn_body(xp_ref, x_ref, xn_ref, sh_ref, sc_ref, g2_ref, n2_ref, fg_ref, wup_ref, cw_ref, cb_ref,
              wdn_ref, o_ref, f_ref):
    j = pl.program_id(1)
    nj = pl.num_programs(1)
    tm = x_ref.shape[1]
    x = x_ref[0]
    xe = jnp.concatenate([xp_ref[0], x, xn_ref[0]], axis=0)
    hf = _rms(xe, n2_ref[...]) * (1.0 + sc_ref[0]) + sh_ref[0]
    keep_top = (j > 0).astype(F32)
    keep_bot = (j < nj - 1).astype(F32)
    ht = hf[HALO:HALO + tm]
    he = jnp.concatenate([hf[:HALO] * keep_top, ht, hf[HALO + tm:] * keep_bot], axis=0).astype(BF16)
    ht = ht.astype(BF16)
    ext = tm + 2 * HALO
    for ci in range(FFN_DIM // FFN_FC):
        cols = slice(ci * FFN_FC, (ci + 1) * FFN_FC)
        gcols = slice(FFN_DIM + ci * FFN_FC, FFN_DIM + (ci + 1) * FFN_FC)
        a = _dot(he, wup_ref[:, cols])
        cw = cw_ref[:, cols]
        conv = (cb_ref[:, cols] + pltpu.roll(a, 1, 0)[HALO:HALO + tm] * cw[0:1]
                + a[HALO:HALO + tm] * cw[1:2]
                + pltpu.roll(a, ext - 1, 0)[HALO:HALO + tm] * cw[2:3])
        g = _dot(ht, wup_ref[:, gcols])
        f_ref[:, cols] = (conv * jax.nn.sigmoid(conv) * g).astype(BF16)
    y = x + g2_ref[0] * _dot(f_ref[...], wdn_ref[...])
    o_ref[0] = _rms(y, fg_ref[...])


def _ffn_call(x1, sh2, sc2, g2, w):
    b, s, _ = x1.shape
    tm = min(FFN_TM, s)
    per = tm // HALO
    nblk = s // HALO
    row = pl.BlockSpec((1, tm, D_MODEL), lambda i, j: (i, j, 0))
    prev = pl.BlockSpec((1, HALO, D_MODEL), lambda i, j: (i, jnp.maximum(j * per - 1, 0), 0))
    nxt = pl.BlockSpec((1, HALO, D_MODEL), lambda i, j: (i, jnp.minimum((j + 1) * per, nblk - 1), 0))
    modrow = pl.BlockSpec((1, 1, D_MODEL), lambda i, j: (i, 0, 0))
    consts = [w['norm2_g'], w['final_g'], w['up'], w['ffn_conv_w'], w['ffn_conv_b'], w['down']]
    return pl.pallas_call(
        _ffn_body,
        grid=(b, s // tm),
        in_specs=[prev, row, nxt, modrow, modrow, modrow] + [_const_spec(a.shape) for a in consts],
        out_specs=row,
        out_shape=jax.ShapeDtypeStruct((b, s, D_MODEL), F32),
        scratch_shapes=[pltpu.VMEM((tm, FFN_DIM), BF16)],
        compiler_params=_params(("parallel", "parallel")),
        name="ffn",
    )(x1, x1, x1, sh2, sc2, g2, *consts)


def _rope_tables(n):
    rows = n // GRID_W
    row_ids = jnp.repeat(jnp.arange(rows), GRID_W).astype(F32)
    col_ids = jnp.tile(jnp.arange(GRID_W), rows).astype(F32)
    axis_dim = QK_ROPE_DIM // 2
    inv = 1.0 / (ROPE_BASE ** (jnp.arange(0, axis_dim, 2, dtype=F32) / axis_dim))
    ang = jnp.concatenate([row_ids[:, None] * inv, col_ids[:, None] * inv], axis=-1)
    cos, sin = jnp.cos(ang), jnp.sin(ang)
    ones = lambda w_: jnp.ones((n, w_), F32)
    zeros = lambda w_: jnp.zeros((n, w_), F32)
    tail = HEAD_PAD - QK_DIM
    c = jnp.concatenate([ones(QK_NOPE_DIM), cos, cos, ones(tail)], axis=-1)
    s1 = jnp.concatenate([zeros(QK_NOPE_DIM), -sin, zeros(ROPE_HALF + tail)], axis=-1)
    s2 = jnp.concatenate([zeros(QK_NOPE_DIM + ROPE_HALF), sin, zeros(tail)], axis=-1)
    return c, s1, s2


def _identity_tables(n):
    return (jnp.ones((n, HEAD_PAD), F32), jnp.zeros((n, HEAD_PAD), F32), jnp.zeros((n, HEAD_PAD), F32))


def _prep_weights(w_in, q_norm_g, kv_norm_g, w_uq, w_ukv, w_o_attn, lru_conv_w, lru_conv_b, lru_w_a,
                  lru_b_a, lru_w_x, lru_b_x, lru_lambda, w_o_lru, w_out, b_gate, norm2_g, w_up,
                  ffn_conv_w, ffn_conv_b, w_down, final_g):
    w = {}
    kr = jnp.pad(w_in[:, OFF_KR:OFF_XB], ((0, 0), (QK_NOPE_DIM, HEAD_PAD - QK_DIM)))
    w['in_a'] = jnp.concatenate([w_in[:, :OFF_KR], kr], axis=1).astype(BF16)
    w['in_b'] = w_in[:, OFF_XB:].astype(BF16)
    w['gq'] = q_norm_g[None, :]
    w['gkv'] = kv_norm_g[None, :]
    uq = w_uq.reshape(Q_LORA_RANK, N_HEADS, QK_DIM)
    w['uq'] = jnp.pad(uq, ((0, 0), (0, 0), (0, HEAD_PAD - QK_DIM))).reshape(Q_LORA_RANK, -1).astype(BF16)
    ukv = w_ukv.reshape(KV_LORA_RANK, N_HEADS, QK_NOPE_DIM + V_HEAD_DIM)
    w['uk'] = jnp.pad(ukv[..., :QK_NOPE_DIM],
                      ((0, 0), (0, 0), (0, HEAD_PAD - QK_NOPE_DIM))).reshape(KV_LORA_RANK, -1).astype(BF16)
    w['uv'] = ukv[..., QK_NOPE_DIM:].reshape(KV_LORA_RANK, -1).T.astype(BF16)
    w['conv_w'] = lru_conv_w
    w['conv_b'] = lru_conv_b[None, :]
    gw = 0.5 * jnp.concatenate([lru_w_a[0], lru_w_x[0], lru_w_a[1], lru_w_x[1]], axis=-1)
    gb = 0.5 * jnp.stack([lru_b_a[0], lru_b_x[0], lru_b_a[1], lru_b_x[1]], axis=0)
    gb = gb.reshape(4, LRU_BLOCKS, LRU_BLOCK_W).transpose(1, 0, 2).reshape(LRU_BLOCKS, 1, -1)
    terms = []
    for _ in range(GATE_BIAS_ROWS):
        t = gb * VELTKAMP_8BIT
        hi = t - (t - gb)
        terms.append(hi)
        gb = gb - hi
    bias_rows = jnp.pad(jnp.concatenate(terms, axis=1), ((0, 0), (0, LANES - GATE_BIAS_ROWS), (0, 0)))
    w['gate_w'] = jnp.concatenate([gw, bias_rows], axis=1).astype(BF16)
    w['lam'] = lru_lambda.reshape(2, LRU_BLOCKS, LRU_BLOCK_W).transpose(1, 0, 2)
    w['o_attn'] = w_o_attn.astype(BF16)
    w['o_lru'] = w_o_lru.astype(BF16)
    w['out'] = w_out.astype(BF16)
    w['b_gate'] = b_gate[None, :]
    w['norm2_g'] = norm2_g[None, :]
    w['final_g'] = final_g[None, :]
    w['up'] = w_up.astype(BF16)
    w['ffn_conv_w'] = ffn_conv_w
    w['ffn_conv_b'] = ffn_conv_b[None, :]
    w['down'] = w_down.astype(BF16)
    return w


def kernel(x, c, ctx, c_ctx, w_mod, b_mod, norm1_g, w_in, b_gate, q_norm_g, kv_norm_g, w_uq, w_ukv,
           w_o_attn, lru_conv_w, lru_conv_b, lru_w_a, lru_b_a, lru_w_x, lru_b_x, lru_lambda, w_o_lru,
           w_out, norm2_g, w_up, ffn_conv_w, ffn_conv_b, w_down, final_g):
    assert w_mod.shape[0] == 1, "single-layer block"
    b, s, _ = x.shape
    c_len = ctx.shape[1]
    assert b + 1 <= MOD_ROWS

    w = _prep_weights(w_in[0], q_norm_g[0], kv_norm_g[0], w_uq[0], w_ukv[0], w_o_attn[0], lru_conv_w[0],
                      lru_conv_b[0], lru_w_a[0], lru_b_a[0], lru_w_x[0], lru_b_x[0], lru_lambda[0],
                      w_o_lru[0], w_out[0], b_gate[0], norm2_g[0], w_up[0], ffn_conv_w[0],
                      ffn_conv_b[0], w_down[0], final_g)
    g1n = norm1_g[0][None, :]

    cc = jnp.concatenate([c, c_ctx[None, :], jnp.zeros((MOD_ROWS - b - 1, D_MODEL), F32)], axis=0)
    mod = _mod_call(cc, w_mod, b_mod)
    mod_l = mod[:b].reshape(b, 1, 6, D_MODEL)
    sh1, sc1, g1, sh2, sc2, g2 = (mod_l[:, :, i] for i in range(6))
    mod_c = jnp.broadcast_to(mod[b].reshape(1, 1, 6, D_MODEL), (b, 1, 6, D_MODEL))
    sh1c, sc1c = mod_c[:, :, 0], mod_c[:, :, 1]

    k_c, v_c, xb_c = _proj_call(False, ctx, sh1c, sc1c, g1n, w, _identity_tables(c_len))
    states = _lru_call(False, xb_c, w)

    k_l, v_l, xb_l, q_l, yb_l, gl_l = _proj_call(True, x, sh1, sc1, g1n, w, _rope_tables(s))
    m = _lru_call(True, xb_l, w, yb=yb_l, h0=states)
    attn = _attn_call(q_l, k_l, k_c, v_l, v_c)
    x1 = _merge_call(attn, m, gl_l, x, g1, w)
    return _ffn_call(x1, sh2, sc2, g2, w)
```

```python
import functools

import jax
import jax.numpy as jnp
from jax import lax
from jax.experimental import pallas as pl
from jax.experimental.pallas import tpu as pltpu

F32 = jnp.float32
BF16 = jnp.bfloat16

D_MODEL = 1024
GRID_W = 64
N_HEADS = 8
QK_NOPE_DIM = 64
QK_ROPE_DIM = 32
ROPE_HALF = QK_ROPE_DIM // 2
V_HEAD_DIM = 64
QK_DIM = QK_NOPE_DIM + QK_ROPE_DIM
HEAD_PAD = 128
Q_LORA_RANK = 384
KV_LORA_RANK = 256
ROPE_BASE = 10000.0
LRU_WIDTH = 1280
LRU_BLOCKS = 10
LRU_BLOCK_W = LRU_WIDTH // LRU_BLOCKS
LRU_CONV_W = 4
LRU_C = 8.0
FFN_DIM = 2816
FFN_CONV_W = 3
EPS = 1e-6
TINY = 1e-30
OFF_KV = Q_LORA_RANK
OFF_KR = OFF_KV + KV_LORA_RANK
OFF_XB = OFF_KR + QK_ROPE_DIM
OFF_YB = OFF_XB + LRU_WIDTH
OFF_G = OFF_YB + LRU_WIDTH
IN_Q = 0
IN_KV = IN_Q + Q_LORA_RANK
IN_KR = IN_KV + KV_LORA_RANK
IN_A_END = IN_KR + HEAD_PAD
IN_XB = 0
IN_YB = IN_XB + LRU_WIDTH
IN_GL = IN_YB + LRU_WIDTH
IN_END = IN_GL + 2 * D_MODEL

SUBLANES = 8
VT_ROWS = 80
LANES = 128
VMEM_LIMIT = 56 * 1024 * 1024

MOD_ROWS = 8
MOD_TN = 768
PROJ_TM = 512
LRU_TC = 256
LRU_UNROLL = 16
LRU_FIX_ROWS = 64
GATE_BIAS_ROWS = 3
VELTKAMP_8BIT = 65537.0
LOG2_E = 1.4426950408889634
ATTN_TQ = 4096
ATTN_SUB = 256
ATTN_TK = 256
MERGE_TM = 512
MERGE_SUBTILES = 2
FFN_TM = 512
FFN_FC = 256
HALO = SUBLANES


def _dot(a, b):
    return jnp.dot(a, b, preferred_element_type=F32)


def _dot_nt(a, b):
    return lax.dot_general(a, b, (((1,), (1,)), ((), ())), preferred_element_type=F32)


def _rms(x, g):
    return x * lax.rsqrt(jnp.mean(x * x, axis=-1, keepdims=True) + EPS) * g


def _const_spec(shape):
    nd = len(shape)
    return pl.BlockSpec(shape, lambda *_: (0,) * nd, pipeline_mode=pl.Buffered(1))


def _params(sem):
    return pltpu.CompilerParams(dimension_semantics=sem, vmem_limit_bytes=VMEM_LIMIT)


def _mod_body(c_ref, w_ref, b_ref, o_ref):
    c = c_ref[...]
    s = c * jax.nn.sigmoid(c)
    o_ref[...] = _dot(s.astype(BF16), w_ref[...].astype(BF16)) + b_ref[...]


def _mod_call(cc, w_mod, b_mod):
    n = w_mod.shape[-1]
    return pl.pallas_call(
        _mod_body,
        grid=(n // MOD_TN,),
        in_specs=[
            pl.BlockSpec((MOD_ROWS, D_MODEL), lambda j: (0, 0)),
            pl.BlockSpec((None, D_MODEL, MOD_TN), lambda j: (0, 0, j)),
            pl.BlockSpec((1, MOD_TN), lambda j: (0, j)),
        ],
        out_specs=pl.BlockSpec((MOD_ROWS, MOD_TN), lambda j: (0, j)),
        out_shape=jax.ShapeDtypeStruct((MOD_ROWS, n), F32),
        compiler_params=_params(("arbitrary",)),
        name="mod",
    )(cc, w_mod, b_mod)


def _rope128(t, c, s1, s2):
    return (t * c + pltpu.roll(t, HEAD_PAD - ROPE_HALF, 1) * s1
            + pltpu.roll(t, ROPE_HALF, 1) * s2)


def _proj_body(latent, xp_ref, x_ref, xn_ref, sh_ref, sc_ref, g1_ref, wa_ref, wb_ref, gkv_ref, wuk_ref,
               wuv_ref, cw_ref, cb_ref, c_ref, s1_ref, s2_ref, *rest):
    if latent:
        gq_ref, wuq_ref, k_ref, v_ref, xc_ref, q_ref, yb_ref, gl_ref = rest
    else:
        k_ref, v_ref, xc_ref = rest
    i = pl.program_id(0)
    tm = x_ref.shape[1]
    xe = jnp.concatenate([xp_ref[0], x_ref[0], xn_ref[0]], axis=0)
    hf = _rms(xe, g1_ref[...]) * (1.0 + sc_ref[0]) + sh_ref[0]
    keep_top = (i > 0).astype(F32)
    keep_bot = (i < pl.num_programs(0) - 1).astype(F32)
    h = hf[HALO:HALO + tm]
    he = jnp.concatenate([hf[:HALO] * keep_top, h, hf[HALO + tm:] * keep_bot], axis=0).astype(BF16)
    hb = h.astype(BF16)
    c, s1, s2 = c_ref[...], s1_ref[...], s2_ref[...]

    if latent:
        lat = _dot(hb, wa_ref[...])
        q_lat = lat[:, :IN_KV - IN_Q]
        lat = lat[:, IN_KV - IN_Q:]
    else:
        lat = _dot(hb, wa_ref[:, IN_KV:IN_A_END])
    kv_lat, kr_raw = lat[:, :IN_KR - IN_KV], lat[:, IN_KR - IN_KV:]
    xb = _dot(he, wb_ref[:, IN_XB:IN_YB])
    ext = tm + 2 * HALO
    cw = cw_ref[...]
    xc_ref[0] = (cb_ref[...] + pltpu.roll(xb, 2, 0)[HALO:HALO + tm] * cw[0:1]
                 + pltpu.roll(xb, 1, 0)[HALO:HALO + tm] * cw[1:2]
                 + xb[HALO:HALO + tm] * cw[2:3]
                 + pltpu.roll(xb, ext - 1, 0)[HALO:HALO + tm] * cw[3:4])
    kvn = _rms(kv_lat, gkv_ref[...]).astype(BF16)
    kk = _dot(kvn, wuk_ref[...])
    vt = _dot_nt(wuv_ref[...], kvn)
    ones = jnp.ones((VT_ROWS - V_HEAD_DIM, vt.shape[1]), F32)
    v_ref[0] = jnp.concatenate([piece for hd in range(N_HEADS)
                                for piece in (vt[hd * V_HEAD_DIM:(hd + 1) * V_HEAD_DIM], ones)],
                               axis=0).astype(BF16)
    if latent:
        qn = _rms(q_lat, gq_ref[...]).astype(BF16)
        qq = _dot(qn, wuq_ref[...])
        yb_ref[0] = _gelu_tanh(_dot(hb, wb_ref[:, IN_YB:IN_GL]))
        gl_ref[0] = _dot(hb, wb_ref[:, IN_GL:IN_END])
    kr = _rope128(kr_raw, c, s1, s2)
    for hd in range(N_HEADS):
        sl = slice(hd * HEAD_PAD, (hd + 1) * HEAD_PAD)
        k_ref[0, :, sl] = (kk[:, sl] + kr).astype(BF16)
    if latent:
        scale = QK_DIM ** -0.5 * LOG2_E
        for hd in range(N_HEADS):
            sl = slice(hd * HEAD_PAD, (hd + 1) * HEAD_PAD)
            q_ref[0, :, sl] = (_rope128(qq[:, sl], c, s1, s2) * scale).astype(BF16)


def _proj_call(latent, x, sh, sc, g1, w, tabs):
    b, s, _ = x.shape
    tm = min(PROJ_TM, s)
    row = lambda width: pl.BlockSpec((1, tm, width), lambda i, j: (j, i, 0))
    modrow = pl.BlockSpec((1, 1, D_MODEL), lambda i, j: (j, 0, 0))
    tab = pl.BlockSpec((tm, HEAD_PAD), lambda i, j: (i, 0))
    per = tm // HALO
    nblk = s // HALO
    prev = pl.BlockSpec((1, HALO, D_MODEL), lambda i, j: (j, jnp.maximum(i * per - 1, 0), 0))
    nxt = pl.BlockSpec((1, HALO, D_MODEL), lambda i, j: (j, jnp.minimum((i + 1) * per, nblk - 1), 0))
    ins = [x, x, x, sh, sc, g1, w['in_a'], w['in_b'], w['gkv'], w['uk'], w['uv'], w['conv_w'], w['conv_b'], *tabs]
    in_specs = ([prev, row(D_MODEL), nxt, modrow, modrow] + [_const_spec(a.shape) for a in ins[5:13]]
                + [tab] * 3)
    widths = [N_HEADS * HEAD_PAD, None, LRU_WIDTH]
    dtypes = [BF16, BF16, F32]
    if latent:
        extra = [w['gq'], w['uq']]
        ins += extra
        in_specs += [_const_spec(a.shape) for a in extra]
        widths += [N_HEADS * HEAD_PAD, LRU_WIDTH, 2 * D_MODEL]
        dtypes += [BF16, F32, F32]
    hv = N_HEADS * VT_ROWS
    out_specs = [row(wd) for wd in widths if wd is not None]
    out_shape = [jax.ShapeDtypeStruct((b, s, wd), dt) for wd, dt in zip(widths, dtypes) if wd is not None]
    out_specs.insert(1, pl.BlockSpec((1, hv, tm), lambda i, j: (j, 0, i)))
    out_shape.insert(1, jax.ShapeDtypeStruct((b, hv, s), BF16))
    return pl.pallas_call(
        functools.partial(_proj_body, latent),
        grid=(s // tm, b),
        in_specs=in_specs,
        out_specs=out_specs,
        out_shape=out_shape,
        compiler_params=_params(("parallel", "parallel")),
        name="proj_latent" if latent else "proj_ctx",
    )(*ins)


def _gelu_tanh(x):
    return 0.5 * x * (1.0 + jnp.tanh(0.7978845608028654 * (x + 0.044715 * (x * x * x))))


def _lru_body(latent, xc_ref, wg_ref, lam_ref, *rest):
    if latent:
        yb_ref, h0_ref, out_ref, gates, scans = rest
    else:
        st_ref, gates, scans = rest
    s = xc_ref.shape[1]
    af, uf, ab, ub = (gates.at[pl.ds(k * s, s)] for k in range(4))
    hfl, pfl, hbl, pbl = (scans.at[pl.ds(k * s, s)] for k in range(4))
    seg = s // SUBLANES
    tc = min(LRU_TC, s)
    nc = s // tc
    piece = min(tc, seg)

    wg = wg_ref[0]
    lam = lam_ref[0]
    hcsp = (0.5 * LRU_C) * (jnp.maximum(-lam, 0.0) + jnp.log1p(jnp.exp(-jnp.abs(lam))))
    bias_lhs = jnp.where(lax.broadcasted_iota(jnp.int32, (tc, LANES), 1) < GATE_BIAS_ROWS, 1.0, 0.0).astype(BF16)

    def interleaved(ci):
        pieces = []
        for k in range(tc // piece):
            t0 = ci * tc + k * piece
            j = t0 // seg
            pieces.append((slice(k * piece, (k + 1) * piece),
                           pl.ds((t0 - j * seg) * SUBLANES + j, piece, stride=SUBLANES)))
        return pieces

    def gate_chunk(ci, carry):
        r0 = pl.multiple_of(ci * tc, tc)
        xc = xc_ref[0, pl.ds(r0, tc), :]
        t = jnp.tanh(_dot(jnp.concatenate([xc.astype(BF16), bias_lhs], axis=1), wg))
        hx = 0.5 * xc
        for d, (a_ref, u_ref) in enumerate(((af, uf), (ab, ub))):
            t_r = t[:, (2 * d) * LANES:(2 * d + 1) * LANES]
            t_i = t[:, (2 * d + 1) * LANES:(2 * d + 2) * LANES]
            neg_log_a = hcsp[d:d + 1] + hcsp[d:d + 1] * t_r
            a = jnp.exp2(neg_log_a * (-LOG2_E))
            y = (1.0 - a) * (1.0 + a)
            mult = y * lax.rsqrt(jnp.maximum(y, TINY))
            u = (mult * hx) * (1.0 + t_i)
            for rows, rd in interleaved(ci):
                a_ref[rd, :] = a[rows]
                u_ref[rd, :] = u[rows]
        return carry

    lax.fori_loop(0, nc, gate_chunk, 0, unroll=min(8, nc))

    def scan_step(i, carry):
        hf, pf, hb, pb = carry
        fwd = pl.ds(pl.multiple_of(i * SUBLANES, SUBLANES), SUBLANES)
        bwd = pl.ds(pl.multiple_of((seg - 1 - i) * SUBLANES, SUBLANES), SUBLANES)
        a = af[fwd, :]
        hf = a * hf + uf[fwd, :]
        pf = a * pf
        hfl[fwd, :] = hf
        pfl[fwd, :] = pf
        a = ab[bwd, :]
        hb = a * hb + ub[bwd, :]
        pb = a * pb
        hbl[bwd, :] = hb
        pbl[bwd, :] = pb
        return hf, pf, hb, pb

    zeros = jnp.zeros((SUBLANES, LANES), F32)
    ones = jnp.ones((SUBLANES, LANES), F32)
    hf, pf, hb, pb = lax.fori_loop(0, seg, scan_step, (zeros, ones, zeros, ones), unroll=LRU_UNROLL)

    if latent:
        h0 = h0_ref[0]
        cf, cbk = h0[0:1], h0[1:2]
    else:
        cf = cbk = jnp.zeros((1, LANES), F32)
    cfs = []
    for j in range(SUBLANES):
        cfs.append(cf)
        cf = hf[j:j + 1] + pf[j:j + 1] * cf
    cbs = [None] * SUBLANES
    for j in reversed(range(SUBLANES)):
        cbs[j] = cbk
        cbk = hb[j:j + 1] + pb[j:j + 1] * cbk

    if latent:
        fix_rows = min(LRU_FIX_ROWS, s)
        cf_all = jnp.tile(jnp.concatenate(cfs, axis=0), (fix_rows // SUBLANES, 1))
        cb_all = jnp.tile(jnp.concatenate(cbs, axis=0), (fix_rows // SUBLANES, 1))

        def fix_chunk(ci, carry):
            rows = pl.ds(pl.multiple_of(ci * fix_rows, fix_rows), fix_rows)
            af[rows, :] = (hfl[rows, :] + pfl[rows, :] * cf_all) + (hbl[rows, :] + pbl[rows, :] * cb_all)
            return carry

        lax.fori_loop(0, s // fix_rows, fix_chunk, 0, unroll=2)

        def out_chunk(ci, carry):
            r0 = pl.multiple_of(ci * tc, tc)
            hsum = jnp.concatenate([af[rd, :] for _, rd in interleaved(ci)], axis=0)
            out_ref[0, pl.ds(r0, tc), :] = (hsum * yb_ref[0, pl.ds(r0, tc), :]).astype(BF16)
            return carry

        lax.fori_loop(0, nc, out_chunk, 0)
    else:
        st_ref[0] = jnp.concatenate([cf, cbk], axis=0)


def _lru_call(latent, xc, w, yb=None, h0=None):
    b, s, _ = xc.shape
    seq = pl.BlockSpec((1, s, LANES), lambda i, j: (i, 0, j))
    st = pl.BlockSpec((1, 2, LANES), lambda i, j: (i, 0, j))
    in_specs = [
        seq,
        pl.BlockSpec((1, 2 * LANES, 4 * LANES), lambda i, j: (j, 0, 0)),
        pl.BlockSpec((1, 2, LANES), lambda i, j: (j, 0, 0)),
    ]
    ins = [xc, w['gate_w'], w['lam']]
    scratch = [pltpu.VMEM((4 * s, LANES), F32)] * 2
    if latent:
        ins += [yb, h0]
        in_specs += [seq, st]
        out_specs = seq
        out_shape = jax.ShapeDtypeStruct((b, s, LRU_WIDTH), BF16)
    else:
        out_specs = st
        out_shape = jax.ShapeDtypeStruct((b, 2, LRU_WIDTH), F32)
    return pl.pallas_call(
        functools.partial(_lru_body, latent),
        grid=(b, LRU_BLOCKS),
        in_specs=in_specs,
        out_specs=out_specs,
        out_shape=out_shape,
        scratch_shapes=scratch,
        compiler_params=_params(("parallel", "parallel")),
        name="lru_latent" if latent else "lru_ctx",
    )(*ins)


def _col_groups(x, op):
    rows, cols = x.shape
    return op(x.reshape(rows // SUBLANES, SUBLANES, cols), axis=0)


def _attn_body(q_ref, kl_ref, kc_ref, vl_ref, vc_ref, o_ref, s0_ref, s1_ref, p0_ref, p1_ref, o0_ref):
    s_len = kl_ref.shape[1]
    c_len = kc_ref.shape[1]
    tk = min(ATTN_TK, s_len)
    sub = s0_ref.shape[1]
    n_sub = q_ref.shape[1] // sub
    chunks = [(kc_ref, 0, c_len, 0)] + [(kl_ref, ki * tk, tk, c_len + ki * tk) for ki in range(s_len // tk)]
    s_refs, p_refs = (s0_ref, s1_ref), (p0_ref, p1_ref)

    def qrows(qt):
        return pl.ds(pl.multiple_of(qt * sub, sub), sub)

    def score_pass(qt, hd):
        lanes = slice(hd * HEAD_PAD, (hd + 1) * HEAD_PAD)
        q = q_ref[0, qrows(qt), lanes]
        m8 = None
        for k_ref, r0, rows, off in chunks:
            sk = _dot_nt(k_ref[0, r0:r0 + rows, lanes], q)
            s_refs[hd][off:off + rows, :] = sk
            g = _col_groups(sk, jnp.max)
            m8 = g if m8 is None else jnp.maximum(m8, g)
        return jnp.max(m8, axis=0, keepdims=True)

    def exp_pass(hd, m):
        for _, _, rows, off in chunks:
            p_refs[hd][off:off + rows, :] = jnp.exp2(s_refs[hd][off:off + rows, :] - m).astype(BF16)

    def value_pass(hd):
        vrows = slice(hd * VT_ROWS, (hd + 1) * VT_ROWS)
        acc = (_dot(vc_ref[0, vrows, :], p_refs[hd][0:c_len, :])
               + _dot(vl_ref[0, vrows, :], p_refs[hd][c_len:c_len + s_len, :]))
        return acc[:V_HEAD_DIM] / acc[V_HEAD_DIM:V_HEAD_DIM + 1]

    def store(qt, o0, o1):
        o_ref[0, qrows(qt), :] = jnp.concatenate([o0, o1], axis=0).T.astype(BF16)

    def finish(hd, m):
        exp_pass(hd, m)
        return value_pass(hd)

    def head1_trip(qt, m_prev):
        o0_ref[...] = finish(0, m_prev)
        return score_pass(qt, 1)

    def head0_trip(qt, m_prev):
        store(qt - 1, o0_ref[...], finish(1, m_prev))
        return score_pass(qt, 0)

    def trip(u, m_prev):
        return lax.cond(u % 2 == 1, head1_trip, head0_trip, u // 2, m_prev)

    m = lax.fori_loop(1, 2 * n_sub, trip, score_pass(0, 0))
    store(n_sub - 1, o0_ref[...], finish(1, m))


def _attn_call(q, k_l, k_c, vt_l, vt_c):
    b, s, _ = q.shape
    c_len = k_c.shape[1]
    tq = min(ATTN_TQ, s)
    sub = min(ATTN_SUB, tq)
    pairs = N_HEADS // 2
    return pl.pallas_call(
        _attn_body,
        grid=(b, pairs, s // tq),
        in_specs=[
            pl.BlockSpec((1, tq, 2 * HEAD_PAD), lambda i, j, t: (i, t, j)),
            pl.BlockSpec((1, s, 2 * HEAD_PAD), lambda i, j, t: (i, 0, j)),
            pl.BlockSpec((1, c_len, 2 * HEAD_PAD), lambda i, j, t: (i, 0, j)),
            pl.BlockSpec((1, 2 * VT_ROWS, s), lambda i, j, t: (i, j, 0)),
            pl.BlockSpec((1, 2 * VT_ROWS, c_len), lambda i, j, t: (i, j, 0)),
        ],
        out_specs=pl.BlockSpec((1, tq, 2 * V_HEAD_DIM), lambda i, j, t: (i, t, j)),
        out_shape=jax.ShapeDtypeStruct((b, s, N_HEADS * V_HEAD_DIM), BF16),
        scratch_shapes=([pltpu.VMEM((c_len + s, sub), F32)] * 2 + [pltpu.VMEM((c_len + s, sub), BF16)] * 2
                        + [pltpu.VMEM((V_HEAD_DIM, sub), F32)]),
        compiler_params=_params(("parallel", "parallel", "parallel")),
        name="attn",
    )(q, k_l, k_c, vt_l, vt_c)


def _merge_body(attn_ref, m_ref, gl_ref, x_ref, g1_ref, bg_ref, woa_ref, wol_ref, wout_ref, o_ref):
    tm = x_ref.shape[1]
    sub = tm // MERGE_SUBTILES
    rows = [slice(i * sub, (i + 1) * sub) for i in range(MERGE_SUBTILES)]
    ys = [(_dot(attn_ref[0, r, :], woa_ref[...]), _dot(m_ref[0, r, :], wol_ref[...])) for r in rows]
    for r, (y_a, y_b) in zip(rows, ys):
        gates = 0.5 + 0.5 * jnp.tanh(0.5 * (gl_ref[0, r, :] + bg_ref[...]))
        mix = gates[:, :D_MODEL] * y_a + gates[:, D_MODEL:] * y_b
        o_ref[0, r, :] = x_ref[0, r, :] + g1_ref[0] * _dot(mix.astype(BF16), wout_ref[...])


def _merge_call(attn, m, gl, x, g1, w):
    b, s, _ = x.shape
    tm = min(MERGE_TM, s)
    row = lambda width: pl.BlockSpec((1, tm, width), lambda i, j: (i, j, 0))
    modrow = pl.BlockSpec((1, 1, D_MODEL), lambda i, j: (i, 0, 0))
    consts = [w['b_gate'], w['o_attn'], w['o_lru'], w['out']]
    return pl.pallas_call(
        _merge_body,
        grid=(b, s // tm),
        in_specs=[row(N_HEADS * V_HEAD_DIM), row(LRU_WIDTH), row(2 * D_MODEL), row(D_MODEL), modrow]
        + [_const_spec(a.shape) for a in consts],
        out_specs=row(D_MODEL),
        out_shape=jax.ShapeDtypeStruct((b, s, D_MODEL), F32),
        compiler_params=_params(("parallel", "parallel")),
        name="merge",
    )(attn, m, gl, x, g1, *consts)


def _ffn_body(xp_ref, x_ref, xn_ref, sh_ref, sc_ref, g2_ref, n2_ref, fg_ref, wup_ref, cw_ref, cb_ref,
              wdn_ref, o_ref, f_ref):
    j = pl.program_id(1)
    nj = pl.num_programs(1)
    tm = x_ref.shape[1]
    x = x_ref[0]
    xe = jnp.concatenate([xp_ref[0], x, xn_ref[0]], axis=0)
    hf = _rms(xe, n2_ref[...]) * (1.0 + sc_ref[0]) + sh_ref[0]
    keep_top = (j > 0).astype(F32)
    keep_bot = (j < nj - 1).astype(F32)
    ht = hf[HALO:HALO + tm]
    he = jnp.concatenate([hf[:HALO] * keep_top, ht, hf[HALO + tm:] * keep_bot], axis=0).astype(BF16)
    ht = ht.astype(BF16)
    ext = tm + 2 * HALO
    for ci in range(FFN_DIM // FFN_FC):
        cols = slice(ci * FFN_FC, (ci + 1) * FFN_FC)
        gcols = slice(FFN_DIM + ci * FFN_FC, FFN_DIM + (ci + 1) * FFN_FC)
        a = _dot(he, wup_ref[:, cols])
        cw = cw_ref[:, cols]
        conv = (cb_ref[:, cols] + pltpu.roll(a, 1, 0)[HALO:HALO + tm] * cw[0:1]
                + a[HALO:HALO + tm] * cw[1:2]
                + pltpu.roll(a, ext - 1, 0)[HALO:HALO + tm] * cw[2:3])
        g = _dot(ht, wup_ref[:, gcols])
        f_ref[:, cols] = (conv * jax.nn.sigmoid(conv) * g).astype(BF16)
    y = x + g2_ref[0] * _dot(f_ref[...], wdn_ref[...])
    o_ref[0] = _rms(y, fg_ref[...])


def _ffn_call(x1, sh2, sc2, g2, w):
    b, s, _ = x1.shape
    tm = min(FFN_TM, s)
    per = tm // HALO
    nblk = s // HALO
    row = pl.BlockSpec((1, tm, D_MODEL), lambda i, j: (i, j, 0))
    prev = pl.BlockSpec((1, HALO, D_MODEL), lambda i, j: (i, jnp.maximum(j * per - 1, 0), 0))
    nxt = pl.BlockSpec((1, HALO, D_MODEL), lambda i, j: (i, jnp.minimum((j + 1) * per, nblk - 1), 0))
    modrow = pl.BlockSpec((1, 1, D_MODEL), lambda i, j: (i, 0, 0))
    consts = [w['norm2_g'], w['final_g'], w['up'], w['ffn_conv_w'], w['ffn_conv_b'], w['down']]
    return pl.pallas_call(
        _ffn_body,
        grid=(b, s // tm),
        in_specs=[prev, row, nxt, modrow, modrow, modrow] + [_const_spec(a.shape) for a in consts],
        out_specs=row,
        out_shape=jax.ShapeDtypeStruct((b, s, D_MODEL), F32),
        scratch_shapes=[pltpu.VMEM((tm, FFN_DIM), BF16)],
        compiler_params=_params(("parallel", "parallel")),
        name="ffn",
    )(x1, x1, x1, sh2, sc2, g2, *consts)


def _rope_tables(n):
    rows = n // GRID_W
    row_ids = jnp.repeat(jnp.arange(rows), GRID_W).astype(F32)
    col_ids = jnp.tile(jnp.arange(GRID_W), rows).astype(F32)
    axis_dim = QK_ROPE_DIM // 2
    inv = 1.0 / (ROPE_BASE ** (jnp.arange(0, axis_dim, 2, dtype=F32) / axis_dim))
    ang = jnp.concatenate([row_ids[:, None] * inv, col_ids[:, None] * inv], axis=-1)
    cos, sin = jnp.cos(ang), jnp.sin(ang)
    ones = lambda w_: jnp.ones((n, w_), F32)
    zeros = lambda w_: jnp.zeros((n, w_), F32)
    tail = HEAD_PAD - QK_DIM
    c = jnp.concatenate([ones(QK_NOPE_DIM), cos, cos, ones(tail)], axis=-1)
    s1 = jnp.concatenate([zeros(QK_NOPE_DIM), -sin, zeros(ROPE_HALF + tail)], axis=-1)
    s2 = jnp.concatenate([zeros(QK_NOPE_DIM + ROPE_HALF), sin, zeros(tail)], axis=-1)
    return c, s1, s2


def _identity_tables(n):
    return (jnp.ones((n, HEAD_PAD), F32), jnp.zeros((n, HEAD_PAD), F32), jnp.zeros((n, HEAD_PAD), F32))


def _prep_weights(w_in, q_norm_g, kv_norm_g, w_uq, w_ukv, w_o_attn, lru_conv_w, lru_conv_b, lru_w_a,
                  lru_b_a, lru_w_x, lru_b_x, lru_lambda, w_o_lru, w_out, b_gate, norm2_g, w_up,
                  ffn_conv_w, ffn_conv_b, w_down, final_g):
    w = {}
    kr = jnp.pad(w_in[:, OFF_KR:OFF_XB], ((0, 0), (QK_NOPE_DIM, HEAD_PAD - QK_DIM)))
    w['in_a'] = jnp.concatenate([w_in[:, :OFF_KR], kr], axis=1).astype(BF16)
    w['in_b'] = w_in[:, OFF_XB:].astype(BF16)
    w['gq'] = q_norm_g[None, :]
    w['gkv'] = kv_norm_g[None, :]
    uq = w_uq.reshape(Q_LORA_RANK, N_HEADS, QK_DIM)
    w['uq'] = jnp.pad(uq, ((0, 0), (0, 0), (0, HEAD_PAD - QK_DIM))).reshape(Q_LORA_RANK, -1).astype(BF16)
    ukv = w_ukv.reshape(KV_LORA_RANK, N_HEADS, QK_NOPE_DIM + V_HEAD_DIM)
    w['uk'] = jnp.pad(ukv[..., :QK_NOPE_DIM],
                      ((0, 0), (0, 0), (0, HEAD_PAD - QK_NOPE_DIM))).reshape(KV_LORA_RANK, -1).astype(BF16)
    w['uv'] = ukv[..., QK_NOPE_DIM:].reshape(KV_LORA_RANK, -1).T.astype(BF16)
    w['conv_w'] = lru_conv_w
    w['conv_b'] = lru_conv_b[None, :]
    gw = 0.5 * jnp.concatenate([lru_w_a[0], lru_w_x[0], lru_w_a[1], lru_w_x[1]], axis=-1)
    gb = 0.5 * jnp.stack([lru_b_a[0], lru_b_x[0], lru_b_a[1], lru_b_x[1]], axis=0)
    gb = gb.reshape(4, LRU_BLOCKS, LRU_BLOCK_W).transpose(1, 0, 2).reshape(LRU_BLOCKS, 1, -1)
    terms = []
    for _ in range(GATE_BIAS_ROWS):
        t = gb * VELTKAMP_8BIT
        hi = t - (t - gb)
        terms.append(hi)
        gb = gb - hi
    bias_rows = jnp.pad(jnp.concatenate(terms, axis=1), ((0, 0), (0, LANES - GATE_BIAS_ROWS), (0, 0)))
    w['gate_w'] = jnp.concatenate([gw, bias_rows], axis=1).astype(BF16)
    w['lam'] = lru_lambda.reshape(2, LRU_BLOCKS, LRU_BLOCK_W).transpose(1, 0, 2)
    w['o_attn'] = w_o_attn.astype(BF16)
    w['o_lru'] = w_o_lru.astype(BF16)
    w['out'] = w_out.astype(BF16)
    w['b_gate'] = b_gate[None, :]
    w['norm2_g'] = norm2_g[None, :]
    w['final_g'] = final_g[None, :]
    w['up'] = w_up.astype(BF16)
    w['ffn_conv_w'] = ffn_conv_w
    w['ffn_conv_b'] = ffn_conv_b[None, :]
    w['down'] = w_down.astype(BF16)
    return w


def kernel(x, c, ctx, c_ctx, w_mod, b_mod, norm1_g, w_in, b_gate, q_norm_g, kv_norm_g, w_uq, w_ukv,
           w_o_attn, lru_conv_w, lru_conv_b, lru_w_a, lru_b_a, lru_w_x, lru_b_x, lru_lambda, w_o_lru,
           w_out, norm2_g, w_up, ffn_conv_w, ffn_conv_b, w_down, final_g):
    assert w_mod.shape[0] == 1, "single-layer block"
    b, s, _ = x.shape
    c_len = ctx.shape[1]
    assert b + 1 <= MOD_ROWS

    w = _prep_weights(w_in[0], q_norm_g[0], kv_norm_g[0], w_uq[0], w_ukv[0], w_o_attn[0], lru_conv_w[0],
                      lru_conv_b[0], lru_w_a[0], lru_b_a[0], lru_w_x[0], lru_b_x[0], lru_lambda[0],
                      w_o_lru[0], w_out[0], b_gate[0], norm2_g[0], w_up[0], ffn_conv_w[0],
                      ffn_conv_b[0], w_down[0], final_g)
    g1n = norm1_g[0][None, :]

    cc = jnp.concatenate([c, c_ctx[None, :], jnp.zeros((MOD_ROWS - b - 1, D_MODEL), F32)], axis=0)
    mod = _mod_call(cc, w_mod, b_mod)
    mod_l = mod[:b].reshape(b, 1, 6, D_MODEL)
    sh1, sc1, g1, sh2, sc2, g2 = (mod_l[:, :, i] for i in range(6))
    mod_c = jnp.broadcast_to(mod[b].reshape(1, 1, 6, D_MODEL), (b, 1, 6, D_MODEL))
    sh1c, sc1c = mod_c[:, :, 0], mod_c[:, :, 1]

    k_c, v_c, xb_c = _proj_call(False, ctx, sh1c, sc1c, g1n, w, _identity_tables(c_len))
    states = _lru_call(False, xb_c, w)

    k_l, v_l, xb_l, q_l, yb_l, gl_l = _proj_call(True, x, sh1, sc1, g1n, w, _rope_tables(s))
    m = _lru_call(True, xb_l, w, yb=yb_l, h0=states)
    attn = _attn_call(q_l, k_l, k_c, v_l, v_c)
    x1 = _merge_call(attn, m, gl_l, x, g1, w)
    return _ffn_call(x1, sh2, sc2, g2, w)
```

```python
import functools

import jax
import jax.numpy as jnp
from jax import lax
from jax.experimental import pallas as pl
from jax.experimental.pallas import tpu as pltpu

F32 = jnp.float32
BF16 = jnp.bfloat16

D_MODEL = 1024
GRID_W = 64
N_HEADS = 8
QK_NOPE_DIM = 64
QK_ROPE_DIM = 32
ROPE_HALF = QK_ROPE_DIM // 2
V_HEAD_DIM = 64
QK_DIM = QK_NOPE_DIM + QK_ROPE_DIM
HEAD_PAD = 128
Q_LORA_RANK = 384
KV_LORA_RANK = 256
ROPE_BASE = 10000.0
LRU_WIDTH = 1280
LRU_BLOCKS = 10
LRU_BLOCK_W = LRU_WIDTH // LRU_BLOCKS
LRU_CONV_W = 4
LRU_C = 8.0
FFN_DIM = 2816
FFN_CONV_W = 3
EPS = 1e-6
TINY = 1e-30
OFF_KV = Q_LORA_RANK
OFF_KR = OFF_KV + KV_LORA_RANK
OFF_XB = OFF_KR + QK_ROPE_DIM
OFF_YB = OFF_XB + LRU_WIDTH
OFF_G = OFF_YB + LRU_WIDTH
IN_Q = 0
IN_KV = IN_Q + Q_LORA_RANK
IN_KR = IN_KV + KV_LORA_RANK
IN_A_END = IN_KR + HEAD_PAD
IN_XB = 0
IN_YB = IN_XB + LRU_WIDTH
IN_GL = IN_YB + LRU_WIDTH
IN_END = IN_GL + 2 * D_MODEL

SUBLANES = 8
VT_ROWS = 80
LANES = 128
VMEM_LIMIT = 56 * 1024 * 1024

MOD_ROWS = 8
MOD_SHIFT1, MOD_SCALE1, MOD_GATE1, MOD_SHIFT2, MOD_SCALE2, MOD_GATE2 = range(6)
MOD_TN = 768
PROJ_TM = 512
LRU_TC = 256
LRU_UNROLL = 16
LRU_FIX_ROWS = 64
GATE_BIAS_ROWS = 3
VELTKAMP_8BIT = 65537.0
LOG2_E = 1.4426950408889634
ATTN_TQ = 4096
ATTN_SUB = 256
ATTN_TK = 256
MERGE_TM = 512
MERGE_SUBTILES = 2
FFN_TM = 512
FFN_FC = 256
HALO = SUBLANES


def _dot(a, b):
    return jnp.dot(a, b, preferred_element_type=F32)


def _dot_nt(a, b):
    return lax.dot_general(a, b, (((1,), (1,)), ((), ())), preferred_element_type=F32)


def _rms(x, g):
    return x * lax.rsqrt(jnp.mean(x * x, axis=-1, keepdims=True) + EPS) * g


def _const_spec(shape):
    nd = len(shape)
    return pl.BlockSpec(shape, lambda *_: (0,) * nd, pipeline_mode=pl.Buffered(1))


def _mod_spec(component, row_of):
    return pl.BlockSpec((None, None, 1, D_MODEL), lambda *idx: (row_of(*idx), component, 0, 0))


def _params(sem):
    return pltpu.CompilerParams(dimension_semantics=sem, vmem_limit_bytes=VMEM_LIMIT)


def _mod_body(c_ref, w_ref, b_ref, o_ref):
    c = c_ref[...]
    s = c * jax.nn.sigmoid(c)
    o_ref[...] = _dot(s.astype(BF16), w_ref[...].astype(BF16)) + b_ref[...]


def _mod_call(cc, w_mod, b_mod):
    n = w_mod.shape[-1]
    return pl.pallas_call(
        _mod_body,
        grid=(n // MOD_TN,),
        in_specs=[
            pl.BlockSpec((MOD_ROWS, D_MODEL), lambda j: (0, 0)),
            pl.BlockSpec((None, D_MODEL, MOD_TN), lambda j: (0, 0, j)),
            pl.BlockSpec((1, MOD_TN), lambda j: (0, j)),
        ],
        out_specs=pl.BlockSpec((MOD_ROWS, MOD_TN), lambda j: (0, j)),
        out_shape=jax.ShapeDtypeStruct((MOD_ROWS, n), F32),
        compiler_params=_params(("arbitrary",)),
        name="mod",
    )(cc, w_mod, b_mod)


def _rope128(t, c, s1, s2):
    return (t * c + pltpu.roll(t, HEAD_PAD - ROPE_HALF, 1) * s1
            + pltpu.roll(t, ROPE_HALF, 1) * s2)


def _proj_body(latent, xp_ref, x_ref, xn_ref, sh_ref, sc_ref, g1_ref, wa_ref, wb_ref, gkv_ref, wuk_ref,
               wuv_ref, cw_ref, cb_ref, c_ref, s1_ref, s2_ref, *rest):
    if latent:
        gq_ref, wuq_ref, k_ref, v_ref, xc_ref, q_ref, yb_ref, gl_ref = rest
    else:
        k_ref, v_ref, xc_ref = rest
    i = pl.program_id(0)
    tm = x_ref.shape[1]
    xe = jnp.concatenate([xp_ref[0], x_ref[0], xn_ref[0]], axis=0)
    hf = _rms(xe, g1_ref[...]) * (1.0 + sc_ref[...]) + sh_ref[...]
    keep_top = (i > 0).astype(F32)
    keep_bot = (i < pl.num_programs(0) - 1).astype(F32)
    h = hf[HALO:HALO + tm]
    he = jnp.concatenate([hf[:HALO] * keep_top, h, hf[HALO + tm:] * keep_bot], axis=0).astype(BF16)
    hb = h.astype(BF16)
    c, s1, s2 = c_ref[...], s1_ref[...], s2_ref[...]

    if latent:
        lat = _dot(hb, wa_ref[...])
        q_lat = lat[:, :IN_KV - IN_Q]
        lat = lat[:, IN_KV - IN_Q:]
    else:
        lat = _dot(hb, wa_ref[:, IN_KV:IN_A_END])
    kv_lat, kr_raw = lat[:, :IN_KR - IN_KV], lat[:, IN_KR - IN_KV:]
    xb = _dot(he, wb_ref[:, IN_XB:IN_YB])
    ext = tm + 2 * HALO
    cw = cw_ref[...]
    xc_ref[0] = (cb_ref[...] + pltpu.roll(xb, 2, 0)[HALO:HALO + tm] * cw[0:1]
                 + pltpu.roll(xb, 1, 0)[HALO:HALO + tm] * cw[1:2]
                 + xb[HALO:HALO + tm] * cw[2:3]
                 + pltpu.roll(xb, ext - 1, 0)[HALO:HALO + tm] * cw[3:4])
    kvn = _rms(kv_lat, gkv_ref[...]).astype(BF16)
    kk = _dot(kvn, wuk_ref[...])
    vt = _dot_nt(wuv_ref[...], kvn)
    ones = jnp.ones((VT_ROWS - V_HEAD_DIM, vt.shape[1]), F32)
    v_ref[0] = jnp.concatenate([piece for hd in range(N_HEADS)
                                for piece in (vt[hd * V_HEAD_DIM:(hd + 1) * V_HEAD_DIM], ones)],
                               axis=0).astype(BF16)
    if latent:
        qn = _rms(q_lat, gq_ref[...]).astype(BF16)
        qq = _dot(qn, wuq_ref[...])
        yb_ref[0] = _gelu_tanh(_dot(hb, wb_ref[:, IN_YB:IN_GL]))
        gl_ref[0] = _dot(hb, wb_ref[:, IN_GL:IN_END])
    kr = _rope128(kr_raw, c, s1, s2)
    for hd in range(N_HEADS):
        sl = slice(hd * HEAD_PAD, (hd + 1) * HEAD_PAD)
        k_ref[0, :, sl] = (kk[:, sl] + kr).astype(BF16)
    if latent:
        scale = QK_DIM ** -0.5 * LOG2_E
        for hd in range(N_HEADS):
            sl = slice(hd * HEAD_PAD, (hd + 1) * HEAD_PAD)
            q_ref[0, :, sl] = (_rope128(qq[:, sl], c, s1, s2) * scale).astype(BF16)


def _proj_call(latent, x, mod, mod_row, g1, w, tabs):
    b, s, _ = x.shape
    tm = min(PROJ_TM, s)
    row = lambda width: pl.BlockSpec((1, tm, width), lambda i, j: (j, i, 0))
    row_of = (lambda i, j: j) if mod_row is None else (lambda i, j: mod_row)
    tab = pl.BlockSpec((tm, HEAD_PAD), lambda i, j: (i, 0))
    per = tm // HALO
    nblk = s // HALO
    prev = pl.BlockSpec((1, HALO, D_MODEL), lambda i, j: (j, jnp.maximum(i * per - 1, 0), 0))
    nxt = pl.BlockSpec((1, HALO, D_MODEL), lambda i, j: (j, jnp.minimum((i + 1) * per, nblk - 1), 0))
    ins = [x, x, x, mod, mod, g1, w['in_a'], w['in_b'], w['gkv'], w['uk'], w['uv'], w['conv_w'], w['conv_b'], *tabs]
    in_specs = ([prev, row(D_MODEL), nxt, _mod_spec(MOD_SHIFT1, row_of), _mod_spec(MOD_SCALE1, row_of)]
                + [_const_spec(a.shape) for a in ins[5:13]]
                + [tab] * 3)
    widths = [N_HEADS * HEAD_PAD, None, LRU_WIDTH]
    dtypes = [BF16, BF16, F32]
    if latent:
        extra = [w['gq'], w['uq']]
        ins += extra
        in_specs += [_const_spec(a.shape) for a in extra]
        widths += [N_HEADS * HEAD_PAD, LRU_WIDTH, 2 * D_MODEL]
        dtypes += [BF16, F32, F32]
    hv = N_HEADS * VT_ROWS
    out_specs = [row(wd) for wd in widths if wd is not None]
    out_shape = [jax.ShapeDtypeStruct((b, s, wd), dt) for wd, dt in zip(widths, dtypes) if wd is not None]
    out_specs.insert(1, pl.BlockSpec((1, hv, tm), lambda i, j: (j, 0, i)))
    out_shape.insert(1, jax.ShapeDtypeStruct((b, hv, s), BF16))
    return pl.pallas_call(
        functools.partial(_proj_body, latent),
        grid=(s // tm, b),
        in_specs=in_specs,
        out_specs=out_specs,
        out_shape=out_shape,
        compiler_params=_params(("parallel", "parallel")),
        name="proj_latent" if latent else "proj_ctx",
    )(*ins)


def _gelu_tanh(x):
    return 0.5 * x * (1.0 + jnp.tanh(0.7978845608028654 * (x + 0.044715 * (x * x * x))))


def _lru_body(latent, xc_ref, wg_ref, lam_ref, *rest):
    if latent:
        yb_ref, h0_ref, out_ref, gates, scans = rest
    else:
        st_ref, gates, scans = rest
    s = xc_ref.shape[1]
    af, uf, ab, ub = (gates.at[pl.ds(k * s, s)] for k in range(4))
    hfl, pfl, hbl, pbl = (scans.at[pl.ds(k * s, s)] for k in range(4))
    seg = s // SUBLANES
    tc = min(LRU_TC, s)
    nc = s // tc
    piece = min(tc, seg)

    wg = wg_ref[0]
    lam = lam_ref[0]
    hcsp = (0.5 * LRU_C) * (jnp.maximum(-lam, 0.0) + jnp.log1p(jnp.exp(-jnp.abs(lam))))
    bias_lhs = jnp.where(lax.broadcasted_iota(jnp.int32, (tc, LANES), 1) < GATE_BIAS_ROWS, 1.0, 0.0).astype(BF16)

    def interleaved(ci):
        pieces = []
        for k in range(tc // piece):
            t0 = ci * tc + k * piece
            j = t0 // seg
            pieces.append((slice(k * piece, (k + 1) * piece),
                           pl.ds((t0 - j * seg) * SUBLANES + j, piece, stride=SUBLANES)))
        return pieces

    def gate_chunk(ci, carry):
        r0 = pl.multiple_of(ci * tc, tc)
        xc = xc_ref[0, pl.ds(r0, tc), :]
        t = jnp.tanh(_dot(jnp.concatenate([xc.astype(BF16), bias_lhs], axis=1), wg))
        hx = 0.5 * xc
        for d, (a_ref, u_ref) in enumerate(((af, uf), (ab, ub))):
            t_r = t[:, (2 * d) * LANES:(2 * d + 1) * LANES]
            t_i = t[:, (2 * d + 1) * LANES:(2 * d + 2) * LANES]
            neg_log_a = hcsp[d:d + 1] + hcsp[d:d + 1] * t_r
            a = jnp.exp2(neg_log_a * (-LOG2_E))
            y = (1.0 - a) * (1.0 + a)
            mult = y * lax.rsqrt(jnp.maximum(y, TINY))
            u = (mult * hx) * (1.0 + t_i)
            for rows, rd in interleaved(ci):
                a_ref[rd, :] = a[rows]
                u_ref[rd, :] = u[rows]
        return carry

    lax.fori_loop(0, nc, gate_chunk, 0, unroll=min(8, nc))

    def scan_step(i, carry):
        hf, pf, hb, pb = carry
        fwd = pl.ds(pl.multiple_of(i * SUBLANES, SUBLANES), SUBLANES)
        bwd = pl.ds(pl.multiple_of((seg - 1 - i) * SUBLANES, SUBLANES), SUBLANES)
        a = af[fwd, :]
        hf = a * hf + uf[fwd, :]
        pf = a * pf
        hfl[fwd, :] = hf
        pfl[fwd, :] = pf
        a = ab[bwd, :]
        hb = a * hb + ub[bwd, :]
        pb = a * pb
        hbl[bwd, :] = hb
        pbl[bwd, :] = pb
        return hf, pf, hb, pb

    zeros = jnp.zeros((SUBLANES, LANES), F32)
    ones = jnp.ones((SUBLANES, LANES), F32)
    hf, pf, hb, pb = lax.fori_loop(0, seg, scan_step, (zeros, ones, zeros, ones), unroll=LRU_UNROLL)

    if latent:
        h0 = h0_ref[0]
        cf, cbk = h0[0:1], h0[1:2]
    else:
        cf = cbk = jnp.zeros((1, LANES), F32)
    cfs = []
    for j in range(SUBLANES):
        cfs.append(cf)
        cf = hf[j:j + 1] + pf[j:j + 1] * cf
    cbs = [None] * SUBLANES
    for j in reversed(range(SUBLANES)):
        cbs[j] = cbk
        cbk = hb[j:j + 1] + pb[j:j + 1] * cbk

    if latent:
        fix_rows = min(LRU_FIX_ROWS, s)
        cf_all = jnp.tile(jnp.concatenate(cfs, axis=0), (fix_rows // SUBLANES, 1))
        cb_all = jnp.tile(jnp.concatenate(cbs, axis=0), (fix_rows // SUBLANES, 1))

        def fix_chunk(ci, carry):
            rows = pl.ds(pl.multiple_of(ci * fix_rows, fix_rows), fix_rows)
            af[rows, :] = (hfl[rows, :] + pfl[rows, :] * cf_all) + (hbl[rows, :] + pbl[rows, :] * cb_all)
            return carry

        lax.fori_loop(0, s // fix_rows, fix_chunk, 0, unroll=2)

        def out_chunk(ci, carry):
            r0 = pl.multiple_of(ci * tc, tc)
            hsum = jnp.concatenate([af[rd, :] for _, rd in interleaved(ci)], axis=0)
            out_ref[0, pl.ds(r0, tc), :] = (hsum * yb_ref[0, pl.ds(r0, tc), :]).astype(BF16)
            return carry

        lax.fori_loop(0, nc, out_chunk, 0)
    else:
        st_ref[0] = jnp.concatenate([cf, cbk], axis=0)


def _lru_call(latent, xc, w, yb=None, h0=None):
    b, s, _ = xc.shape
    seq = pl.BlockSpec((1, s, LANES), lambda i, j: (i, 0, j))
    st = pl.BlockSpec((1, 2, LANES), lambda i, j: (i, 0, j))
    in_specs = [
        seq,
        pl.BlockSpec((1, 2 * LANES, 4 * LANES), lambda i, j: (j, 0, 0)),
        pl.BlockSpec((1, 2, LANES), lambda i, j: (j, 0, 0)),
    ]
    ins = [xc, w['gate_w'], w['lam']]
    scratch = [pltpu.VMEM((4 * s, LANES), F32)] * 2
    if latent:
        ins += [yb, h0]
        in_specs += [seq, st]
        out_specs = seq
        out_shape = jax.ShapeDtypeStruct((b, s, LRU_WIDTH), BF16)
    else:
        out_specs = st
        out_shape = jax.ShapeDtypeStruct((b, 2, LRU_WIDTH), F32)
    return pl.pallas_call(
        functools.partial(_lru_body, latent),
        grid=(b, LRU_BLOCKS),
        in_specs=in_specs,
        out_specs=out_specs,
        out_shape=out_shape,
        scratch_shapes=scratch,
        compiler_params=_params(("parallel", "parallel")),
        name="lru_latent" if latent else "lru_ctx",
    )(*ins)


def _col_groups(x, op):
    rows, cols = x.shape
    return op(x.reshape(rows // SUBLANES, SUBLANES, cols), axis=0)


def _attn_body(q_ref, kl_ref, kc_ref, vl_ref, vc_ref, o_ref, s0_ref, s1_ref, p0_ref, p1_ref, o0_ref):
    s_len = kl_ref.shape[1]
    c_len = kc_ref.shape[1]
    tk = min(ATTN_TK, s_len)
    sub = s0_ref.shape[1]
    n_sub = q_ref.shape[1] // sub
    chunks = [(kc_ref, 0, c_len, 0)] + [(kl_ref, ki * tk, tk, c_len + ki * tk) for ki in range(s_len // tk)]
    s_refs, p_refs = (s0_ref, s1_ref), (p0_ref, p1_ref)

    def qrows(qt):
        return pl.ds(pl.multiple_of(qt * sub, sub), sub)

    def score_pass(qt, hd):
        lanes = slice(hd * HEAD_PAD, (hd + 1) * HEAD_PAD)
        q = q_ref[0, qrows(qt), lanes]
        m8 = None
        for k_ref, r0, rows, off in chunks:
            sk = _dot_nt(k_ref[0, r0:r0 + rows, lanes], q)
            s_refs[hd][off:off + rows, :] = sk
            g = _col_groups(sk, jnp.max)
            m8 = g if m8 is None else jnp.maximum(m8, g)
        return jnp.max(m8, axis=0, keepdims=True)

    def exp_pass(hd, m):
        for _, _, rows, off in chunks:
            p_refs[hd][off:off + rows, :] = jnp.exp2(s_refs[hd][off:off + rows, :] - m).astype(BF16)

    def value_pass(hd):
        vrows = slice(hd * VT_ROWS, (hd + 1) * VT_ROWS)
        acc = (_dot(vc_ref[0, vrows, :], p_refs[hd][0:c_len, :])
               + _dot(vl_ref[0, vrows, :], p_refs[hd][c_len:c_len + s_len, :]))
        return acc[:V_HEAD_DIM] / acc[V_HEAD_DIM:V_HEAD_DIM + 1]

    def store(qt, o0, o1):
        o_ref[0, qrows(qt), :] = jnp.concatenate([o0, o1], axis=0).T.astype(BF16)

    def finish(hd, m):
        exp_pass(hd, m)
        return value_pass(hd)

    def head1_trip(qt, m_prev):
        o0_ref[...] = finish(0, m_prev)
        return score_pass(qt, 1)

    def head0_trip(qt, m_prev):
        store(qt - 1, o0_ref[...], finish(1, m_prev))
        return score_pass(qt, 0)

    def trip(u, m_prev):
        return lax.cond(u % 2 == 1, head1_trip, head0_trip, u // 2, m_prev)

    m = lax.fori_loop(1, 2 * n_sub, trip, score_pass(0, 0))
    store(n_sub - 1, o0_ref[...], finish(1, m))


def _attn_call(q, k_l, k_c, vt_l, vt_c):
    b, s, _ = q.shape
    c_len = k_c.shape[1]
    tq = min(ATTN_TQ, s)
    sub = min(ATTN_SUB, tq)
    pairs = N_HEADS // 2
    return pl.pallas_call(
        _attn_body,
        grid=(b, pairs, s // tq),
        in_specs=[
            pl.BlockSpec((1, tq, 2 * HEAD_PAD), lambda i, j, t: (i, t, j)),
            pl.BlockSpec((1, s, 2 * HEAD_PAD), lambda i, j, t: (i, 0, j)),
            pl.BlockSpec((1, c_len, 2 * HEAD_PAD), lambda i, j, t: (i, 0, j)),
            pl.BlockSpec((1, 2 * VT_ROWS, s), lambda i, j, t: (i, j, 0)),
            pl.BlockSpec((1, 2 * VT_ROWS, c_len), lambda i, j, t: (i, j, 0)),
        ],
        out_specs=pl.BlockSpec((1, tq, 2 * V_HEAD_DIM), lambda i, j, t: (i, t, j)),
        out_shape=jax.ShapeDtypeStruct((b, s, N_HEADS * V_HEAD_DIM), BF16),
        scratch_shapes=([pltpu.VMEM((c_len + s, sub), F32)] * 2 + [pltpu.VMEM((c_len + s, sub), BF16)] * 2
                        + [pltpu.VMEM((V_HEAD_DIM, sub), F32)]),
        compiler_params=_params(("parallel", "parallel", "parallel")),
        name="attn",
    )(q, k_l, k_c, vt_l, vt_c)


def _merge_body(attn_ref, m_ref, gl_ref, x_ref, g1_ref, bg_ref, woa_ref, wol_ref, wout_ref, o_ref):
    tm = x_ref.shape[1]
    sub = tm // MERGE_SUBTILES
    rows = [slice(i * sub, (i + 1) * sub) for i in range(MERGE_SUBTILES)]
    ys = [(_dot(attn_ref[0, r, :], woa_ref[...]), _dot(m_ref[0, r, :], wol_ref[...])) for r in rows]
    for r, (y_a, y_b) in zip(rows, ys):
        gates = 0.5 + 0.5 * jnp.tanh(0.5 * (gl_ref[0, r, :] + bg_ref[...]))
        mix = gates[:, :D_MODEL] * y_a + gates[:, D_MODEL:] * y_b
        o_ref[0, r, :] = x_ref[0, r, :] + g1_ref[...] * _dot(mix.astype(BF16), wout_ref[...])


def _merge_call(attn, m, gl, x, mod, w):
    b, s, _ = x.shape
    tm = min(MERGE_TM, s)
    row = lambda width: pl.BlockSpec((1, tm, width), lambda i, j: (i, j, 0))
    consts = [w['b_gate'], w['o_attn'], w['o_lru'], w['out']]
    return pl.pallas_call(
        _merge_body,
        grid=(b, s // tm),
        in_specs=[row(N_HEADS * V_HEAD_DIM), row(LRU_WIDTH), row(2 * D_MODEL), row(D_MODEL),
                  _mod_spec(MOD_GATE1, lambda i, j: i)]
        + [_const_spec(a.shape) for a in consts],
        out_specs=row(D_MODEL),
        out_shape=jax.ShapeDtypeStruct((b, s, D_MODEL), F32),
        compiler_params=_params(("parallel", "parallel")),
        name="merge",
    )(attn, m, gl, x, mod, *consts)


def _ffn_body(xp_ref, x_ref, xn_ref, sh_ref, sc_ref, g2_ref, n2_ref, fg_ref, wup_ref, cw_ref, cb_ref,
              wdn_ref, o_ref, f_ref):
    j = pl.program_id(1)
    nj = pl.num_programs(1)
    tm = x_ref.shape[1]
    x = x_ref[0]
    xe = jnp.concatenate([xp_ref[0], x, xn_ref[0]], axis=0)
    hf = _rms(xe, n2_ref[...]) * (1.0 + sc_ref[...]) + sh_ref[...]
    keep_top = (j > 0).astype(F32)
    keep_bot = (j < nj - 1).astype(F32)
    ht = hf[HALO:HALO + tm]
    he = jnp.concatenate([hf[:HALO] * keep_top, ht, hf[HALO + tm:] * keep_bot], axis=0).astype(BF16)
    ht = ht.astype(BF16)
    ext = tm + 2 * HALO
    for ci in range(FFN_DIM // FFN_FC):
        cols = slice(ci * FFN_FC, (ci + 1) * FFN_FC)
        gcols = slice(FFN_DIM + ci * FFN_FC, FFN_DIM + (ci + 1) * FFN_FC)
        a = _dot(he, wup_ref[:, cols])
        cw = cw_ref[:, cols]
        conv = (cb_ref[:, cols] + pltpu.roll(a, 1, 0)[HALO:HALO + tm] * cw[0:1]
                + a[HALO:HALO + tm] * cw[1:2]
                + pltpu.roll(a, ext - 1, 0)[HALO:HALO + tm] * cw[2:3])
        g = _dot(ht, wup_ref[:, gcols])
        f_ref[:, cols] = (conv * jax.nn.sigmoid(conv) * g).astype(BF16)
    y = x + g2_ref[...] * _dot(f_ref[...], wdn_ref[...])
    o_ref[0] = _rms(y, fg_ref[...])


def _ffn_call(x1, mod, w):
    b, s, _ = x1.shape
    tm = min(FFN_TM, s)
    per = tm // HALO
    nblk = s // HALO
    row = pl.BlockSpec((1, tm, D_MODEL), lambda i, j: (i, j, 0))
    prev = pl.BlockSpec((1, HALO, D_MODEL), lambda i, j: (i, jnp.maximum(j * per - 1, 0), 0))
    nxt = pl.BlockSpec((1, HALO, D_MODEL), lambda i, j: (i, jnp.minimum((j + 1) * per, nblk - 1), 0))
    batch_row = lambda i, j: i
    consts = [w['norm2_g'], w['final_g'], w['up'], w['ffn_conv_w'], w['ffn_conv_b'], w['down']]
    return pl.pallas_call(
        _ffn_body,
        grid=(b, s // tm),
        in_specs=[prev, row, nxt, _mod_spec(MOD_SHIFT2, batch_row), _mod_spec(MOD_SCALE2, batch_row),
                  _mod_spec(MOD_GATE2, batch_row)] + [_const_spec(a.shape) for a in consts],
        out_specs=row,
        out_shape=jax.ShapeDtypeStruct((b, s, D_MODEL), F32),
        scratch_shapes=[pltpu.VMEM((tm, FFN_DIM), BF16)],
        compiler_params=_params(("parallel", "parallel")),
        name="ffn",
    )(x1, x1, x1, mod, mod, mod, *consts)


def _rope_tables(n):
    rows = n // GRID_W
    row_ids = jnp.repeat(jnp.arange(rows), GRID_W).astype(F32)
    col_ids = jnp.tile(jnp.arange(GRID_W), rows).astype(F32)
    axis_dim = QK_ROPE_DIM // 2
    inv = 1.0 / (ROPE_BASE ** (jnp.arange(0, axis_dim, 2, dtype=F32) / axis_dim))
    ang = jnp.concatenate([row_ids[:, None] * inv, col_ids[:, None] * inv], axis=-1)
    cos, sin = jnp.cos(ang), jnp.sin(ang)
    ones = lambda w_: jnp.ones((n, w_), F32)
    zeros = lambda w_: jnp.zeros((n, w_), F32)
    tail = HEAD_PAD - QK_DIM
    c = jnp.concatenate([ones(QK_NOPE_DIM), cos, cos, ones(tail)], axis=-1)
    s1 = jnp.concatenate([zeros(QK_NOPE_DIM), -sin, zeros(ROPE_HALF + tail)], axis=-1)
    s2 = jnp.concatenate([zeros(QK_NOPE_DIM + ROPE_HALF), sin, zeros(tail)], axis=-1)
    return c, s1, s2


def _identity_tables(n):
    return (jnp.ones((n, HEAD_PAD), F32), jnp.zeros((n, HEAD_PAD), F32), jnp.zeros((n, HEAD_PAD), F32))


def _prep_weights(w_in, q_norm_g, kv_norm_g, w_uq, w_ukv, w_o_attn, lru_conv_w, lru_conv_b, lru_w_a,
                  lru_b_a, lru_w_x, lru_b_x, lru_lambda, w_o_lru, w_out, b_gate, norm2_g, w_up,
                  ffn_conv_w, ffn_conv_b, w_down, final_g):
    w = {}
    kr = jnp.pad(w_in[:, OFF_KR:OFF_XB], ((0, 0), (QK_NOPE_DIM, HEAD_PAD - QK_DIM)))
    w['in_a'] = jnp.concatenate([w_in[:, :OFF_KR], kr], axis=1).astype(BF16)
    w['in_b'] = w_in[:, OFF_XB:].astype(BF16)
    w['gq'] = q_norm_g[None, :]
    w['gkv'] = kv_norm_g[None, :]
    uq = w_uq.reshape(Q_LORA_RANK, N_HEADS, QK_DIM)
    w['uq'] = jnp.pad(uq, ((0, 0), (0, 0), (0, HEAD_PAD - QK_DIM))).reshape(Q_LORA_RANK, -1).astype(BF16)
    ukv = w_ukv.reshape(KV_LORA_RANK, N_HEADS, QK_NOPE_DIM + V_HEAD_DIM)
    w['uk'] = jnp.pad(ukv[..., :QK_NOPE_DIM],
                      ((0, 0), (0, 0), (0, HEAD_PAD - QK_NOPE_DIM))).reshape(KV_LORA_RANK, -1).astype(BF16)
    w['uv'] = ukv[..., QK_NOPE_DIM:].reshape(KV_LORA_RANK, -1).T.astype(BF16)
    w['conv_w'] = lru_conv_w
    w['conv_b'] = lru_conv_b[None, :]
    gw = 0.5 * jnp.concatenate([lru_w_a[0], lru_w_x[0], lru_w_a[1], lru_w_x[1]], axis=-1)
    gb = 0.5 * jnp.stack([lru_b_a[0], lru_b_x[0], lru_b_a[1], lru_b_x[1]], axis=0)
    gb = gb.reshape(4, LRU_BLOCKS, LRU_BLOCK_W).transpose(1, 0, 2).reshape(LRU_BLOCKS, 1, -1)
    terms = []
    for _ in range(GATE_BIAS_ROWS):
        t = gb * VELTKAMP_8BIT
        hi = t - (t - gb)
        terms.append(hi)
        gb = gb - hi
    bias_rows = jnp.pad(jnp.concatenate(terms, axis=1), ((0, 0), (0, LANES - GATE_BIAS_ROWS), (0, 0)))
    w['gate_w'] = jnp.concatenate([gw, bias_rows], axis=1).astype(BF16)
    w['lam'] = lru_lambda.reshape(2, LRU_BLOCKS, LRU_BLOCK_W).transpose(1, 0, 2)
    w['o_attn'] = w_o_attn.astype(BF16)
    w['o_lru'] = w_o_lru.astype(BF16)
    w['out'] = w_out.astype(BF16)
    w['b_gate'] = b_gate[None, :]
    w['norm2_g'] = norm2_g[None, :]
    w['final_g'] = final_g[None, :]
    w['up'] = w_up.astype(BF16)
    w['ffn_conv_w'] = ffn_conv_w
    w['ffn_conv_b'] = ffn_conv_b[None, :]
    w['down'] = w_down.astype(BF16)
    return w


def kernel(x, c, ctx, c_ctx, w_mod, b_mod, norm1_g, w_in, b_gate, q_norm_g, kv_norm_g, w_uq, w_ukv,
           w_o_attn, lru_conv_w, lru_conv_b, lru_w_a, lru_b_a, lru_w_x, lru_b_x, lru_lambda, w_o_lru,
           w_out, norm2_g, w_up, ffn_conv_w, ffn_conv_b, w_down, final_g):
    assert w_mod.shape[0] == 1, "single-layer block"
    b, s, _ = x.shape
    c_len = ctx.shape[1]
    assert b + 1 <= MOD_ROWS

    w = _prep_weights(w_in[0], q_norm_g[0], kv_norm_g[0], w_uq[0], w_ukv[0], w_o_attn[0], lru_conv_w[0],
                      lru_conv_b[0], lru_w_a[0], lru_b_a[0], lru_w_x[0], lru_b_x[0], lru_lambda[0],
                      w_o_lru[0], w_out[0], b_gate[0], norm2_g[0], w_up[0], ffn_conv_w[0],
                      ffn_conv_b[0], w_down[0], final_g)
    g1n = norm1_g[0][None, :]

    cc = jnp.concatenate([c, c_ctx[None, :], jnp.zeros((MOD_ROWS - b - 1, D_MODEL), F32)], axis=0)
    mod = _mod_call(cc, w_mod, b_mod).reshape(MOD_ROWS, 6, 1, D_MODEL)

    k_c, v_c, xb_c = _proj_call(False, ctx, mod, b, g1n, w, _identity_tables(c_len))
    states = _lru_call(False, xb_c, w)

    k_l, v_l, xb_l, q_l, yb_l, gl_l = _proj_call(True, x, mod, None, g1n, w, _rope_tables(s))
    m = _lru_call(True, xb_l, w, yb=yb_l, h0=states)
    attn = _attn_call(q_l, k_l, k_c, v_l, v_c)
    x1 = _merge_call(attn, m, gl_l, x, mod, w)
    return _ffn_call(x1, mod, w)
```

```python
import functools

import jax
import jax.numpy as jnp
from jax import lax
from jax.experimental import pallas as pl
from jax.experimental.pallas import tpu as pltpu

F32 = jnp.float32
BF16 = jnp.bfloat16

D_MODEL = 1024
GRID_W = 64
N_HEADS = 8
QK_NOPE_DIM = 64
QK_ROPE_DIM = 32
ROPE_HALF = QK_ROPE_DIM // 2
V_HEAD_DIM = 64
QK_DIM = QK_NOPE_DIM + QK_ROPE_DIM
HEAD_PAD = 128
Q_LORA_RANK = 384
KV_LORA_RANK = 256
ROPE_BASE = 10000.0
LRU_WIDTH = 1280
LRU_BLOCKS = 10
LRU_BLOCK_W = LRU_WIDTH // LRU_BLOCKS
LRU_CONV_W = 4
LRU_C = 8.0
FFN_DIM = 2816
FFN_CONV_W = 3
EPS = 1e-6
TINY = 1e-30
OFF_KV = Q_LORA_RANK
OFF_KR = OFF_KV + KV_LORA_RANK
OFF_XB = OFF_KR + QK_ROPE_DIM
OFF_YB = OFF_XB + LRU_WIDTH
OFF_G = OFF_YB + LRU_WIDTH
IN_Q = 0
IN_KV = IN_Q + Q_LORA_RANK
IN_KR = IN_KV + KV_LORA_RANK
IN_A_END = IN_KR + HEAD_PAD
IN_XB = 0
IN_YB = IN_XB + LRU_WIDTH
IN_GL = IN_YB + LRU_WIDTH
IN_END = IN_GL + 2 * D_MODEL

SUBLANES = 8
VT_ROWS = 80
LANES = 128
VMEM_LIMIT = 56 * 1024 * 1024

MOD_ROWS = 8
MOD_SHIFT1, MOD_SCALE1, MOD_GATE1, MOD_SHIFT2, MOD_SCALE2, MOD_GATE2 = range(6)
MOD_TN = 768
PROJ_TM = 512
LRU_TC = 256
LRU_UNROLL = 16
LRU_FIX_ROWS = 64
GATE_BIAS_ROWS = 3
VELTKAMP_8BIT = 65537.0
LOG2_E = 1.4426950408889634
ATTN_TQ = 4096
ATTN_SUB = 256
ATTN_TK = 256
MERGE_TM = 512
MERGE_SUBTILES = 2
FFN_TM = 512
FFN_FC = 256
HALO = SUBLANES


def _dot(a, b):
    return jnp.dot(a, b, preferred_element_type=F32)


def _dot_nt(a, b):
    return lax.dot_general(a, b, (((1,), (1,)), ((), ())), preferred_element_type=F32)


def _rms(x, g):
    return x * lax.rsqrt(jnp.mean(x * x, axis=-1, keepdims=True) + EPS) * g


def _const_spec(shape):
    nd = len(shape)
    return pl.BlockSpec(shape, lambda *_: (0,) * nd, pipeline_mode=pl.Buffered(1))


def _mod_spec(component, row_of):
    return pl.BlockSpec((None, None, 1, D_MODEL), lambda *idx: (row_of(*idx), component, 0, 0))


def _params(sem):
    return pltpu.CompilerParams(dimension_semantics=sem, vmem_limit_bytes=VMEM_LIMIT)


def _mod_body(c_ref, w_ref, b_ref, o_ref):
    c = c_ref[...]
    s = c * jax.nn.sigmoid(c)
    o_ref[...] = _dot(s.astype(BF16), w_ref[...].astype(BF16)) + b_ref[...]


def _mod_call(cc, w_mod, b_mod):
    n = w_mod.shape[-1]
    return pl.pallas_call(
        _mod_body,
        grid=(n // MOD_TN,),
        in_specs=[
            pl.BlockSpec((MOD_ROWS, D_MODEL), lambda j: (0, 0)),
            pl.BlockSpec((None, D_MODEL, MOD_TN), lambda j: (0, 0, j)),
            pl.BlockSpec((1, MOD_TN), lambda j: (0, j)),
        ],
        out_specs=pl.BlockSpec((MOD_ROWS, MOD_TN), lambda j: (0, j)),
        out_shape=jax.ShapeDtypeStruct((MOD_ROWS, n), F32),
        compiler_params=_params(("arbitrary",)),
        name="mod",
    )(cc, w_mod, b_mod)


def _rope128(t, c, s1, s2):
    return (t * c + pltpu.roll(t, HEAD_PAD - ROPE_HALF, 1) * s1
            + pltpu.roll(t, ROPE_HALF, 1) * s2)


def _proj_body(latent, xp_ref, x_ref, xn_ref, sh_ref, sc_ref, g1_ref, wa_ref, wb_ref, gkv_ref, wuk_ref,
               wuv_ref, cw_ref, cb_ref, c_ref, s1_ref, s2_ref, *rest):
    if latent:
        gq_ref, wuq_ref, k_ref, v_ref, xc_ref, q_ref, yb_ref = rest
    else:
        k_ref, v_ref, xc_ref = rest
    i = pl.program_id(0)
    tm = x_ref.shape[1]
    xe = jnp.concatenate([xp_ref[0], x_ref[0], xn_ref[0]], axis=0)
    hf = _rms(xe, g1_ref[...]) * (1.0 + sc_ref[...]) + sh_ref[...]
    keep_top = (i > 0).astype(F32)
    keep_bot = (i < pl.num_programs(0) - 1).astype(F32)
    h = hf[HALO:HALO + tm]
    he = jnp.concatenate([hf[:HALO] * keep_top, h, hf[HALO + tm:] * keep_bot], axis=0).astype(BF16)
    hb = h.astype(BF16)
    c, s1, s2 = c_ref[...], s1_ref[...], s2_ref[...]

    if latent:
        lat = _dot(hb, wa_ref[...])
        q_lat = lat[:, :IN_KV - IN_Q]
        lat = lat[:, IN_KV - IN_Q:]
    else:
        lat = _dot(hb, wa_ref[:, IN_KV:IN_A_END])
    kv_lat, kr_raw = lat[:, :IN_KR - IN_KV], lat[:, IN_KR - IN_KV:]
    xb = _dot(he, wb_ref[:, IN_XB:IN_YB])
    ext = tm + 2 * HALO
    cw = cw_ref[...]
    xc_ref[0] = (cb_ref[...] + pltpu.roll(xb, 2, 0)[HALO:HALO + tm] * cw[0:1]
                 + pltpu.roll(xb, 1, 0)[HALO:HALO + tm] * cw[1:2]
                 + xb[HALO:HALO + tm] * cw[2:3]
                 + pltpu.roll(xb, ext - 1, 0)[HALO:HALO + tm] * cw[3:4])
    kvn = _rms(kv_lat, gkv_ref[...]).astype(BF16)
    kk = _dot(kvn, wuk_ref[...])
    vt = _dot_nt(wuv_ref[...], kvn)
    ones = jnp.ones((VT_ROWS - V_HEAD_DIM, vt.shape[1]), F32)
    v_ref[0] = jnp.concatenate([piece for hd in range(N_HEADS)
                                for piece in (vt[hd * V_HEAD_DIM:(hd + 1) * V_HEAD_DIM], ones)],
                               axis=0).astype(BF16)
    if latent:
        qn = _rms(q_lat, gq_ref[...]).astype(BF16)
        qq = _dot(qn, wuq_ref[...])
        yb_ref[0] = _gelu_tanh(_dot(hb, wb_ref[:, IN_YB:IN_GL]))
    kr = _rope128(kr_raw, c, s1, s2)
    for hd in range(N_HEADS):
        sl = slice(hd * HEAD_PAD, (hd + 1) * HEAD_PAD)
        k_ref[0, :, sl] = (kk[:, sl] + kr).astype(BF16)
    if latent:
        scale = QK_DIM ** -0.5 * LOG2_E
        for hd in range(N_HEADS):
            sl = slice(hd * HEAD_PAD, (hd + 1) * HEAD_PAD)
            q_ref[0, :, sl] = (_rope128(qq[:, sl], c, s1, s2) * scale).astype(BF16)


def _proj_call(latent, x, mod, mod_row, g1, w, tabs):
    b, s, _ = x.shape
    tm = min(PROJ_TM, s)
    row = lambda width: pl.BlockSpec((1, tm, width), lambda i, j: (j, i, 0))
    row_of = (lambda i, j: j) if mod_row is None else (lambda i, j: mod_row)
    tab = pl.BlockSpec((tm, HEAD_PAD), lambda i, j: (i, 0))
    per = tm // HALO
    nblk = s // HALO
    prev = pl.BlockSpec((1, HALO, D_MODEL), lambda i, j: (j, jnp.maximum(i * per - 1, 0), 0))
    nxt = pl.BlockSpec((1, HALO, D_MODEL), lambda i, j: (j, jnp.minimum((i + 1) * per, nblk - 1), 0))
    ins = [x, x, x, mod, mod, g1, w['in_a'], w['in_b'], w['gkv'], w['uk'], w['uv'], w['conv_w'], w['conv_b'], *tabs]
    in_specs = ([prev, row(D_MODEL), nxt, _mod_spec(MOD_SHIFT1, row_of), _mod_spec(MOD_SCALE1, row_of)]
                + [_const_spec(a.shape) for a in ins[5:13]]
                + [tab] * 3)
    widths = [N_HEADS * HEAD_PAD, None, LRU_WIDTH]
    dtypes = [BF16, BF16, F32]
    if latent:
        extra = [w['gq'], w['uq']]
        ins += extra
        in_specs += [_const_spec(a.shape) for a in extra]
        widths += [N_HEADS * HEAD_PAD, LRU_WIDTH]
        dtypes += [BF16, F32]
    hv = N_HEADS * VT_ROWS
    out_specs = [row(wd) for wd in widths if wd is not None]
    out_shape = [jax.ShapeDtypeStruct((b, s, wd), dt) for wd, dt in zip(widths, dtypes) if wd is not None]
    out_specs.insert(1, pl.BlockSpec((1, hv, tm), lambda i, j: (j, 0, i)))
    out_shape.insert(1, jax.ShapeDtypeStruct((b, hv, s), BF16))
    return pl.pallas_call(
        functools.partial(_proj_body, latent),
        grid=(s // tm, b),
        in_specs=in_specs,
        out_specs=out_specs,
        out_shape=out_shape,
        compiler_params=_params(("parallel", "parallel")),
        name="proj_latent" if latent else "proj_ctx",
    )(*ins)


def _gelu_tanh(x):
    return 0.5 * x * (1.0 + jnp.tanh(0.7978845608028654 * (x + 0.044715 * (x * x * x))))


def _lru_body(latent, xc_ref, wg_ref, lam_ref, *rest):
    if latent:
        yb_ref, h0_ref, out_ref, gates, scans = rest
    else:
        st_ref, gates, scans = rest
    s = xc_ref.shape[1]
    af, uf, ab, ub = (gates.at[pl.ds(k * s, s)] for k in range(4))
    hfl, pfl, hbl, pbl = (scans.at[pl.ds(k * s, s)] for k in range(4))
    seg = s // SUBLANES
    tc = min(LRU_TC, s)
    nc = s // tc
    piece = min(tc, seg)

    wg = wg_ref[0]
    lam = lam_ref[0]
    hcsp = (0.5 * LRU_C) * (jnp.maximum(-lam, 0.0) + jnp.log1p(jnp.exp(-jnp.abs(lam))))
    bias_lhs = jnp.where(lax.broadcasted_iota(jnp.int32, (tc, LANES), 1) < GATE_BIAS_ROWS, 1.0, 0.0).astype(BF16)

    def interleaved(ci):
        pieces = []
        for k in range(tc // piece):
            t0 = ci * tc + k * piece
            j = t0 // seg
            pieces.append((slice(k * piece, (k + 1) * piece),
                           pl.ds((t0 - j * seg) * SUBLANES + j, piece, stride=SUBLANES)))
        return pieces

    def gate_chunk(ci, carry):
        r0 = pl.multiple_of(ci * tc, tc)
        xc = xc_ref[0, pl.ds(r0, tc), :]
        t = jnp.tanh(_dot(jnp.concatenate([xc.astype(BF16), bias_lhs], axis=1), wg))
        hx = 0.5 * xc
        for d, (a_ref, u_ref) in enumerate(((af, uf), (ab, ub))):
            t_r = t[:, (2 * d) * LANES:(2 * d + 1) * LANES]
            t_i = t[:, (2 * d + 1) * LANES:(2 * d + 2) * LANES]
            neg_log_a = hcsp[d:d + 1] + hcsp[d:d + 1] * t_r
            a = jnp.exp2(neg_log_a * (-LOG2_E))
            y = (1.0 - a) * (1.0 + a)
            mult = y * lax.rsqrt(jnp.maximum(y, TINY))
            u = (mult * hx) * (1.0 + t_i)
            for rows, rd in interleaved(ci):
                a_ref[rd, :] = a[rows]
                u_ref[rd, :] = u[rows]
        return carry

    lax.fori_loop(0, nc, gate_chunk, 0, unroll=min(8, nc))

    def scan_step(i, carry):
        hf, pf, hb, pb = carry
        fwd = pl.ds(pl.multiple_of(i * SUBLANES, SUBLANES), SUBLANES)
        bwd = pl.ds(pl.multiple_of((seg - 1 - i) * SUBLANES, SUBLANES), SUBLANES)
        a = af[fwd, :]
        hf = a * hf + uf[fwd, :]
        pf = a * pf
        hfl[fwd, :] = hf
        pfl[fwd, :] = pf
        a = ab[bwd, :]
        hb = a * hb + ub[bwd, :]
        pb = a * pb
        hbl[bwd, :] = hb
        pbl[bwd, :] = pb
        return hf, pf, hb, pb

    zeros = jnp.zeros((SUBLANES, LANES), F32)
    ones = jnp.ones((SUBLANES, LANES), F32)
    hf, pf, hb, pb = lax.fori_loop(0, seg, scan_step, (zeros, ones, zeros, ones), unroll=LRU_UNROLL)

    if latent:
        h0 = h0_ref[0]
        cf, cbk = h0[0:1], h0[1:2]
    else:
        cf = cbk = jnp.zeros((1, LANES), F32)
    cfs = []
    for j in range(SUBLANES):
        cfs.append(cf)
        cf = hf[j:j + 1] + pf[j:j + 1] * cf
    cbs = [None] * SUBLANES
    for j in reversed(range(SUBLANES)):
        cbs[j] = cbk
        cbk = hb[j:j + 1] + pb[j:j + 1] * cbk

    if latent:
        fix_rows = min(LRU_FIX_ROWS, s)
        cf_all = jnp.tile(jnp.concatenate(cfs, axis=0), (fix_rows // SUBLANES, 1))
        cb_all = jnp.tile(jnp.concatenate(cbs, axis=0), (fix_rows // SUBLANES, 1))

        def fix_chunk(ci, carry):
            rows = pl.ds(pl.multiple_of(ci * fix_rows, fix_rows), fix_rows)
            af[rows, :] = (hfl[rows, :] + pfl[rows, :] * cf_all) + (hbl[rows, :] + pbl[rows, :] * cb_all)
            return carry

        lax.fori_loop(0, s // fix_rows, fix_chunk, 0, unroll=2)

        def out_chunk(ci, carry):
            r0 = pl.multiple_of(ci * tc, tc)
            hsum = jnp.concatenate([af[rd, :] for _, rd in interleaved(ci)], axis=0)
            out_ref[0, pl.ds(r0, tc), :] = (hsum * yb_ref[0, pl.ds(r0, tc), :]).astype(BF16)
            return carry

        lax.fori_loop(0, nc, out_chunk, 0)
    else:
        st_ref[0] = jnp.concatenate([cf, cbk], axis=0)


def _lru_call(latent, xc, w, yb=None, h0=None):
    b, s, _ = xc.shape
    seq = pl.BlockSpec((1, s, LANES), lambda i, j: (i, 0, j))
    st = pl.BlockSpec((1, 2, LANES), lambda i, j: (i, 0, j))
    in_specs = [
        seq,
        pl.BlockSpec((1, 2 * LANES, 4 * LANES), lambda i, j: (j, 0, 0)),
        pl.BlockSpec((1, 2, LANES), lambda i, j: (j, 0, 0)),
    ]
    ins = [xc, w['gate_w'], w['lam']]
    scratch = [pltpu.VMEM((4 * s, LANES), F32)] * 2
    if latent:
        ins += [yb, h0]
        in_specs += [seq, st]
        out_specs = seq
        out_shape = jax.ShapeDtypeStruct((b, s, LRU_WIDTH), BF16)
    else:
        out_specs = st
        out_shape = jax.ShapeDtypeStruct((b, 2, LRU_WIDTH), F32)
    return pl.pallas_call(
        functools.partial(_lru_body, latent),
        grid=(b, LRU_BLOCKS),
        in_specs=in_specs,
        out_specs=out_specs,
        out_shape=out_shape,
        scratch_shapes=scratch,
        compiler_params=_params(("parallel", "parallel")),
        name="lru_latent" if latent else "lru_ctx",
    )(*ins)


def _col_groups(x, op):
    rows, cols = x.shape
    return op(x.reshape(rows // SUBLANES, SUBLANES, cols), axis=0)


def _attn_body(q_ref, kl_ref, kc_ref, vl_ref, vc_ref, o_ref, s0_ref, s1_ref, p0_ref, p1_ref, o0_ref):
    s_len = kl_ref.shape[1]
    c_len = kc_ref.shape[1]
    tk = min(ATTN_TK, s_len)
    sub = s0_ref.shape[1]
    n_sub = q_ref.shape[1] // sub
    chunks = [(kc_ref, 0, c_len, 0)] + [(kl_ref, ki * tk, tk, c_len + ki * tk) for ki in range(s_len // tk)]
    s_refs, p_refs = (s0_ref, s1_ref), (p0_ref, p1_ref)

    def qrows(qt):
        return pl.ds(pl.multiple_of(qt * sub, sub), sub)

    def score_pass(qt, hd):
        lanes = slice(hd * HEAD_PAD, (hd + 1) * HEAD_PAD)
        q = q_ref[0, qrows(qt), lanes]
        m8 = None
        for k_ref, r0, rows, off in chunks:
            sk = _dot_nt(k_ref[0, r0:r0 + rows, lanes], q)
            s_refs[hd][off:off + rows, :] = sk
            g = _col_groups(sk, jnp.max)
            m8 = g if m8 is None else jnp.maximum(m8, g)
        return jnp.max(m8, axis=0, keepdims=True)

    def exp_pass(hd, m):
        for _, _, rows, off in chunks:
            p_refs[hd][off:off + rows, :] = jnp.exp2(s_refs[hd][off:off + rows, :] - m).astype(BF16)

    def value_pass(hd):
        vrows = slice(hd * VT_ROWS, (hd + 1) * VT_ROWS)
        acc = (_dot(vc_ref[0, vrows, :], p_refs[hd][0:c_len, :])
               + _dot(vl_ref[0, vrows, :], p_refs[hd][c_len:c_len + s_len, :]))
        return acc[:V_HEAD_DIM] / acc[V_HEAD_DIM:V_HEAD_DIM + 1]

    def store(qt, o0, o1):
        o_ref[0, qrows(qt), :] = jnp.concatenate([o0, o1], axis=0).T.astype(BF16)

    def finish(hd, m):
        exp_pass(hd, m)
        return value_pass(hd)

    def head1_trip(qt, m_prev):
        o0_ref[...] = finish(0, m_prev)
        return score_pass(qt, 1)

    def head0_trip(qt, m_prev):
        store(qt - 1, o0_ref[...], finish(1, m_prev))
        return score_pass(qt, 0)

    def trip(u, m_prev):
        return lax.cond(u % 2 == 1, head1_trip, head0_trip, u // 2, m_prev)

    m = lax.fori_loop(1, 2 * n_sub, trip, score_pass(0, 0))
    store(n_sub - 1, o0_ref[...], finish(1, m))


def _attn_call(q, k_l, k_c, vt_l, vt_c):
    b, s, _ = q.shape
    c_len = k_c.shape[1]
    tq = min(ATTN_TQ, s)
    sub = min(ATTN_SUB, tq)
    pairs = N_HEADS // 2
    return pl.pallas_call(
        _attn_body,
        grid=(b, pairs, s // tq),
        in_specs=[
            pl.BlockSpec((1, tq, 2 * HEAD_PAD), lambda i, j, t: (i, t, j)),
            pl.BlockSpec((1, s, 2 * HEAD_PAD), lambda i, j, t: (i, 0, j)),
            pl.BlockSpec((1, c_len, 2 * HEAD_PAD), lambda i, j, t: (i, 0, j)),
            pl.BlockSpec((1, 2 * VT_ROWS, s), lambda i, j, t: (i, j, 0)),
            pl.BlockSpec((1, 2 * VT_ROWS, c_len), lambda i, j, t: (i, j, 0)),
        ],
        out_specs=pl.BlockSpec((1, tq, 2 * V_HEAD_DIM), lambda i, j, t: (i, t, j)),
        out_shape=jax.ShapeDtypeStruct((b, s, N_HEADS * V_HEAD_DIM), BF16),
        scratch_shapes=([pltpu.VMEM((c_len + s, sub), F32)] * 2 + [pltpu.VMEM((c_len + s, sub), BF16)] * 2
                        + [pltpu.VMEM((V_HEAD_DIM, sub), F32)]),
        compiler_params=_params(("parallel", "parallel", "parallel")),
        name="attn",
    )(q, k_l, k_c, vt_l, vt_c)


def _merge_body(attn_ref, m_ref, x_ref, sh_ref, sc_ref, g1_ref, n1_ref, bg_ref, wb_ref, woa_ref, wol_ref,
                wout_ref, o_ref):
    tm = x_ref.shape[1]
    sub = tm // MERGE_SUBTILES
    rows = [slice(i * sub, (i + 1) * sub) for i in range(MERGE_SUBTILES)]
    ys = []
    for r in rows:
        h = _rms(x_ref[0, r, :], n1_ref[...]) * (1.0 + sc_ref[...]) + sh_ref[...]
        ys.append((_dot(attn_ref[0, r, :], woa_ref[...]), _dot(m_ref[0, r, :], wol_ref[...]),
                   _dot(h.astype(BF16), wb_ref[:, IN_GL:IN_END])))
    for r, (y_a, y_b, gl) in zip(rows, ys):
        gates = 0.5 + 0.5 * jnp.tanh(0.5 * (gl + bg_ref[...]))
        mix = gates[:, :D_MODEL] * y_a + gates[:, D_MODEL:] * y_b
        o_ref[0, r, :] = x_ref[0, r, :] + g1_ref[...] * _dot(mix.astype(BF16), wout_ref[...])


def _merge_call(attn, m, x, mod, g1n, w):
    b, s, _ = x.shape
    tm = min(MERGE_TM, s)
    row = lambda width: pl.BlockSpec((1, tm, width), lambda i, j: (i, j, 0))
    batch_row = lambda i, j: i
    consts = [g1n, w['b_gate'], w['in_b'], w['o_attn'], w['o_lru'], w['out']]
    return pl.pallas_call(
        _merge_body,
        grid=(b, s // tm),
        in_specs=[row(N_HEADS * V_HEAD_DIM), row(LRU_WIDTH), row(D_MODEL), _mod_spec(MOD_SHIFT1, batch_row),
                  _mod_spec(MOD_SCALE1, batch_row), _mod_spec(MOD_GATE1, batch_row)]
        + [_const_spec(a.shape) for a in consts],
        out_specs=row(D_MODEL),
        out_shape=jax.ShapeDtypeStruct((b, s, D_MODEL), F32),
        compiler_params=_params(("parallel", "parallel")),
        name="merge",
    )(attn, m, x, mod, mod, mod, *consts)


def _ffn_body(xp_ref, x_ref, xn_ref, sh_ref, sc_ref, g2_ref, n2_ref, fg_ref, wup_ref, cw_ref, cb_ref,
              wdn_ref, o_ref, f_ref):
    j = pl.program_id(1)
    nj = pl.num_programs(1)
    tm = x_ref.shape[1]
    x = x_ref[0]
    xe = jnp.concatenate([xp_ref[0], x, xn_ref[0]], axis=0)
    hf = _rms(xe, n2_ref[...]) * (1.0 + sc_ref[...]) + sh_ref[...]
    keep_top = (j > 0).astype(F32)
    keep_bot = (j < nj - 1).astype(F32)
    ht = hf[HALO:HALO + tm]
    he = jnp.concatenate([hf[:HALO] * keep_top, ht, hf[HALO + tm:] * keep_bot], axis=0).astype(BF16)
    ht = ht.astype(BF16)
    ext = tm + 2 * HALO
    for ci in range(FFN_DIM // FFN_FC):
        cols = slice(ci * FFN_FC, (ci + 1) * FFN_FC)
        gcols = slice(FFN_DIM + ci * FFN_FC, FFN_DIM + (ci + 1) * FFN_FC)
        a = _dot(he, wup_ref[:, cols])
        cw = cw_ref[:, cols]
        conv = (cb_ref[:, cols] + pltpu.roll(a, 1, 0)[HALO:HALO + tm] * cw[0:1]
                + a[HALO:HALO + tm] * cw[1:2]
                + pltpu.roll(a, ext - 1, 0)[HALO:HALO + tm] * cw[2:3])
        g = _dot(ht, wup_ref[:, gcols])
        f_ref[:, cols] = (conv * jax.nn.sigmoid(conv) * g).astype(BF16)
    y = x + g2_ref[...] * _dot(f_ref[...], wdn_ref[...])
    o_ref[0] = _rms(y, fg_ref[...])


def _ffn_call(x1, mod, w):
    b, s, _ = x1.shape
    tm = min(FFN_TM, s)
    per = tm // HALO
    nblk = s // HALO
    row = pl.BlockSpec((1, tm, D_MODEL), lambda i, j: (i, j, 0))
    prev = pl.BlockSpec((1, HALO, D_MODEL), lambda i, j: (i, jnp.maximum(j * per - 1, 0), 0))
    nxt = pl.BlockSpec((1, HALO, D_MODEL), lambda i, j: (i, jnp.minimum((j + 1) * per, nblk - 1), 0))
    batch_row = lambda i, j: i
    consts = [w['norm2_g'], w['final_g'], w['up'], w['ffn_conv_w'], w['ffn_conv_b'], w['down']]
    return pl.pallas_call(
        _ffn_body,
        grid=(b, s // tm),
        in_specs=[prev, row, nxt, _mod_spec(MOD_SHIFT2, batch_row), _mod_spec(MOD_SCALE2, batch_row),
                  _mod_spec(MOD_GATE2, batch_row)] + [_const_spec(a.shape) for a in consts],
        out_specs=row,
        out_shape=jax.ShapeDtypeStruct((b, s, D_MODEL), F32),
        scratch_shapes=[pltpu.VMEM((tm, FFN_DIM), BF16)],
        compiler_params=_params(("parallel", "parallel")),
        name="ffn",
    )(x1, x1, x1, mod, mod, mod, *consts)


def _rope_tables(n):
    rows = n // GRID_W
    row_ids = jnp.repeat(jnp.arange(rows), GRID_W).astype(F32)
    col_ids = jnp.tile(jnp.arange(GRID_W), rows).astype(F32)
    axis_dim = QK_ROPE_DIM // 2
    inv = 1.0 / (ROPE_BASE ** (jnp.arange(0, axis_dim, 2, dtype=F32) / axis_dim))
    ang = jnp.concatenate([row_ids[:, None] * inv, col_ids[:, None] * inv], axis=-1)
    cos, sin = jnp.cos(ang), jnp.sin(ang)
    ones = lambda w_: jnp.ones((n, w_), F32)
    zeros = lambda w_: jnp.zeros((n, w_), F32)
    tail = HEAD_PAD - QK_DIM
    c = jnp.concatenate([ones(QK_NOPE_DIM), cos, cos, ones(tail)], axis=-1)
    s1 = jnp.concatenate([zeros(QK_NOPE_DIM), -sin, zeros(ROPE_HALF + tail)], axis=-1)
    s2 = jnp.concatenate([zeros(QK_NOPE_DIM + ROPE_HALF), sin, zeros(tail)], axis=-1)
    return c, s1, s2


def _identity_tables(n):
    return (jnp.ones((n, HEAD_PAD), F32), jnp.zeros((n, HEAD_PAD), F32), jnp.zeros((n, HEAD_PAD), F32))


def _prep_weights(w_in, q_norm_g, kv_norm_g, w_uq, w_ukv, w_o_attn, lru_conv_w, lru_conv_b, lru_w_a,
                  lru_b_a, lru_w_x, lru_b_x, lru_lambda, w_o_lru, w_out, b_gate, norm2_g, w_up,
                  ffn_conv_w, ffn_conv_b, w_down, final_g):
    w = {}
    kr = jnp.pad(w_in[:, OFF_KR:OFF_XB], ((0, 0), (QK_NOPE_DIM, HEAD_PAD - QK_DIM)))
    w['in_a'] = jnp.concatenate([w_in[:, :OFF_KR], kr], axis=1).astype(BF16)
    w['in_b'] = w_in[:, OFF_XB:].astype(BF16)
    w['gq'] = q_norm_g[None, :]
    w['gkv'] = kv_norm_g[None, :]
    uq = w_uq.reshape(Q_LORA_RANK, N_HEADS, QK_DIM)
    w['uq'] = jnp.pad(uq, ((0, 0), (0, 0), (0, HEAD_PAD - QK_DIM))).reshape(Q_LORA_RANK, -1).astype(BF16)
    ukv = w_ukv.reshape(KV_LORA_RANK, N_HEADS, QK_NOPE_DIM + V_HEAD_DIM)
    w['uk'] = jnp.pad(ukv[..., :QK_NOPE_DIM],
                      ((0, 0), (0, 0), (0, HEAD_PAD - QK_NOPE_DIM))).reshape(KV_LORA_RANK, -1).astype(BF16)
    w['uv'] = ukv[..., QK_NOPE_DIM:].reshape(KV_LORA_RANK, -1).T.astype(BF16)
    w['conv_w'] = lru_conv_w
    w['conv_b'] = lru_conv_b[None, :]
    gw = 0.5 * jnp.concatenate([lru_w_a[0], lru_w_x[0], lru_w_a[1], lru_w_x[1]], axis=-1)
    gb = 0.5 * jnp.stack([lru_b_a[0], lru_b_x[0], lru_b_a[1], lru_b_x[1]], axis=0)
    gb = gb.reshape(4, LRU_BLOCKS, LRU_BLOCK_W).transpose(1, 0, 2).reshape(LRU_BLOCKS, 1, -1)
    terms = []
    for _ in range(GATE_BIAS_ROWS):
        t = gb * VELTKAMP_8BIT
        hi = t - (t - gb)
        terms.append(hi)
        gb = gb - hi
    bias_rows = jnp.pad(jnp.concatenate(terms, axis=1), ((0, 0), (0, LANES - GATE_BIAS_ROWS), (0, 0)))
    w['gate_w'] = jnp.concatenate([gw, bias_rows], axis=1).astype(BF16)
    w['lam'] = lru_lambda.reshape(2, LRU_BLOCKS, LRU_BLOCK_W).transpose(1, 0, 2)
    w['o_attn'] = w_o_attn.astype(BF16)
    w['o_lru'] = w_o_lru.astype(BF16)
    w['out'] = w_out.astype(BF16)
    w['b_gate'] = b_gate[None, :]
    w['norm2_g'] = norm2_g[None, :]
    w['final_g'] = final_g[None, :]
    w['up'] = w_up.astype(BF16)
    w['ffn_conv_w'] = ffn_conv_w
    w['ffn_conv_b'] = ffn_conv_b[None, :]
    w['down'] = w_down.astype(BF16)
    return w


def kernel(x, c, ctx, c_ctx, w_mod, b_mod, norm1_g, w_in, b_gate, q_norm_g, kv_norm_g, w_uq, w_ukv,
           w_o_attn, lru_conv_w, lru_conv_b, lru_w_a, lru_b_a, lru_w_x, lru_b_x, lru_lambda, w_o_lru,
           w_out, norm2_g, w_up, ffn_conv_w, ffn_conv_b, w_down, final_g):
    assert w_mod.shape[0] == 1, "single-layer block"
    b, s, _ = x.shape
    c_len = ctx.shape[1]
    assert b + 1 <= MOD_ROWS

    w = _prep_weights(w_in[0], q_norm_g[0], kv_norm_g[0], w_uq[0], w_ukv[0], w_o_attn[0], lru_conv_w[0],
                      lru_conv_b[0], lru_w_a[0], lru_b_a[0], lru_w_x[0], lru_b_x[0], lru_lambda[0],
                      w_o_lru[0], w_out[0], b_gate[0], norm2_g[0], w_up[0], ffn_conv_w[0],
                      ffn_conv_b[0], w_down[0], final_g)
    g1n = norm1_g[0][None, :]

    cc = jnp.concatenate([c, c_ctx[None, :], jnp.zeros((MOD_ROWS - b - 1, D_MODEL), F32)], axis=0)
    mod = _mod_call(cc, w_mod, b_mod).reshape(MOD_ROWS, 6, 1, D_MODEL)

    k_c, v_c, xb_c = _proj_call(False, ctx, mod, b, g1n, w, _identity_tables(c_len))
    states = _lru_call(False, xb_c, w)

    k_l, v_l, xb_l, q_l, yb_l = _proj_call(True, x, mod, None, g1n, w, _rope_tables(s))
    m = _lru_call(True, xb_l, w, yb=yb_l, h0=states)
    attn = _attn_call(q_l, k_l, k_c, v_l, v_c)
    x1 = _merge_call(attn, m, x, mod, g1n, w)
    return _ffn_call(x1, mod, w)
```

```python
import functools

import jax
import jax.numpy as jnp
from jax import lax
from jax.experimental import pallas as pl
from jax.experimental.pallas import tpu as pltpu

F32 = jnp.float32
BF16 = jnp.bfloat16

D_MODEL = 1024
GRID_W = 64
N_HEADS = 8
QK_NOPE_DIM = 64
QK_ROPE_DIM = 32
ROPE_HALF = QK_ROPE_DIM // 2
V_HEAD_DIM = 64
QK_DIM = QK_NOPE_DIM + QK_ROPE_DIM
HEAD_PAD = 128
Q_LORA_RANK = 384
KV_LORA_RANK = 256
ROPE_BASE = 10000.0
LRU_WIDTH = 1280
LRU_BLOCKS = 10
LRU_BLOCK_W = LRU_WIDTH // LRU_BLOCKS
LRU_CONV_W = 4
LRU_C = 8.0
FFN_DIM = 2816
FFN_CONV_W = 3
EPS = 1e-6
TINY = 1e-30
OFF_KV = Q_LORA_RANK
OFF_KR = OFF_KV + KV_LORA_RANK
OFF_XB = OFF_KR + QK_ROPE_DIM
OFF_YB = OFF_XB + LRU_WIDTH
OFF_G = OFF_YB + LRU_WIDTH
IN_Q = 0
IN_KV = IN_Q + Q_LORA_RANK
IN_KR = IN_KV + KV_LORA_RANK
IN_A_END = IN_KR + HEAD_PAD
IN_XB = 0
IN_YB = IN_XB + LRU_WIDTH
IN_GL = IN_YB + LRU_WIDTH
IN_END = IN_GL + 2 * D_MODEL

SUBLANES = 8
VT_ROWS = 80
LANES = 128
VMEM_LIMIT = 56 * 1024 * 1024

MOD_ROWS = 8
MOD_SHIFT1, MOD_SCALE1, MOD_GATE1, MOD_SHIFT2, MOD_SCALE2, MOD_GATE2 = range(6)
MOD_TN = 768
PROJ_TM = 512
LRU_TC = 256
LRU_UNROLL = 16
LRU_FIX_ROWS = 64
GATE_BIAS_ROWS = 3
VELTKAMP_8BIT = 65537.0
LOG2_E = 1.4426950408889634
ATTN_TQ = 4096
ATTN_SUB = 256
ATTN_TK = 256
MERGE_TM = 512
MERGE_SUBTILES = 2
FFN_TM = 512
FFN_FC = 256
HALO = SUBLANES


def _dot(a, b):
    return jnp.dot(a, b, preferred_element_type=F32)


def _dot_nt(a, b):
    return lax.dot_general(a, b, (((1,), (1,)), ((), ())), preferred_element_type=F32)


def _rms(x, g):
    return x * lax.rsqrt(jnp.mean(x * x, axis=-1, keepdims=True) + EPS) * g


def _const_spec(shape):
    nd = len(shape)
    return pl.BlockSpec(shape, lambda *_: (0,) * nd, pipeline_mode=pl.Buffered(1))


def _mod_spec(component, row_of):
    return pl.BlockSpec((None, None, 1, D_MODEL), lambda *idx: (row_of(*idx), component, 0, 0))


def _params(sem):
    return pltpu.CompilerParams(dimension_semantics=sem, vmem_limit_bytes=VMEM_LIMIT)


def _mod_body(c_ref, w_ref, b_ref, o_ref):
    c = c_ref[...]
    s = c * jax.nn.sigmoid(c)
    o_ref[...] = _dot(s.astype(BF16), w_ref[...].astype(BF16)) + b_ref[...]


def _mod_call(cc, w_mod, b_mod):
    n = w_mod.shape[-1]
    return pl.pallas_call(
        _mod_body,
        grid=(n // MOD_TN,),
        in_specs=[
            pl.BlockSpec((MOD_ROWS, D_MODEL), lambda j: (0, 0)),
            pl.BlockSpec((None, D_MODEL, MOD_TN), lambda j: (0, 0, j)),
            pl.BlockSpec((1, MOD_TN), lambda j: (0, j)),
        ],
        out_specs=pl.BlockSpec((MOD_ROWS, MOD_TN), lambda j: (0, j)),
        out_shape=jax.ShapeDtypeStruct((MOD_ROWS, n), F32),
        compiler_params=_params(("arbitrary",)),
        name="mod",
    )(cc, w_mod, b_mod)


def _rope128(t, c, s1, s2):
    return (t * c + pltpu.roll(t, HEAD_PAD - ROPE_HALF, 1) * s1
            + pltpu.roll(t, ROPE_HALF, 1) * s2)


def _proj_body(latent, xp_ref, x_ref, xn_ref, sh_ref, sc_ref, g1_ref, wa_ref, wb_ref, gkv_ref, wuk_ref,
               wuv_ref, cw_ref, cb_ref, c_ref, s1_ref, s2_ref, *rest):
    if latent:
        gq_ref, wuq_ref, k_ref, v_ref, xc_ref, q_ref, yb_ref = rest
    else:
        k_ref, v_ref, xc_ref = rest
    i = pl.program_id(0)
    tm = x_ref.shape[1]
    xe = jnp.concatenate([xp_ref[0], x_ref[0], xn_ref[0]], axis=0)
    hf = _rms(xe, g1_ref[...]) * (1.0 + sc_ref[...]) + sh_ref[...]
    keep_top = (i > 0).astype(F32)
    keep_bot = (i < pl.num_programs(0) - 1).astype(F32)
    h = hf[HALO:HALO + tm]
    he = jnp.concatenate([hf[:HALO] * keep_top, h, hf[HALO + tm:] * keep_bot], axis=0).astype(BF16)
    hb = h.astype(BF16)
    c, s1, s2 = c_ref[...], s1_ref[...], s2_ref[...]

    if latent:
        lat = _dot(hb, wa_ref[...])
        q_lat = lat[:, :IN_KV - IN_Q]
        lat = lat[:, IN_KV - IN_Q:]
    else:
        lat = _dot(hb, wa_ref[:, IN_KV:IN_A_END])
    kv_lat, kr_raw = lat[:, :IN_KR - IN_KV], lat[:, IN_KR - IN_KV:]
    xb = _dot(he, wb_ref[:, IN_XB:IN_YB])
    ext = tm + 2 * HALO
    cw = cw_ref[...]
    xc_ref[0] = (cb_ref[...] + pltpu.roll(xb, 2, 0)[HALO:HALO + tm] * cw[0:1]
                 + pltpu.roll(xb, 1, 0)[HALO:HALO + tm] * cw[1:2]
                 + xb[HALO:HALO + tm] * cw[2:3]
                 + pltpu.roll(xb, ext - 1, 0)[HALO:HALO + tm] * cw[3:4])
    kvn = _rms(kv_lat, gkv_ref[...]).astype(BF16)
    kk = _dot(kvn, wuk_ref[...])
    vt = _dot_nt(wuv_ref[...], kvn)
    ones = jnp.ones((VT_ROWS - V_HEAD_DIM, vt.shape[1]), F32)
    v_ref[0] = jnp.concatenate([piece for hd in range(N_HEADS)
                                for piece in (vt[hd * V_HEAD_DIM:(hd + 1) * V_HEAD_DIM], ones)],
                               axis=0).astype(BF16)
    if latent:
        qn = _rms(q_lat, gq_ref[...]).astype(BF16)
        qq = _dot(qn, wuq_ref[...])
        yb_ref[0] = _gelu_tanh(_dot(hb, wb_ref[:, IN_YB:IN_GL]))
    kr = _rope128(kr_raw, c, s1, s2)
    for hd in range(N_HEADS):
        sl = slice(hd * HEAD_PAD, (hd + 1) * HEAD_PAD)
        k_ref[0, :, sl] = (kk[:, sl] + kr).astype(BF16)
    if latent:
        scale = QK_DIM ** -0.5 * LOG2_E
        for hd in range(N_HEADS):
            sl = slice(hd * HEAD_PAD, (hd + 1) * HEAD_PAD)
            q_ref[0, :, sl] = (_rope128(qq[:, sl], c, s1, s2) * scale).astype(BF16)


def _proj_call(latent, x, mod, mod_row, g1, w, tabs):
    b, s, _ = x.shape
    tm = min(PROJ_TM, s)
    row = lambda width: pl.BlockSpec((1, tm, width), lambda i, j: (j, i, 0))
    row_of = (lambda i, j: j) if mod_row is None else (lambda i, j: mod_row)
    tab = pl.BlockSpec((tm, HEAD_PAD), lambda i, j: (i, 0))
    per = tm // HALO
    nblk = s // HALO
    prev = pl.BlockSpec((1, HALO, D_MODEL), lambda i, j: (j, jnp.maximum(i * per - 1, 0), 0))
    nxt = pl.BlockSpec((1, HALO, D_MODEL), lambda i, j: (j, jnp.minimum((i + 1) * per, nblk - 1), 0))
    ins = [x, x, x, mod, mod, g1, w['in_a'], w['in_b'], w['gkv'], w['uk'], w['uv'], w['conv_w'], w['conv_b'], *tabs]
    in_specs = ([prev, row(D_MODEL), nxt, _mod_spec(MOD_SHIFT1, row_of), _mod_spec(MOD_SCALE1, row_of)]
                + [_const_spec(a.shape) for a in ins[5:13]]
                + [tab] * 3)
    widths = [N_HEADS * HEAD_PAD, None, LRU_WIDTH]
    dtypes = [BF16, BF16, F32]
    if latent:
        extra = [w['gq'], w['uq']]
        ins += extra
        in_specs += [_const_spec(a.shape) for a in extra]
        widths += [N_HEADS * HEAD_PAD, LRU_WIDTH]
        dtypes += [BF16, F32]
    hv = N_HEADS * VT_ROWS
    out_specs = [row(wd) for wd in widths if wd is not None]
    out_shape = [jax.ShapeDtypeStruct((b, s, wd), dt) for wd, dt in zip(widths, dtypes) if wd is not None]
    out_specs.insert(1, pl.BlockSpec((1, hv, tm), lambda i, j: (j, 0, i)))
    out_shape.insert(1, jax.ShapeDtypeStruct((b, hv, s), BF16))
    return pl.pallas_call(
        functools.partial(_proj_body, latent),
        grid=(s // tm, b),
        in_specs=in_specs,
        out_specs=out_specs,
        out_shape=out_shape,
        compiler_params=_params(("parallel", "parallel")),
        name="proj_latent" if latent else "proj_ctx",
    )(*ins)


def _gelu_tanh(x):
    return 0.5 * x * (1.0 + jnp.tanh(0.7978845608028654 * (x + 0.044715 * (x * x * x))))


def _lru_body(latent, xc_ref, wg_ref, lam_ref, *rest):
    for kb in range(xc_ref.shape[2] // LANES):
        _lru_block(latent, kb, xc_ref, wg_ref, lam_ref, *rest)


def _lru_block(latent, kb, xc_ref, wg_ref, lam_ref, *rest):
    if latent:
        yb_ref, h0_ref, out_ref, gates, scans = rest
    else:
        st_ref, gates, scans = rest
    s = xc_ref.shape[1]
    lanes = slice(kb * LANES, (kb + 1) * LANES)
    af, uf, ab, ub = (gates.at[pl.ds(k * s, s)] for k in range(4))
    hfl, pfl, hbl, pbl = (scans.at[pl.ds(k * s, s)] for k in range(4))
    seg = s // SUBLANES
    tc = min(LRU_TC, s)
    nc = s // tc
    piece = min(tc, seg)

    wg = wg_ref[kb]
    lam = lam_ref[kb]
    hcsp = (0.5 * LRU_C) * (jnp.maximum(-lam, 0.0) + jnp.log1p(jnp.exp(-jnp.abs(lam))))
    bias_lhs = jnp.where(lax.broadcasted_iota(jnp.int32, (tc, LANES), 1) < GATE_BIAS_ROWS, 1.0, 0.0).astype(BF16)

    def interleaved(ci):
        pieces = []
        for k in range(tc // piece):
            t0 = ci * tc + k * piece
            j = t0 // seg
            pieces.append((slice(k * piece, (k + 1) * piece),
                           pl.ds((t0 - j * seg) * SUBLANES + j, piece, stride=SUBLANES)))
        return pieces

    def gate_chunk(ci, carry):
        r0 = pl.multiple_of(ci * tc, tc)
        xc = xc_ref[0, pl.ds(r0, tc), lanes]
        t = jnp.tanh(_dot(jnp.concatenate([xc.astype(BF16), bias_lhs], axis=1), wg))
        hx = 0.5 * xc
        for d, (a_ref, u_ref) in enumerate(((af, uf), (ab, ub))):
            t_r = t[:, (2 * d) * LANES:(2 * d + 1) * LANES]
            t_i = t[:, (2 * d + 1) * LANES:(2 * d + 2) * LANES]
            neg_log_a = hcsp[d:d + 1] + hcsp[d:d + 1] * t_r
            a = jnp.exp2(neg_log_a * (-LOG2_E))
            y = (1.0 - a) * (1.0 + a)
            mult = y * lax.rsqrt(jnp.maximum(y, TINY))
            u = (mult * hx) * (1.0 + t_i)
            for rows, rd in interleaved(ci):
                a_ref[rd, :] = a[rows]
                u_ref[rd, :] = u[rows]
        return carry

    lax.fori_loop(0, nc, gate_chunk, 0, unroll=min(8, nc))

    def scan_step(i, carry):
        hf, pf, hb, pb = carry
        fwd = pl.ds(pl.multiple_of(i * SUBLANES, SUBLANES), SUBLANES)
        bwd = pl.ds(pl.multiple_of((seg - 1 - i) * SUBLANES, SUBLANES), SUBLANES)
        a = af[fwd, :]
        hf = a * hf + uf[fwd, :]
        pf = a * pf
        hfl[fwd, :] = hf
        pfl[fwd, :] = pf
        a = ab[bwd, :]
        hb = a * hb + ub[bwd, :]
        pb = a * pb
        hbl[bwd, :] = hb
        pbl[bwd, :] = pb
        return hf, pf, hb, pb

    zeros = jnp.zeros((SUBLANES, LANES), F32)
    ones = jnp.ones((SUBLANES, LANES), F32)
    hf, pf, hb, pb = lax.fori_loop(0, seg, scan_step, (zeros, ones, zeros, ones), unroll=LRU_UNROLL)

    if latent:
        h0 = h0_ref[0, :, lanes]
        cf, cbk = h0[0:1], h0[1:2]
    else:
        cf = cbk = jnp.zeros((1, LANES), F32)
    cfs = []
    for j in range(SUBLANES):
        cfs.append(cf)
        cf = hf[j:j + 1] + pf[j:j + 1] * cf
    cbs = [None] * SUBLANES
    for j in reversed(range(SUBLANES)):
        cbs[j] = cbk
        cbk = hb[j:j + 1] + pb[j:j + 1] * cbk

    if latent:
        fix_rows = min(LRU_FIX_ROWS, s)
        cf_all = jnp.tile(jnp.concatenate(cfs, axis=0), (fix_rows // SUBLANES, 1))
        cb_all = jnp.tile(jnp.concatenate(cbs, axis=0), (fix_rows // SUBLANES, 1))

        def fix_chunk(ci, carry):
            rows = pl.ds(pl.multiple_of(ci * fix_rows, fix_rows), fix_rows)
            af[rows, :] = (hfl[rows, :] + pfl[rows, :] * cf_all) + (hbl[rows, :] + pbl[rows, :] * cb_all)
            return carry

        lax.fori_loop(0, s // fix_rows, fix_chunk, 0, unroll=2)

        def out_chunk(ci, carry):
            r0 = pl.multiple_of(ci * tc, tc)
            hsum = jnp.concatenate([af[rd, :] for _, rd in interleaved(ci)], axis=0)
            out_ref[0, pl.ds(r0, tc), lanes] = (hsum * yb_ref[0, pl.ds(r0, tc), lanes]).astype(BF16)
            return carry

        lax.fori_loop(0, nc, out_chunk, 0)
    else:
        st_ref[0, :, lanes] = jnp.concatenate([cf, cbk], axis=0)


def _lru_call(latent, xc, w, yb=None, h0=None):
    b, s, _ = xc.shape
    nb = 1 if latent else LRU_BLOCKS
    seq = pl.BlockSpec((1, s, nb * LANES), lambda i, j: (i, 0, j))
    st = pl.BlockSpec((1, 2, nb * LANES), lambda i, j: (i, 0, j))
    in_specs = [
        seq,
        pl.BlockSpec((nb, 2 * LANES, 4 * LANES), lambda i, j: (j, 0, 0)),
        pl.BlockSpec((nb, 2, LANES), lambda i, j: (j, 0, 0)),
    ]
    ins = [xc, w['gate_w'], w['lam']]
    scratch = [pltpu.VMEM((4 * s, LANES), F32)] * 2
    if latent:
        ins += [yb, h0]
        in_specs += [seq, st]
        out_specs = seq
        out_shape = jax.ShapeDtypeStruct((b, s, LRU_WIDTH), BF16)
    else:
        out_specs = st
        out_shape = jax.ShapeDtypeStruct((b, 2, LRU_WIDTH), F32)
    return pl.pallas_call(
        functools.partial(_lru_body, latent),
        grid=(b, LRU_BLOCKS // nb),
        in_specs=in_specs,
        out_specs=out_specs,
        out_shape=out_shape,
        scratch_shapes=scratch,
        compiler_params=_params(("parallel", "parallel")),
        name="lru_latent" if latent else "lru_ctx",
    )(*ins)


def _col_groups(x, op):
    rows, cols = x.shape
    return op(x.reshape(rows // SUBLANES, SUBLANES, cols), axis=0)


def _attn_body(q_ref, kl_ref, kc_ref, vl_ref, vc_ref, o_ref, s0_ref, s1_ref, p0_ref, p1_ref, o0_ref):
    s_len = kl_ref.shape[1]
    c_len = kc_ref.shape[1]
    tk = min(ATTN_TK, s_len)
    sub = s0_ref.shape[1]
    n_sub = q_ref.shape[1] // sub
    chunks = [(kc_ref, 0, c_len, 0)] + [(kl_ref, ki * tk, tk, c_len + ki * tk) for ki in range(s_len // tk)]
    s_refs, p_refs = (s0_ref, s1_ref), (p0_ref, p1_ref)

    def qrows(qt):
        return pl.ds(pl.multiple_of(qt * sub, sub), sub)

    def score_pass(qt, hd):
        lanes = slice(hd * HEAD_PAD, (hd + 1) * HEAD_PAD)
        q = q_ref[0, qrows(qt), lanes]
        m8 = None
        for k_ref, r0, rows, off in chunks:
            sk = _dot_nt(k_ref[0, r0:r0 + rows, lanes], q)
            s_refs[hd][off:off + rows, :] = sk
            g = _col_groups(sk, jnp.max)
            m8 = g if m8 is None else jnp.maximum(m8, g)
        return jnp.max(m8, axis=0, keepdims=True)

    def exp_pass(hd, m):
        for _, _, rows, off in chunks:
            p_refs[hd][off:off + rows, :] = jnp.exp2(s_refs[hd][off:off + rows, :] - m).astype(BF16)

    def value_pass(hd):
        vrows = slice(hd * VT_ROWS, (hd + 1) * VT_ROWS)
        acc = (_dot(vc_ref[0, vrows, :], p_refs[hd][0:c_len, :])
               + _dot(vl_ref[0, vrows, :], p_refs[hd][c_len:c_len + s_len, :]))
        return acc[:V_HEAD_DIM] / acc[V_HEAD_DIM:V_HEAD_DIM + 1]

    def store(qt, o0, o1):
        o_ref[0, qrows(qt), :] = jnp.concatenate([o0, o1], axis=0).T.astype(BF16)

    def finish(hd, m):
        exp_pass(hd, m)
        return value_pass(hd)

    def head1_trip(qt, m_prev):
        o0_ref[...] = finish(0, m_prev)
        return score_pass(qt, 1)

    def head0_trip(qt, m_prev):
        store(qt - 1, o0_ref[...], finish(1, m_prev))
        return score_pass(qt, 0)

    def trip(u, m_prev):
        return lax.cond(u % 2 == 1, head1_trip, head0_trip, u // 2, m_prev)

    m = lax.fori_loop(1, 2 * n_sub, trip, score_pass(0, 0))
    store(n_sub - 1, o0_ref[...], finish(1, m))


def _attn_call(q, k_l, k_c, vt_l, vt_c):
    b, s, _ = q.shape
    c_len = k_c.shape[1]
    tq = min(ATTN_TQ, s)
    sub = min(ATTN_SUB, tq)
    pairs = N_HEADS // 2
    return pl.pallas_call(
        _attn_body,
        grid=(b, pairs, s // tq),
        in_specs=[
            pl.BlockSpec((1, tq, 2 * HEAD_PAD), lambda i, j, t: (i, t, j)),
            pl.BlockSpec((1, s, 2 * HEAD_PAD), lambda i, j, t: (i, 0, j)),
            pl.BlockSpec((1, c_len, 2 * HEAD_PAD), lambda i, j, t: (i, 0, j)),
            pl.BlockSpec((1, 2 * VT_ROWS, s), lambda i, j, t: (i, j, 0)),
            pl.BlockSpec((1, 2 * VT_ROWS, c_len), lambda i, j, t: (i, j, 0)),
        ],
        out_specs=pl.BlockSpec((1, tq, 2 * V_HEAD_DIM), lambda i, j, t: (i, t, j)),
        out_shape=jax.ShapeDtypeStruct((b, s, N_HEADS * V_HEAD_DIM), BF16),
        scratch_shapes=([pltpu.VMEM((c_len + s, sub), F32)] * 2 + [pltpu.VMEM((c_len + s, sub), BF16)] * 2
                        + [pltpu.VMEM((V_HEAD_DIM, sub), F32)]),
        compiler_params=_params(("parallel", "parallel", "parallel")),
        name="attn",
    )(q, k_l, k_c, vt_l, vt_c)


def _merge_body(attn_ref, m_ref, x_ref, sh_ref, sc_ref, g1_ref, n1_ref, bg_ref, wb_ref, woa_ref, wol_ref,
                wout_ref, o_ref):
    tm = x_ref.shape[1]
    sub = tm // MERGE_SUBTILES
    rows = [slice(i * sub, (i + 1) * sub) for i in range(MERGE_SUBTILES)]
    ys = []
    for r in rows:
        h = _rms(x_ref[0, r, :], n1_ref[...]) * (1.0 + sc_ref[...]) + sh_ref[...]
        ys.append((_dot(attn_ref[0, r, :], woa_ref[...]), _dot(m_ref[0, r, :], wol_ref[...]),
                   _dot(h.astype(BF16), wb_ref[:, IN_GL:IN_END])))
    for r, (y_a, y_b, gl) in zip(rows, ys):
        gates = 0.5 + 0.5 * jnp.tanh(0.5 * (gl + bg_ref[...]))
        mix = gates[:, :D_MODEL] * y_a + gates[:, D_MODEL:] * y_b
        o_ref[0, r, :] = x_ref[0, r, :] + g1_ref[...] * _dot(mix.astype(BF16), wout_ref[...])


def _merge_call(attn, m, x, mod, g1n, w):
    b, s, _ = x.shape
    tm = min(MERGE_TM, s)
    row = lambda width: pl.BlockSpec((1, tm, width), lambda i, j: (i, j, 0))
    batch_row = lambda i, j: i
    consts = [g1n, w['b_gate'], w['in_b'], w['o_attn'], w['o_lru'], w['out']]
    return pl.pallas_call(
        _merge_body,
        grid=(b, s // tm),
        in_specs=[row(N_HEADS * V_HEAD_DIM), row(LRU_WIDTH), row(D_MODEL), _mod_spec(MOD_SHIFT1, batch_row),
                  _mod_spec(MOD_SCALE1, batch_row), _mod_spec(MOD_GATE1, batch_row)]
        + [_const_spec(a.shape) for a in consts],
        out_specs=row(D_MODEL),
        out_shape=jax.ShapeDtypeStruct((b, s, D_MODEL), F32),
        compiler_params=_params(("parallel", "parallel")),
        name="merge",
    )(attn, m, x, mod, mod, mod, *consts)


def _ffn_body(xp_ref, x_ref, xn_ref, sh_ref, sc_ref, g2_ref, n2_ref, fg_ref, wup_ref, cw_ref, cb_ref,
              wdn_ref, o_ref, f_ref):
    j = pl.program_id(1)
    nj = pl.num_programs(1)
    tm = x_ref.shape[1]
    x = x_ref[0]
    xe = jnp.concatenate([xp_ref[0], x, xn_ref[0]], axis=0)
    hf = _rms(xe, n2_ref[...]) * (1.0 + sc_ref[...]) + sh_ref[...]
    keep_top = (j > 0).astype(F32)
    keep_bot = (j < nj - 1).astype(F32)
    ht = hf[HALO:HALO + tm]
    he = jnp.concatenate([hf[:HALO] * keep_top, ht, hf[HALO + tm:] * keep_bot], axis=0).astype(BF16)
    ht = ht.astype(BF16)
    ext = tm + 2 * HALO
    for ci in range(FFN_DIM // FFN_FC):
        cols = slice(ci * FFN_FC, (ci + 1) * FFN_FC)
        gcols = slice(FFN_DIM + ci * FFN_FC, FFN_DIM + (ci + 1) * FFN_FC)
        a = _dot(he, wup_ref[:, cols])
        cw = cw_ref[:, cols]
        conv = (cb_ref[:, cols] + pltpu.roll(a, 1, 0)[HALO:HALO + tm] * cw[0:1]
                + a[HALO:HALO + tm] * cw[1:2]
                + pltpu.roll(a, ext - 1, 0)[HALO:HALO + tm] * cw[2:3])
        g = _dot(ht, wup_ref[:, gcols])
        f_ref[:, cols] = (conv * jax.nn.sigmoid(conv) * g).astype(BF16)
    y = x + g2_ref[...] * _dot(f_ref[...], wdn_ref[...])
    o_ref[0] = _rms(y, fg_ref[...])


def _ffn_call(x1, mod, w):
    b, s, _ = x1.shape
    tm = min(FFN_TM, s)
    per = tm // HALO
    nblk = s // HALO
    row = pl.BlockSpec((1, tm, D_MODEL), lambda i, j: (i, j, 0))
    prev = pl.BlockSpec((1, HALO, D_MODEL), lambda i, j: (i, jnp.maximum(j * per - 1, 0), 0))
    nxt = pl.BlockSpec((1, HALO, D_MODEL), lambda i, j: (i, jnp.minimum((j + 1) * per, nblk - 1), 0))
    batch_row = lambda i, j: i
    consts = [w['norm2_g'], w['final_g'], w['up'], w['ffn_conv_w'], w['ffn_conv_b'], w['down']]
    return pl.pallas_call(
        _ffn_body,
        grid=(b, s // tm),
        in_specs=[prev, row, nxt, _mod_spec(MOD_SHIFT2, batch_row), _mod_spec(MOD_SCALE2, batch_row),
                  _mod_spec(MOD_GATE2, batch_row)] + [_const_spec(a.shape) for a in consts],
        out_specs=row,
        out_shape=jax.ShapeDtypeStruct((b, s, D_MODEL), F32),
        scratch_shapes=[pltpu.VMEM((tm, FFN_DIM), BF16)],
        compiler_params=_params(("parallel", "parallel")),
        name="ffn",
    )(x1, x1, x1, mod, mod, mod, *consts)


def _rope_tables(n):
    rows = n // GRID_W
    row_ids = jnp.repeat(jnp.arange(rows), GRID_W).astype(F32)
    col_ids = jnp.tile(jnp.arange(GRID_W), rows).astype(F32)
    axis_dim = QK_ROPE_DIM // 2
    inv = 1.0 / (ROPE_BASE ** (jnp.arange(0, axis_dim, 2, dtype=F32) / axis_dim))
    ang = jnp.concatenate([row_ids[:, None] * inv, col_ids[:, None] * inv], axis=-1)
    cos, sin = jnp.cos(ang), jnp.sin(ang)
    ones = lambda w_: jnp.ones((n, w_), F32)
    zeros = lambda w_: jnp.zeros((n, w_), F32)
    tail = HEAD_PAD - QK_DIM
    c = jnp.concatenate([ones(QK_NOPE_DIM), cos, cos, ones(tail)], axis=-1)
    s1 = jnp.concatenate([zeros(QK_NOPE_DIM), -sin, zeros(ROPE_HALF + tail)], axis=-1)
    s2 = jnp.concatenate([zeros(QK_NOPE_DIM + ROPE_HALF), sin, zeros(tail)], axis=-1)
    return c, s1, s2


def _identity_tables(n):
    return (jnp.ones((n, HEAD_PAD), F32), jnp.zeros((n, HEAD_PAD), F32), jnp.zeros((n, HEAD_PAD), F32))


def _prep_weights(w_in, q_norm_g, kv_norm_g, w_uq, w_ukv, w_o_attn, lru_conv_w, lru_conv_b, lru_w_a,
                  lru_b_a, lru_w_x, lru_b_x, lru_lambda, w_o_lru, w_out, b_gate, norm2_g, w_up,
                  ffn_conv_w, ffn_conv_b, w_down, final_g):
    w = {}
    kr = jnp.pad(w_in[:, OFF_KR:OFF_XB], ((0, 0), (QK_NOPE_DIM, HEAD_PAD - QK_DIM)))
    w['in_a'] = jnp.concatenate([w_in[:, :OFF_KR], kr], axis=1).astype(BF16)
    w['in_b'] = w_in[:, OFF_XB:].astype(BF16)
    w['gq'] = q_norm_g[None, :]
    w['gkv'] = kv_norm_g[None, :]
    uq = w_uq.reshape(Q_LORA_RANK, N_HEADS, QK_DIM)
    w['uq'] = jnp.pad(uq, ((0, 0), (0, 0), (0, HEAD_PAD - QK_DIM))).reshape(Q_LORA_RANK, -1).astype(BF16)
    ukv = w_ukv.reshape(KV_LORA_RANK, N_HEADS, QK_NOPE_DIM + V_HEAD_DIM)
    w['uk'] = jnp.pad(ukv[..., :QK_NOPE_DIM],
                      ((0, 0), (0, 0), (0, HEAD_PAD - QK_NOPE_DIM))).reshape(KV_LORA_RANK, -1).astype(BF16)
    w['uv'] = ukv[..., QK_NOPE_DIM:].reshape(KV_LORA_RANK, -1).T.astype(BF16)
    w['conv_w'] = lru_conv_w
    w['conv_b'] = lru_conv_b[None, :]
    gw = 0.5 * jnp.concatenate([lru_w_a[0], lru_w_x[0], lru_w_a[1], lru_w_x[1]], axis=-1)
    gb = 0.5 * jnp.stack([lru_b_a[0], lru_b_x[0], lru_b_a[1], lru_b_x[1]], axis=0)
    gb = gb.reshape(4, LRU_BLOCKS, LRU_BLOCK_W).transpose(1, 0, 2).reshape(LRU_BLOCKS, 1, -1)
    terms = []
    for _ in range(GATE_BIAS_ROWS):
        t = gb * VELTKAMP_8BIT
        hi = t - (t - gb)
        terms.append(hi)
        gb = gb - hi
    bias_rows = jnp.pad(jnp.concatenate(terms, axis=1), ((0, 0), (0, LANES - GATE_BIAS_ROWS), (0, 0)))
    w['gate_w'] = jnp.concatenate([gw, bias_rows], axis=1).astype(BF16)
    w['lam'] = lru_lambda.reshape(2, LRU_BLOCKS, LRU_BLOCK_W).transpose(1, 0, 2)
    w['o_attn'] = w_o_attn.astype(BF16)
    w['o_lru'] = w_o_lru.astype(BF16)
    w['out'] = w_out.astype(BF16)
    w['b_gate'] = b_gate[None, :]
    w['norm2_g'] = norm2_g[None, :]
    w['final_g'] = final_g[None, :]
    w['up'] = w_up.astype(BF16)
    w['ffn_conv_w'] = ffn_conv_w
    w['ffn_conv_b'] = ffn_conv_b[None, :]
    w['down'] = w_down.astype(BF16)
    return w


def kernel(x, c, ctx, c_ctx, w_mod, b_mod, norm1_g, w_in, b_gate, q_norm_g, kv_norm_g, w_uq, w_ukv,
           w_o_attn, lru_conv_w, lru_conv_b, lru_w_a, lru_b_a, lru_w_x, lru_b_x, lru_lambda, w_o_lru,
           w_out, norm2_g, w_up, ffn_conv_w, ffn_conv_b, w_down, final_g):
    assert w_mod.shape[0] == 1, "single-layer block"
    b, s, _ = x.shape
    c_len = ctx.shape[1]
    assert b + 1 <= MOD_ROWS

    w = _prep_weights(w_in[0], q_norm_g[0], kv_norm_g[0], w_uq[0], w_ukv[0], w_o_attn[0], lru_conv_w[0],
                      lru_conv_b[0], lru_w_a[0], lru_b_a[0], lru_w_x[0], lru_b_x[0], lru_lambda[0],
                      w_o_lru[0], w_out[0], b_gate[0], norm2_g[0], w_up[0], ffn_conv_w[0],
                      ffn_conv_b[0], w_down[0], final_g)
    g1n = norm1_g[0][None, :]

    cc = jnp.concatenate([c, c_ctx[None, :], jnp.zeros((MOD_ROWS - b - 1, D_MODEL), F32)], axis=0)
    mod = _mod_call(cc, w_mod, b_mod).reshape(MOD_ROWS, 6, 1, D_MODEL)

    k_c, v_c, xb_c = _proj_call(False, ctx, mod, b, g1n, w, _identity_tables(c_len))
    states = _lru_call(False, xb_c, w)

    k_l, v_l, xb_l, q_l, yb_l = _proj_call(True, x, mod, None, g1n, w, _rope_tables(s))
    m = _lru_call(True, xb_l, w, yb=yb_l, h0=states)
    attn = _attn_call(q_l, k_l, k_c, v_l, v_c)
    x1 = _merge_call(attn, m, x, mod, g1n, w)
    return _ffn_call(x1, mod, w)
```

```python
import functools

import jax
import jax.numpy as jnp
from jax import lax
from jax.experimental import pallas as pl
from jax.experimental.pallas import tpu as pltpu

F32 = jnp.float32
BF16 = jnp.bfloat16

D_MODEL = 1024
GRID_W = 64
N_HEADS = 8
QK_NOPE_DIM = 64
QK_ROPE_DIM = 32
ROPE_HALF = QK_ROPE_DIM // 2
V_HEAD_DIM = 64
QK_DIM = QK_NOPE_DIM + QK_ROPE_DIM
HEAD_PAD = 128
Q_LORA_RANK = 384
KV_LORA_RANK = 256
ROPE_BASE = 10000.0
LRU_WIDTH = 1280
LRU_BLOCKS = 10
LRU_BLOCK_W = LRU_WIDTH // LRU_BLOCKS
LRU_C = 8.0
FFN_DIM = 2816
EPS = 1e-6
TINY = 1e-30
LOG2_E = 1.4426950408889634
OFF_KV = Q_LORA_RANK
OFF_KR = OFF_KV + KV_LORA_RANK
OFF_XB = OFF_KR + QK_ROPE_DIM
IN_Q = 0
IN_KV = IN_Q + Q_LORA_RANK
IN_KR = IN_KV + KV_LORA_RANK
IN_A_END = IN_KR + HEAD_PAD
IN_XB = 0
IN_YB = IN_XB + LRU_WIDTH
IN_GL = IN_YB + LRU_WIDTH
IN_END = IN_GL + 2 * D_MODEL

SUBLANES = 8
LANES = 128
HALO = SUBLANES
VMEM_LIMIT = 56 * 1024 * 1024

MOD_ROWS = 8
MOD_SHIFT1, MOD_SCALE1, MOD_GATE1, MOD_SHIFT2, MOD_SCALE2, MOD_GATE2 = range(6)
MOD_TN = 1536
PROJ_TM = 512
LRU_TC = 256
LRU_UNROLL = 16
LRU_FIX_ROWS = 64
GATE_BIAS_ROWS = 3
VELTKAMP_8BIT = 65537.0
VT_ROWS = 80
ATTN_TQ = 4096
ATTN_SUB = 256
ATTN_TK = 256
MERGE_TM = 1024
MERGE_SUBTILES = 4
FFN_TM = 1024
FFN_FC = 256


def _dot(a, b):
    return jnp.dot(a, b, preferred_element_type=F32)


def _dot_nt(a, b):
    return lax.dot_general(a, b, (((1,), (1,)), ((), ())), preferred_element_type=F32)


def _rms(x, g):
    return x * lax.rsqrt(jnp.mean(x * x, axis=-1, keepdims=True) + EPS) * g


def _const_spec(shape):
    nd = len(shape)
    return pl.BlockSpec(shape, lambda *_: (0,) * nd, pipeline_mode=pl.Buffered(1))


def _mod_spec(component, row_of):
    return pl.BlockSpec((None, None, 1, D_MODEL), lambda *idx: (row_of(*idx), component, 0, 0))


def _params(sem):
    return pltpu.CompilerParams(dimension_semantics=sem, vmem_limit_bytes=VMEM_LIMIT)


def _mod_body(c_ref, w_ref, b_ref, o_ref):
    c = c_ref[...]
    s = c * jax.nn.sigmoid(c)
    o_ref[...] = _dot(s.astype(BF16), w_ref[...].astype(BF16)) + b_ref[...]


def _mod_call(cc, w_mod, b_mod):
    n = w_mod.shape[-1]
    return pl.pallas_call(
        _mod_body,
        grid=(n // MOD_TN,),
        in_specs=[
            pl.BlockSpec((MOD_ROWS, D_MODEL), lambda j: (0, 0)),
            pl.BlockSpec((None, D_MODEL, MOD_TN), lambda j: (0, 0, j)),
            pl.BlockSpec((1, MOD_TN), lambda j: (0, j)),
        ],
        out_specs=pl.BlockSpec((MOD_ROWS, MOD_TN), lambda j: (0, j)),
        out_shape=jax.ShapeDtypeStruct((MOD_ROWS, n), F32),
        compiler_params=_params(("arbitrary",)),
        name="mod",
    )(cc, w_mod, b_mod)


def _rope128(t, c, s1, s2):
    return (t * c + pltpu.roll(t, HEAD_PAD - ROPE_HALF, 1) * s1
            + pltpu.roll(t, ROPE_HALF, 1) * s2)


def _proj_body(latent, xp_ref, x_ref, xn_ref, sh_ref, sc_ref, g1_ref, wa_ref, wb_ref, gkv_ref, wuk_ref,
               wuv_ref, cw_ref, cb_ref, c_ref, s1_ref, s2_ref, *rest):
    if latent:
        gq_ref, wuq_ref, k_ref, v_ref, xc_ref, q_ref, yb_ref = rest
    else:
        k_ref, v_ref, xc_ref = rest
    i = pl.program_id(0)
    tm = x_ref.shape[1]
    xe = jnp.concatenate([xp_ref[0], x_ref[0], xn_ref[0]], axis=0)
    hf = _rms(xe, g1_ref[...]) * (1.0 + sc_ref[...]) + sh_ref[...]
    keep_top = (i > 0).astype(F32)
    keep_bot = (i < pl.num_programs(0) - 1).astype(F32)
    h = hf[HALO:HALO + tm]
    he = jnp.concatenate([hf[:HALO] * keep_top, h, hf[HALO + tm:] * keep_bot], axis=0).astype(BF16)
    hb = h.astype(BF16)
    c, s1, s2 = c_ref[...], s1_ref[...], s2_ref[...]

    if latent:
        lat = _dot(hb, wa_ref[...])
        q_lat = lat[:, :IN_KV - IN_Q]
        lat = lat[:, IN_KV - IN_Q:]
    else:
        lat = _dot(hb, wa_ref[:, IN_KV:IN_A_END])
    kv_lat, kr_raw = lat[:, :IN_KR - IN_KV], lat[:, IN_KR - IN_KV:]
    xb = _dot(he, wb_ref[:, IN_XB:IN_YB])
    ext = tm + 2 * HALO
    cw = cw_ref[...]
    xc_ref[0] = (cb_ref[...] + pltpu.roll(xb, 2, 0)[HALO:HALO + tm] * cw[0:1]
                 + pltpu.roll(xb, 1, 0)[HALO:HALO + tm] * cw[1:2]
                 + xb[HALO:HALO + tm] * cw[2:3]
                 + pltpu.roll(xb, ext - 1, 0)[HALO:HALO + tm] * cw[3:4])
    kvn = _rms(kv_lat, gkv_ref[...]).astype(BF16)
    kk = _dot(kvn, wuk_ref[...])
    vt = _dot_nt(wuv_ref[...], kvn)
    ones = jnp.ones((VT_ROWS - V_HEAD_DIM, vt.shape[1]), F32)
    v_ref[0] = jnp.concatenate([piece for hd in range(N_HEADS)
                                for piece in (vt[hd * V_HEAD_DIM:(hd + 1) * V_HEAD_DIM], ones)],
                               axis=0).astype(BF16)
    if latent:
        qn = _rms(q_lat, gq_ref[...]).astype(BF16)
        qq = _dot(qn, wuq_ref[...])
        yb_ref[0] = _gelu_tanh(_dot(hb, wb_ref[:, IN_YB:IN_GL]))
    kr = _rope128(kr_raw, c, s1, s2)
    for hd in range(N_HEADS):
        sl = slice(hd * HEAD_PAD, (hd + 1) * HEAD_PAD)
        k_ref[0, :, sl] = (kk[:, sl] + kr).astype(BF16)
    if latent:
        scale = QK_DIM ** -0.5 * LOG2_E
        for hd in range(N_HEADS):
            sl = slice(hd * HEAD_PAD, (hd + 1) * HEAD_PAD)
            q_ref[0, :, sl] = (_rope128(qq[:, sl], c, s1, s2) * scale).astype(BF16)


def _proj_call(latent, x, mod, mod_row, g1, w, tabs):
    b, s, _ = x.shape
    tm = min(PROJ_TM, s)
    row = lambda width: pl.BlockSpec((1, tm, width), lambda i, j: (j, i, 0))
    row_of = (lambda i, j: j) if mod_row is None else (lambda i, j: mod_row)
    tab = pl.BlockSpec((tm, HEAD_PAD), lambda i, j: (i, 0))
    per = tm // HALO
    nblk = s // HALO
    prev = pl.BlockSpec((1, HALO, D_MODEL), lambda i, j: (j, jnp.maximum(i * per - 1, 0), 0))
    nxt = pl.BlockSpec((1, HALO, D_MODEL), lambda i, j: (j, jnp.minimum((i + 1) * per, nblk - 1), 0))
    ins = [x, x, x, mod, mod, g1, w['in_a'], w['in_b'], w['gkv'], w['uk'], w['uv'], w['conv_w'], w['conv_b'], *tabs]
    in_specs = ([prev, row(D_MODEL), nxt, _mod_spec(MOD_SHIFT1, row_of), _mod_spec(MOD_SCALE1, row_of)]
                + [_const_spec(a.shape) for a in ins[5:13]]
                + [tab] * 3)
    widths = [N_HEADS * HEAD_PAD, None, LRU_WIDTH]
    dtypes = [BF16, BF16, F32]
    if latent:
        extra = [w['gq'], w['uq']]
        ins += extra
        in_specs += [_const_spec(a.shape) for a in extra]
        widths += [N_HEADS * HEAD_PAD, LRU_WIDTH]
        dtypes += [BF16, F32]
    hv = N_HEADS * VT_ROWS
    out_specs = [row(wd) for wd in widths if wd is not None]
    out_shape = [jax.ShapeDtypeStruct((b, s, wd), dt) for wd, dt in zip(widths, dtypes) if wd is not None]
    out_specs.insert(1, pl.BlockSpec((1, hv, tm), lambda i, j: (j, 0, i)))
    out_shape.insert(1, jax.ShapeDtypeStruct((b, hv, s), BF16))
    return pl.pallas_call(
        functools.partial(_proj_body, latent),
        grid=(s // tm, b),
        in_specs=in_specs,
        out_specs=out_specs,
        out_shape=out_shape,
        compiler_params=_params(("parallel", "parallel")),
        name="proj_latent" if latent else "proj_ctx",
    )(*ins)


def _gelu_tanh(x):
    return 0.5 * x * (1.0 + jnp.tanh(0.7978845608028654 * (x + 0.044715 * (x * x * x))))


def _lru_body(latent, xc_ref, wg_ref, lam_ref, *rest):
    for kb in range(xc_ref.shape[2] // LANES):
        _lru_block(latent, kb, xc_ref, wg_ref, lam_ref, *rest)


def _lru_block(latent, kb, xc_ref, wg_ref, lam_ref, *rest):
    if latent:
        yb_ref, h0_ref, out_ref, gates, scans = rest
    else:
        st_ref, gates, scans = rest
    s = xc_ref.shape[1]
    lanes = slice(kb * LANES, (kb + 1) * LANES)
    af, uf, ab, ub = (gates.at[pl.ds(k * s, s)] for k in range(4))
    hfl, pfl, hbl, pbl = (scans.at[pl.ds(k * s, s)] for k in range(4))
    seg = s // SUBLANES
    tc = min(LRU_TC, s)
    nc = s // tc
    piece = min(tc, seg)

    wg = wg_ref[kb]
    lam = lam_ref[kb]
    hcsp = (0.5 * LRU_C) * (jnp.maximum(-lam, 0.0) + jnp.log1p(jnp.exp(-jnp.abs(lam))))
    bias_lhs = jnp.where(lax.broadcasted_iota(jnp.int32, (tc, LANES), 1) < GATE_BIAS_ROWS, 1.0, 0.0).astype(BF16)

    def interleaved(ci):
        pieces = []
        for k in range(tc // piece):
            t0 = ci * tc + k * piece
            j = t0 // seg
            pieces.append((slice(k * piece, (k + 1) * piece),
                           pl.ds((t0 - j * seg) * SUBLANES + j, piece, stride=SUBLANES)))
        return pieces

    def gate_chunk(ci, carry):
        r0 = pl.multiple_of(ci * tc, tc)
        xc = xc_ref[0, pl.ds(r0, tc), lanes]
        t = jnp.tanh(_dot(jnp.concatenate([xc.astype(BF16), bias_lhs], axis=1), wg))
        hx = 0.5 * xc
        for d, (a_ref, u_ref) in enumerate(((af, uf), (ab, ub))):
            t_r = t[:, (2 * d) * LANES:(2 * d + 1) * LANES]
            t_i = t[:, (2 * d + 1) * LANES:(2 * d + 2) * LANES]
            neg_log_a = hcsp[d:d + 1] + hcsp[d:d + 1] * t_r
            a = jnp.exp2(neg_log_a * (-LOG2_E))
            y = (1.0 - a) * (1.0 + a)
            mult = y * lax.rsqrt(jnp.maximum(y, TINY))
            u = (mult * hx) * (1.0 + t_i)
            for rows, rd in interleaved(ci):
                a_ref[rd, :] = a[rows]
                u_ref[rd, :] = u[rows]
        return carry

    lax.fori_loop(0, nc, gate_chunk, 0, unroll=min(8, nc))

    def scan_step(i, carry):
        hf, pf, hb, pb = carry
        fwd = pl.ds(pl.multiple_of(i * SUBLANES, SUBLANES), SUBLANES)
        bwd = pl.ds(pl.multiple_of((seg - 1 - i) * SUBLANES, SUBLANES), SUBLANES)
        a = af[fwd, :]
        hf = a * hf + uf[fwd, :]
        pf = a * pf
        hfl[fwd, :] = hf
        pfl[fwd, :] = pf
        a = ab[bwd, :]
        hb = a * hb + ub[bwd, :]
        pb = a * pb
        hbl[bwd, :] = hb
        pbl[bwd, :] = pb
        return hf, pf, hb, pb

    zeros = jnp.zeros((SUBLANES, LANES), F32)
    ones = jnp.ones((SUBLANES, LANES), F32)
    hf, pf, hb, pb = lax.fori_loop(0, seg, scan_step, (zeros, ones, zeros, ones), unroll=LRU_UNROLL)

    if latent:
        h0 = h0_ref[0, :, lanes]
        cf, cbk = h0[0:1], h0[1:2]
    else:
        cf = cbk = jnp.zeros((1, LANES), F32)
    cfs = []
    for j in range(SUBLANES):
        cfs.append(cf)
        cf = hf[j:j + 1] + pf[j:j + 1] * cf
    cbs = [None] * SUBLANES
    for j in reversed(range(SUBLANES)):
        cbs[j] = cbk
        cbk = hb[j:j + 1] + pb[j:j + 1] * cbk

    if latent:
        fix_rows = min(LRU_FIX_ROWS, s)
        cf_all = jnp.tile(jnp.concatenate(cfs, axis=0), (fix_rows // SUBLANES, 1))
        cb_all = jnp.tile(jnp.concatenate(cbs, axis=0), (fix_rows // SUBLANES, 1))

        def fix_chunk(ci, carry):
            rows = pl.ds(pl.multiple_of(ci * fix_rows, fix_rows), fix_rows)
            af[rows, :] = (hfl[rows, :] + pfl[rows, :] * cf_all) + (hbl[rows, :] + pbl[rows, :] * cb_all)
            return carry

        lax.fori_loop(0, s // fix_rows, fix_chunk, 0, unroll=2)

        def out_chunk(ci, carry):
            r0 = pl.multiple_of(ci * tc, tc)
            hsum = jnp.concatenate([af[rd, :] for _, rd in interleaved(ci)], axis=0)
            out_ref[0, pl.ds(r0, tc), lanes] = (hsum * yb_ref[0, pl.ds(r0, tc), lanes]).astype(BF16)
            return carry

        lax.fori_loop(0, nc, out_chunk, 0)
    else:
        st_ref[0, :, lanes] = jnp.concatenate([cf, cbk], axis=0)


def _lru_call(latent, xc, w, yb=None, h0=None):
    b, s, _ = xc.shape
    nb = 1 if latent else LRU_BLOCKS
    seq = pl.BlockSpec((1, s, nb * LANES), lambda i, j: (i, 0, j))
    st = pl.BlockSpec((1, 2, nb * LANES), lambda i, j: (i, 0, j))
    in_specs = [
        seq,
        pl.BlockSpec((nb, 2 * LANES, 4 * LANES), lambda i, j: (j, 0, 0)),
        pl.BlockSpec((nb, 2, LANES), lambda i, j: (j, 0, 0)),
    ]
    ins = [xc, w['gate_w'], w['lam']]
    scratch = [pltpu.VMEM((4 * s, LANES), F32)] * 2
    if latent:
        ins += [yb, h0]
        in_specs += [seq, st]
        out_specs = seq
        out_shape = jax.ShapeDtypeStruct((b, s, LRU_WIDTH), BF16)
    else:
        out_specs = st
        out_shape = jax.ShapeDtypeStruct((b, 2, LRU_WIDTH), F32)
    return pl.pallas_call(
        functools.partial(_lru_body, latent),
        grid=(b, LRU_BLOCKS // nb),
        in_specs=in_specs,
        out_specs=out_specs,
        out_shape=out_shape,
        scratch_shapes=scratch,
        compiler_params=_params(("parallel", "parallel")),
        name="lru_latent" if latent else "lru_ctx",
    )(*ins)


def _col_groups(x, op):
    rows, cols = x.shape
    return op(x.reshape(rows // SUBLANES, SUBLANES, cols), axis=0)


def _attn_body(q_ref, kl_ref, kc_ref, vl_ref, vc_ref, o_ref, s0_ref, s1_ref, p0_ref, p1_ref, o0_ref):
    s_len = kl_ref.shape[1]
    c_len = kc_ref.shape[1]
    tk = min(ATTN_TK, s_len)
    sub = s0_ref.shape[1]
    n_sub = q_ref.shape[1] // sub
    chunks = [(kc_ref, 0, c_len, 0)] + [(kl_ref, ki * tk, tk, c_len + ki * tk) for ki in range(s_len // tk)]
    s_refs, p_refs = (s0_ref, s1_ref), (p0_ref, p1_ref)

    def qrows(qt):
        return pl.ds(pl.multiple_of(qt * sub, sub), sub)

    def score_pass(qt, hd):
        lanes = slice(hd * HEAD_PAD, (hd + 1) * HEAD_PAD)
        q = q_ref[0, qrows(qt), lanes]
        m8 = None
        for k_ref, r0, rows, off in chunks:
            sk = _dot_nt(k_ref[0, r0:r0 + rows, lanes], q)
            s_refs[hd][off:off + rows, :] = sk
            g = _col_groups(sk, jnp.max)
            m8 = g if m8 is None else jnp.maximum(m8, g)
        return jnp.max(m8, axis=0, keepdims=True)

    def exp_pass(hd, m):
        for _, _, rows, off in chunks:
            p_refs[hd][off:off + rows, :] = jnp.exp2(s_refs[hd][off:off + rows, :] - m).astype(BF16)

    def value_pass(hd):
        vrows = slice(hd * VT_ROWS, (hd + 1) * VT_ROWS)
        acc = (_dot(vc_ref[0, vrows, :], p_refs[hd][0:c_len, :])
               + _dot(vl_ref[0, vrows, :], p_refs[hd][c_len:c_len + s_len, :]))
        return acc[:V_HEAD_DIM] / acc[V_HEAD_DIM:V_HEAD_DIM + 1]

    def store(qt, o0, o1):
        o_ref[0, qrows(qt), :] = jnp.concatenate([o0, o1], axis=0).T.astype(BF16)

    def finish(hd, m):
        exp_pass(hd, m)
        return value_pass(hd)

    def head1_trip(qt, m_prev):
        o0_ref[...] = finish(0, m_prev)
        return score_pass(qt, 1)

    def head0_trip(qt, m_prev):
        store(qt - 1, o0_ref[...], finish(1, m_prev))
        return score_pass(qt, 0)

    def trip(u, m_prev):
        return lax.cond(u % 2 == 1, head1_trip, head0_trip, u // 2, m_prev)

    m = lax.fori_loop(1, 2 * n_sub, trip, score_pass(0, 0))
    store(n_sub - 1, o0_ref[...], finish(1, m))


def _attn_call(q, k_l, k_c, vt_l, vt_c):
    b, s, _ = q.shape
    c_len = k_c.shape[1]
    tq = min(ATTN_TQ, s)
    sub = min(ATTN_SUB, tq)
    pairs = N_HEADS // 2
    return pl.pallas_call(
        _attn_body,
        grid=(b, pairs, s // tq),
        in_specs=[
            pl.BlockSpec((1, tq, 2 * HEAD_PAD), lambda i, j, t: (i, t, j)),
            pl.BlockSpec((1, s, 2 * HEAD_PAD), lambda i, j, t: (i, 0, j)),
            pl.BlockSpec((1, c_len, 2 * HEAD_PAD), lambda i, j, t: (i, 0, j)),
            pl.BlockSpec((1, 2 * VT_ROWS, s), lambda i, j, t: (i, j, 0)),
            pl.BlockSpec((1, 2 * VT_ROWS, c_len), lambda i, j, t: (i, j, 0)),
        ],
        out_specs=pl.BlockSpec((1, tq, 2 * V_HEAD_DIM), lambda i, j, t: (i, t, j)),
        out_shape=jax.ShapeDtypeStruct((b, s, N_HEADS * V_HEAD_DIM), BF16),
        scratch_shapes=([pltpu.VMEM((c_len + s, sub), F32)] * 2 + [pltpu.VMEM((c_len + s, sub), BF16)] * 2
                        + [pltpu.VMEM((V_HEAD_DIM, sub), F32)]),
        compiler_params=_params(("parallel", "parallel", "parallel")),
        name="attn",
    )(q, k_l, k_c, vt_l, vt_c)


def _merge_body(attn_ref, m_ref, x_ref, sh_ref, sc_ref, g1_ref, n1_ref, bg_ref, wb_ref, woa_ref, wol_ref,
                wout_ref, o_ref):
    tm = x_ref.shape[1]
    sub = tm // MERGE_SUBTILES
    rows = [slice(i * sub, (i + 1) * sub) for i in range(MERGE_SUBTILES)]
    ys = []
    for r in rows:
        h = _rms(x_ref[0, r, :], n1_ref[...]) * (1.0 + sc_ref[...]) + sh_ref[...]
        ys.append((_dot(attn_ref[0, r, :], woa_ref[...]), _dot(m_ref[0, r, :], wol_ref[...]),
                   _dot(h.astype(BF16), wb_ref[:, IN_GL:IN_END])))
    for r, (y_a, y_b, gl) in zip(rows, ys):
        gates = 0.5 + 0.5 * jnp.tanh(0.5 * (gl + bg_ref[...]))
        mix = gates[:, :D_MODEL] * y_a + gates[:, D_MODEL:] * y_b
        o_ref[0, r, :] = x_ref[0, r, :] + g1_ref[...] * _dot(mix.astype(BF16), wout_ref[...])


def _merge_call(attn, m, x, mod, g1n, w):
    b, s, _ = x.shape
    tm = min(MERGE_TM, s)
    row = lambda width: pl.BlockSpec((1, tm, width), lambda i, j: (i, j, 0))
    batch_row = lambda i, j: i
    consts = [g1n, w['b_gate'], w['in_b'], w['o_attn'], w['o_lru'], w['out']]
    return pl.pallas_call(
        _merge_body,
        grid=(b, s // tm),
        in_specs=[row(N_HEADS * V_HEAD_DIM), row(LRU_WIDTH), row(D_MODEL), _mod_spec(MOD_SHIFT1, batch_row),
                  _mod_spec(MOD_SCALE1, batch_row), _mod_spec(MOD_GATE1, batch_row)]
        + [_const_spec(a.shape) for a in consts],
        out_specs=row(D_MODEL),
        out_shape=jax.ShapeDtypeStruct((b, s, D_MODEL), F32),
        compiler_params=_params(("parallel", "parallel")),
        name="merge",
    )(attn, m, x, mod, mod, mod, *consts)


def _ffn_body(xp_ref, x_ref, xn_ref, sh_ref, sc_ref, g2_ref, n2_ref, fg_ref, wup_ref, cw_ref, cb_ref,
              wdn_ref, o_ref, f_ref):
    j = pl.program_id(1)
    nj = pl.num_programs(1)
    tm = x_ref.shape[1]
    x = x_ref[0]
    xe = jnp.concatenate([xp_ref[0], x, xn_ref[0]], axis=0)
    hf = _rms(xe, n2_ref[...]) * (1.0 + sc_ref[...]) + sh_ref[...]
    keep_top = (j > 0).astype(F32)
    keep_bot = (j < nj - 1).astype(F32)
    ht = hf[HALO:HALO + tm]
    he = jnp.concatenate([hf[:HALO] * keep_top, ht, hf[HALO + tm:] * keep_bot], axis=0).astype(BF16)
    ht = ht.astype(BF16)
    ext = tm + 2 * HALO
    for ci in range(FFN_DIM // FFN_FC):
        cols = slice(ci * FFN_FC, (ci + 1) * FFN_FC)
        gcols = slice(FFN_DIM + ci * FFN_FC, FFN_DIM + (ci + 1) * FFN_FC)
        a = _dot(he, wup_ref[:, cols])
        cw = cw_ref[:, cols]
        conv = (cb_ref[:, cols] + pltpu.roll(a, 1, 0)[HALO:HALO + tm] * cw[0:1]
                + a[HALO:HALO + tm] * cw[1:2]
                + pltpu.roll(a, ext - 1, 0)[HALO:HALO + tm] * cw[2:3])
        g = _dot(ht, wup_ref[:, gcols])
        f_ref[:, cols] = (conv * jax.nn.sigmoid(conv) * g).astype(BF16)
    y = x + g2_ref[...] * _dot(f_ref[...], wdn_ref[...])
    o_ref[0] = _rms(y, fg_ref[...])


def _ffn_call(x1, mod, w):
    b, s, _ = x1.shape
    tm = min(FFN_TM, s)
    per = tm // HALO
    nblk = s // HALO
    row = pl.BlockSpec((1, tm, D_MODEL), lambda i, j: (i, j, 0))
    prev = pl.BlockSpec((1, HALO, D_MODEL), lambda i, j: (i, jnp.maximum(j * per - 1, 0), 0))
    nxt = pl.BlockSpec((1, HALO, D_MODEL), lambda i, j: (i, jnp.minimum((j + 1) * per, nblk - 1), 0))
    batch_row = lambda i, j: i
    consts = [w['norm2_g'], w['final_g'], w['up'], w['ffn_conv_w'], w['ffn_conv_b'], w['down']]
    return pl.pallas_call(
        _ffn_body,
        grid=(b, s // tm),
        in_specs=[prev, row, nxt, _mod_spec(MOD_SHIFT2, batch_row), _mod_spec(MOD_SCALE2, batch_row),
                  _mod_spec(MOD_GATE2, batch_row)] + [_const_spec(a.shape) for a in consts],
        out_specs=row,
        out_shape=jax.ShapeDtypeStruct((b, s, D_MODEL), F32),
        scratch_shapes=[pltpu.VMEM((tm, FFN_DIM), BF16)],
        compiler_params=_params(("parallel", "parallel")),
        name="ffn",
    )(x1, x1, x1, mod, mod, mod, *consts)


def _rope_tables(n):
    rows = n // GRID_W
    row_ids = jnp.repeat(jnp.arange(rows), GRID_W).astype(F32)
    col_ids = jnp.tile(jnp.arange(GRID_W), rows).astype(F32)
    axis_dim = QK_ROPE_DIM // 2
    inv = 1.0 / (ROPE_BASE ** (jnp.arange(0, axis_dim, 2, dtype=F32) / axis_dim))
    ang = jnp.concatenate([row_ids[:, None] * inv, col_ids[:, None] * inv], axis=-1)
    cos, sin = jnp.cos(ang), jnp.sin(ang)
    ones = lambda w_: jnp.ones((n, w_), F32)
    zeros = lambda w_: jnp.zeros((n, w_), F32)
    tail = HEAD_PAD - QK_DIM
    c = jnp.concatenate([ones(QK_NOPE_DIM), cos, cos, ones(tail)], axis=-1)
    s1 = jnp.concatenate([zeros(QK_NOPE_DIM), -sin, zeros(ROPE_HALF + tail)], axis=-1)
    s2 = jnp.concatenate([zeros(QK_NOPE_DIM + ROPE_HALF), sin, zeros(tail)], axis=-1)
    return c, s1, s2


def _identity_tables(n):
    return (jnp.ones((n, HEAD_PAD), F32), jnp.zeros((n, HEAD_PAD), F32), jnp.zeros((n, HEAD_PAD), F32))


def _prep_weights(w_in, q_norm_g, kv_norm_g, w_uq, w_ukv, w_o_attn, lru_conv_w, lru_conv_b, lru_w_a,
                  lru_b_a, lru_w_x, lru_b_x, lru_lambda, w_o_lru, w_out, b_gate, norm2_g, w_up,
                  ffn_conv_w, ffn_conv_b, w_down, final_g):
    w = {}
    kr = jnp.pad(w_in[:, OFF_KR:OFF_XB], ((0, 0), (QK_NOPE_DIM, HEAD_PAD - QK_DIM)))
    w['in_a'] = jnp.concatenate([w_in[:, :OFF_KR], kr], axis=1).astype(BF16)
    w['in_b'] = w_in[:, OFF_XB:].astype(BF16)
    w['gq'] = q_norm_g[None, :]
    w['gkv'] = kv_norm_g[None, :]
    uq = w_uq.reshape(Q_LORA_RANK, N_HEADS, QK_DIM)
    w['uq'] = jnp.pad(uq, ((0, 0), (0, 0), (0, HEAD_PAD - QK_DIM))).reshape(Q_LORA_RANK, -1).astype(BF16)
    ukv = w_ukv.reshape(KV_LORA_RANK, N_HEADS, QK_NOPE_DIM + V_HEAD_DIM)
    w['uk'] = jnp.pad(ukv[..., :QK_NOPE_DIM],
                      ((0, 0), (0, 0), (0, HEAD_PAD - QK_NOPE_DIM))).reshape(KV_LORA_RANK, -1).astype(BF16)
    w['uv'] = ukv[..., QK_NOPE_DIM:].reshape(KV_LORA_RANK, -1).T.astype(BF16)
    w['conv_w'] = lru_conv_w
    w['conv_b'] = lru_conv_b[None, :]
    gw = 0.5 * jnp.concatenate([lru_w_a[0], lru_w_x[0], lru_w_a[1], lru_w_x[1]], axis=-1)
    gb = 0.5 * jnp.stack([lru_b_a[0], lru_b_x[0], lru_b_a[1], lru_b_x[1]], axis=0)
    gb = gb.reshape(4, LRU_BLOCKS, LRU_BLOCK_W).transpose(1, 0, 2).reshape(LRU_BLOCKS, 1, -1)
    terms = []
    for _ in range(GATE_BIAS_ROWS):
        t = gb * VELTKAMP_8BIT
        hi = t - (t - gb)
        terms.append(hi)
        gb = gb - hi
    bias_rows = jnp.pad(jnp.concatenate(terms, axis=1), ((0, 0), (0, LANES - GATE_BIAS_ROWS), (0, 0)))
    w['gate_w'] = jnp.concatenate([gw, bias_rows], axis=1).astype(BF16)
    w['lam'] = lru_lambda.reshape(2, LRU_BLOCKS, LRU_BLOCK_W).transpose(1, 0, 2)
    w['o_attn'] = w_o_attn.astype(BF16)
    w['o_lru'] = w_o_lru.astype(BF16)
    w['out'] = w_out.astype(BF16)
    w['b_gate'] = b_gate[None, :]
    w['norm2_g'] = norm2_g[None, :]
    w['final_g'] = final_g[None, :]
    w['up'] = w_up.astype(BF16)
    w['ffn_conv_w'] = ffn_conv_w
    w['ffn_conv_b'] = ffn_conv_b[None, :]
    w['down'] = w_down.astype(BF16)
    return w


def kernel(x, c, ctx, c_ctx, w_mod, b_mod, norm1_g, w_in, b_gate, q_norm_g, kv_norm_g, w_uq, w_ukv,
           w_o_attn, lru_conv_w, lru_conv_b, lru_w_a, lru_b_a, lru_w_x, lru_b_x, lru_lambda, w_o_lru,
           w_out, norm2_g, w_up, ffn_conv_w, ffn_conv_b, w_down, final_g):
    assert w_mod.shape[0] == 1, "single-layer block"
    b, s, _ = x.shape
    c_len = ctx.shape[1]
    assert b + 1 <= MOD_ROWS

    w = _prep_weights(w_in[0], q_norm_g[0], kv_norm_g[0], w_uq[0], w_ukv[0], w_o_attn[0], lru_conv_w[0],
                      lru_conv_b[0], lru_w_a[0], lru_b_a[0], lru_w_x[0], lru_b_x[0], lru_lambda[0],
                      w_o_lru[0], w_out[0], b_gate[0], norm2_g[0], w_up[0], ffn_conv_w[0],
                      ffn_conv_b[0], w_down[0], final_g)
    g1n = norm1_g[0][None, :]

    cc = jnp.concatenate([c, c_ctx[None, :], jnp.zeros((MOD_ROWS - b - 1, D_MODEL), F32)], axis=0)
    mod = _mod_call(cc, w_mod, b_mod).reshape(MOD_ROWS, 6, 1, D_MODEL)

    k_c, vt_c, xc_c = _proj_call(False, ctx, mod, b, g1n, w, _identity_tables(c_len))
    states = _lru_call(False, xc_c, w)

    k_l, vt_l, xc_l, q_l, gy_l = _proj_call(True, x, mod, None, g1n, w, _rope_tables(s))
    m = _lru_call(True, xc_l, w, yb=gy_l, h0=states)
    attn = _attn_call(q_l, k_l, k_c, vt_l, vt_c)
    x1 = _merge_call(attn, m, x, mod, g1n, w)
    return _ffn_call(x1, mod, w)
```

```python
import functools

import jax
import jax.numpy as jnp
from jax import lax
from jax.experimental import pallas as pl
from jax.experimental.pallas import tpu as pltpu

F32 = jnp.float32
BF16 = jnp.bfloat16

D_MODEL = 1024
GRID_W = 64
N_HEADS = 8
QK_NOPE_DIM = 64
QK_ROPE_DIM = 32
ROPE_HALF = QK_ROPE_DIM // 2
V_HEAD_DIM = 64
QK_DIM = QK_NOPE_DIM + QK_ROPE_DIM
HEAD_PAD = 128
Q_LORA_RANK = 384
KV_LORA_RANK = 256
ROPE_BASE = 10000.0
LRU_WIDTH = 1280
LRU_BLOCKS = 10
LRU_BLOCK_W = LRU_WIDTH // LRU_BLOCKS
LRU_C = 8.0
FFN_DIM = 2816
EPS = 1e-6
TINY = 1e-30
LOG2_E = 1.4426950408889634
OFF_KV = Q_LORA_RANK
OFF_KR = OFF_KV + KV_LORA_RANK
OFF_XB = OFF_KR + QK_ROPE_DIM
IN_Q = 0
IN_KV = IN_Q + Q_LORA_RANK
IN_KR = IN_KV + KV_LORA_RANK
IN_A_END = IN_KR + HEAD_PAD
IN_XB = 0
IN_YB = IN_XB + LRU_WIDTH
IN_GL = IN_YB + LRU_WIDTH
IN_END = IN_GL + 2 * D_MODEL

SUBLANES = 8
LANES = 128
HALO = SUBLANES
VMEM_LIMIT = 56 * 1024 * 1024

MOD_ROWS = 8
MOD_SHIFT1, MOD_SCALE1, MOD_GATE1, MOD_SHIFT2, MOD_SCALE2, MOD_GATE2 = range(6)
MOD_TN = 1536
PROJ_TM = 512
LRU_TC = 256
LRU_UNROLL = 16
LRU_FIX_ROWS = 64
GATE_BIAS_ROWS = 3
VELTKAMP_8BIT = 65537.0
VT_ROWS = 80
ATTN_TQ = 4096
ATTN_SUB = 256
ATTN_TK = 256
MERGE_TM = 1024
MERGE_SUBTILES = 4
FFN_TM = 1024
FFN_FC = 256


def _dot(a, b):
    return jnp.dot(a, b, preferred_element_type=F32)


def _dot_nt(a, b):
    return lax.dot_general(a, b, (((1,), (1,)), ((), ())), preferred_element_type=F32)


def _rms(x, g):
    return x * lax.rsqrt(jnp.mean(x * x, axis=-1, keepdims=True) + EPS) * g


def _const_spec(shape):
    nd = len(shape)
    return pl.BlockSpec(shape, lambda *_: (0,) * nd, pipeline_mode=pl.Buffered(1))


def _mod_spec(component, row_of):
    return pl.BlockSpec((None, None, 1, D_MODEL), lambda *idx: (row_of(*idx), component, 0, 0))


def _params(sem):
    return pltpu.CompilerParams(dimension_semantics=sem, vmem_limit_bytes=VMEM_LIMIT)


def _mod_body(c_ref, w_ref, b_ref, o_ref):
    c = c_ref[...]
    s = c * jax.nn.sigmoid(c)
    o_ref[...] = _dot(s.astype(BF16), w_ref[...].astype(BF16)) + b_ref[...]


def _mod_call(cc, w_mod, b_mod):
    n = w_mod.shape[-1]
    return pl.pallas_call(
        _mod_body,
        grid=(n // MOD_TN,),
        in_specs=[
            pl.BlockSpec((MOD_ROWS, D_MODEL), lambda j: (0, 0)),
            pl.BlockSpec((None, D_MODEL, MOD_TN), lambda j: (0, 0, j)),
            pl.BlockSpec((1, MOD_TN), lambda j: (0, j)),
        ],
        out_specs=pl.BlockSpec((MOD_ROWS, MOD_TN), lambda j: (0, j)),
        out_shape=jax.ShapeDtypeStruct((MOD_ROWS, n), F32),
        compiler_params=_params(("arbitrary",)),
        name="mod",
    )(cc, w_mod, b_mod)


def _rope128(t, c, s1, s2):
    return (t * c + pltpu.roll(t, HEAD_PAD - ROPE_HALF, 1) * s1
            + pltpu.roll(t, ROPE_HALF, 1) * s2)


def _proj_body(latent, xp_ref, x_ref, xn_ref, sh_ref, sc_ref, g1_ref, wa_ref, wb_ref, gkv_ref, wuk_ref,
               wuv_ref, cw_ref, cb_ref, c_ref, s1_ref, s2_ref, *rest):
    if latent:
        gq_ref, wuq_ref, k_ref, v_ref, xc_ref, q_ref, yb_ref = rest
    else:
        k_ref, v_ref, xc_ref = rest
    i = pl.program_id(0)
    tm = x_ref.shape[1]
    xe = jnp.concatenate([xp_ref[0], x_ref[0], xn_ref[0]], axis=0)
    hf = _rms(xe, g1_ref[...]) * (1.0 + sc_ref[...]) + sh_ref[...]
    keep_top = (i > 0).astype(F32)
    keep_bot = (i < pl.num_programs(0) - 1).astype(F32)
    h = hf[HALO:HALO + tm]
    he = jnp.concatenate([hf[:HALO] * keep_top, h, hf[HALO + tm:] * keep_bot], axis=0).astype(BF16)
    hb = h.astype(BF16)
    c, s1, s2 = c_ref[...], s1_ref[...], s2_ref[...]

    if latent:
        lat = _dot(hb, wa_ref[...])
        q_lat = lat[:, :IN_KV - IN_Q]
        lat = lat[:, IN_KV - IN_Q:]
    else:
        lat = _dot(hb, wa_ref[:, IN_KV:IN_A_END])
    kv_lat, kr_raw = lat[:, :IN_KR - IN_KV], lat[:, IN_KR - IN_KV:]
    xb = _dot(he, wb_ref[:, IN_XB:IN_YB])
    ext = tm + 2 * HALO
    cw = cw_ref[...]
    xc_ref[0] = (cb_ref[...] + pltpu.roll(xb, 2, 0)[HALO:HALO + tm] * cw[0:1]
                 + pltpu.roll(xb, 1, 0)[HALO:HALO + tm] * cw[1:2]
                 + xb[HALO:HALO + tm] * cw[2:3]
                 + pltpu.roll(xb, ext - 1, 0)[HALO:HALO + tm] * cw[3:4])
    kvn = _rms(kv_lat, gkv_ref[...]).astype(BF16)
    kk = _dot(kvn, wuk_ref[...])
    vt = _dot_nt(wuv_ref[...], kvn)
    ones = jnp.ones((VT_ROWS - V_HEAD_DIM, vt.shape[1]), F32)
    v_ref[0] = jnp.concatenate([piece for hd in range(N_HEADS)
                                for piece in (vt[hd * V_HEAD_DIM:(hd + 1) * V_HEAD_DIM], ones)],
                               axis=0).astype(BF16)
    if latent:
        qn = _rms(q_lat, gq_ref[...]).astype(BF16)
        qq = _dot(qn, wuq_ref[...])
        yb_ref[0] = _gelu_tanh(_dot(hb, wb_ref[:, IN_YB:IN_GL]))
    kr = _rope128(kr_raw, c, s1, s2)
    for hd in range(N_HEADS):
        sl = slice(hd * HEAD_PAD, (hd + 1) * HEAD_PAD)
        k_ref[0, :, sl] = (kk[:, sl] + kr).astype(BF16)
    if latent:
        scale = QK_DIM ** -0.5 * LOG2_E
        for hd in range(N_HEADS):
            sl = slice(hd * HEAD_PAD, (hd + 1) * HEAD_PAD)
            q_ref[0, :, sl] = (_rope128(qq[:, sl], c, s1, s2) * scale).astype(BF16)


def _proj_call(latent, x, mod, mod_row, g1, w, tabs):
    b, s, _ = x.shape
    tm = min(PROJ_TM, s)
    row = lambda width: pl.BlockSpec((1, tm, width), lambda i, j: (j, i, 0))
    row_of = (lambda i, j: j) if mod_row is None else (lambda i, j: mod_row)
    tab = pl.BlockSpec((tm, HEAD_PAD), lambda i, j: (i, 0))
    per = tm // HALO
    nblk = s // HALO
    prev = pl.BlockSpec((1, HALO, D_MODEL), lambda i, j: (j, jnp.maximum(i * per - 1, 0), 0))
    nxt = pl.BlockSpec((1, HALO, D_MODEL), lambda i, j: (j, jnp.minimum((i + 1) * per, nblk - 1), 0))
    ins = [x, x, x, mod, mod, g1, w['in_a'], w['in_b'], w['gkv'], w['uk'], w['uv'], w['conv_w'], w['conv_b'], *tabs]
    in_specs = ([prev, row(D_MODEL), nxt, _mod_spec(MOD_SHIFT1, row_of), _mod_spec(MOD_SCALE1, row_of)]
                + [_const_spec(a.shape) for a in ins[5:13]]
                + [tab] * 3)
    widths = [N_HEADS * HEAD_PAD, None, LRU_WIDTH]
    dtypes = [BF16, BF16, F32]
    if latent:
        extra = [w['gq'], w['uq']]
        ins += extra
        in_specs += [_const_spec(a.shape) for a in extra]
        widths += [N_HEADS * HEAD_PAD, LRU_WIDTH]
        dtypes += [BF16, F32]
    hv = N_HEADS * VT_ROWS
    out_specs = [row(wd) for wd in widths if wd is not None]
    out_shape = [jax.ShapeDtypeStruct((b, s, wd), dt) for wd, dt in zip(widths, dtypes) if wd is not None]
    out_specs.insert(1, pl.BlockSpec((1, hv, tm), lambda i, j: (j, 0, i)))
    out_shape.insert(1, jax.ShapeDtypeStruct((b, hv, s), BF16))
    return pl.pallas_call(
        functools.partial(_proj_body, latent),
        grid=(s // tm, b),
        in_specs=in_specs,
        out_specs=out_specs,
        out_shape=out_shape,
        compiler_params=_params(("parallel", "parallel")),
        name="proj_latent" if latent else "proj_ctx",
    )(*ins)


def _gelu_tanh(x):
    return 0.5 * x * (1.0 + jnp.tanh(0.7978845608028654 * (x + 0.044715 * (x * x * x))))


def _lru_body(latent, xc_ref, wg_ref, lam_ref, *rest):
    for kb in range(xc_ref.shape[2] // LANES):
        _lru_block(latent, kb, xc_ref, wg_ref, lam_ref, *rest)


def _lru_block(latent, kb, xc_ref, wg_ref, lam_ref, *rest):
    if latent:
        yb_ref, h0_ref, out_ref, gates, scans = rest
    else:
        st_ref, gates, scans = rest
    s = xc_ref.shape[1]
    lanes = slice(kb * LANES, (kb + 1) * LANES)
    af, uf, ab, ub = (gates.at[pl.ds(k * s, s)] for k in range(4))
    hfl, pfl, hbl, pbl = (scans.at[pl.ds(k * s, s)] for k in range(4))
    seg = s // SUBLANES
    tc = min(LRU_TC, s)
    nc = s // tc
    piece = min(tc, seg)

    wg = wg_ref[kb]
    lam = lam_ref[kb]
    hcsp = (0.5 * LRU_C) * (jnp.maximum(-lam, 0.0) + jnp.log1p(jnp.exp(-jnp.abs(lam))))
    bias_lhs = jnp.where(lax.broadcasted_iota(jnp.int32, (tc, LANES), 1) < GATE_BIAS_ROWS, 1.0, 0.0).astype(BF16)

    def interleaved(ci):
        pieces = []
        for k in range(tc // piece):
            t0 = ci * tc + k * piece
            j = t0 // seg
            pieces.append((slice(k * piece, (k + 1) * piece),
                           pl.ds((t0 - j * seg) * SUBLANES + j, piece, stride=SUBLANES)))
        return pieces

    def gate_chunk(ci, carry):
        r0 = pl.multiple_of(ci * tc, tc)
        xc = xc_ref[0, pl.ds(r0, tc), lanes]
        t = jnp.tanh(_dot(jnp.concatenate([xc.astype(BF16), bias_lhs], axis=1), wg))
        hx = 0.5 * xc
        for d, (a_ref, u_ref) in enumerate(((af, uf), (ab, ub))):
            t_r = t[:, (2 * d) * LANES:(2 * d + 1) * LANES]
            t_i = t[:, (2 * d + 1) * LANES:(2 * d + 2) * LANES]
            neg_log_a = hcsp[d:d + 1] + hcsp[d:d + 1] * t_r
            a = jnp.exp2(neg_log_a * (-LOG2_E))
            y = (1.0 - a) * (1.0 + a)
            mult = y * lax.rsqrt(jnp.maximum(y, TINY))
            u = (mult * hx) * (1.0 + t_i)
            for rows, rd in interleaved(ci):
                a_ref[rd, :] = a[rows]
                u_ref[rd, :] = u[rows]
        return carry

    lax.fori_loop(0, nc, gate_chunk, 0, unroll=min(16, nc))

    def scan_step(i, carry):
        hf, pf, hb, pb = carry
        fwd = pl.ds(pl.multiple_of(i * SUBLANES, SUBLANES), SUBLANES)
        bwd = pl.ds(pl.multiple_of((seg - 1 - i) * SUBLANES, SUBLANES), SUBLANES)
        a = af[fwd, :]
        hf = a * hf + uf[fwd, :]
        pf = a * pf
        hfl[fwd, :] = hf
        pfl[fwd, :] = pf
        a = ab[bwd, :]
        hb = a * hb + ub[bwd, :]
        pb = a * pb
        hbl[bwd, :] = hb
        pbl[bwd, :] = pb
        return hf, pf, hb, pb

    zeros = jnp.zeros((SUBLANES, LANES), F32)
    ones = jnp.ones((SUBLANES, LANES), F32)
    hf, pf, hb, pb = lax.fori_loop(0, seg, scan_step, (zeros, ones, zeros, ones), unroll=LRU_UNROLL)

    if latent:
        h0 = h0_ref[0, :, lanes]
        cf, cbk = h0[0:1], h0[1:2]
    else:
        cf = cbk = jnp.zeros((1, LANES), F32)
    cfs = []
    for j in range(SUBLANES):
        cfs.append(cf)
        cf = hf[j:j + 1] + pf[j:j + 1] * cf
    cbs = [None] * SUBLANES
    for j in reversed(range(SUBLANES)):
        cbs[j] = cbk
        cbk = hb[j:j + 1] + pb[j:j + 1] * cbk

    if latent:
        fix_rows = min(LRU_FIX_ROWS, s)
        cf_all = jnp.tile(jnp.concatenate(cfs, axis=0), (fix_rows // SUBLANES, 1))
        cb_all = jnp.tile(jnp.concatenate(cbs, axis=0), (fix_rows // SUBLANES, 1))

        def fix_chunk(ci, carry):
            rows = pl.ds(pl.multiple_of(ci * fix_rows, fix_rows), fix_rows)
            af[rows, :] = (hfl[rows, :] + pfl[rows, :] * cf_all) + (hbl[rows, :] + pbl[rows, :] * cb_all)
            return carry

        lax.fori_loop(0, s // fix_rows, fix_chunk, 0, unroll=2)

        def out_chunk(ci, carry):
            r0 = pl.multiple_of(ci * tc, tc)
            hsum = jnp.concatenate([af[rd, :] for _, rd in interleaved(ci)], axis=0)
            out_ref[0, pl.ds(r0, tc), lanes] = (hsum * yb_ref[0, pl.ds(r0, tc), lanes]).astype(BF16)
            return carry

        lax.fori_loop(0, nc, out_chunk, 0)
    else:
        st_ref[0, :, lanes] = jnp.concatenate([cf, cbk], axis=0)


def _lru_call(latent, xc, w, yb=None, h0=None):
    b, s, _ = xc.shape
    nb = 1 if latent else LRU_BLOCKS
    seq = pl.BlockSpec((1, s, nb * LANES), lambda i, j: (i, 0, j))
    st = pl.BlockSpec((1, 2, nb * LANES), lambda i, j: (i, 0, j))
    in_specs = [
        seq,
        pl.BlockSpec((nb, 2 * LANES, 4 * LANES), lambda i, j: (j, 0, 0)),
        pl.BlockSpec((nb, 2, LANES), lambda i, j: (j, 0, 0)),
    ]
    ins = [xc, w['gate_w'], w['lam']]
    scratch = [pltpu.VMEM((4 * s, LANES), F32)] * 2
    if latent:
        ins += [yb, h0]
        in_specs += [seq, st]
        out_specs = seq
        out_shape = jax.ShapeDtypeStruct((b, s, LRU_WIDTH), BF16)
    else:
        out_specs = st
        out_shape = jax.ShapeDtypeStruct((b, 2, LRU_WIDTH), F32)
    return pl.pallas_call(
        functools.partial(_lru_body, latent),
        grid=(b, LRU_BLOCKS // nb),
        in_specs=in_specs,
        out_specs=out_specs,
        out_shape=out_shape,
        scratch_shapes=scratch,
        compiler_params=_params(("parallel", "parallel")),
        name="lru_latent" if latent else "lru_ctx",
    )(*ins)


def _col_groups(x, op):
    rows, cols = x.shape
    return op(x.reshape(rows // SUBLANES, SUBLANES, cols), axis=0)


def _attn_body(q_ref, kl_ref, kc_ref, vl_ref, vc_ref, o_ref, s0_ref, s1_ref, p0_ref, p1_ref, o0_ref):
    s_len = kl_ref.shape[1]
    c_len = kc_ref.shape[1]
    tk = min(ATTN_TK, s_len)
    sub = s0_ref.shape[1]
    n_sub = q_ref.shape[1] // sub
    chunks = [(kc_ref, 0, c_len, 0)] + [(kl_ref, ki * tk, tk, c_len + ki * tk) for ki in range(s_len // tk)]
    s_refs, p_refs = (s0_ref, s1_ref), (p0_ref, p1_ref)

    def qrows(qt):
        return pl.ds(pl.multiple_of(qt * sub, sub), sub)

    def score_pass(qt, hd):
        lanes = slice(hd * HEAD_PAD, (hd + 1) * HEAD_PAD)
        q = q_ref[0, qrows(qt), lanes]
        m8 = None
        for k_ref, r0, rows, off in chunks:
            sk = _dot_nt(k_ref[0, r0:r0 + rows, lanes], q)
            s_refs[hd][off:off + rows, :] = sk
            g = _col_groups(sk, jnp.max)
            m8 = g if m8 is None else jnp.maximum(m8, g)
        return jnp.max(m8, axis=0, keepdims=True)

    def exp_pass(hd, m):
        for _, _, rows, off in chunks:
            p_refs[hd][off:off + rows, :] = jnp.exp2(s_refs[hd][off:off + rows, :] - m).astype(BF16)

    def value_pass(hd):
        vrows = slice(hd * VT_ROWS, (hd + 1) * VT_ROWS)
        acc = (_dot(vc_ref[0, vrows, :], p_refs[hd][0:c_len, :])
               + _dot(vl_ref[0, vrows, :], p_refs[hd][c_len:c_len + s_len, :]))
        return acc[:V_HEAD_DIM] / acc[V_HEAD_DIM:V_HEAD_DIM + 1]

    def store(qt, o0, o1):
        o_ref[0, qrows(qt), :] = jnp.concatenate([o0, o1], axis=0).T.astype(BF16)

    def finish(hd, m):
        exp_pass(hd, m)
        return value_pass(hd)

    def head1_trip(qt, m_prev):
        o0_ref[...] = finish(0, m_prev)
        return score_pass(qt, 1)

    def head0_trip(qt, m_prev):
        store(qt - 1, o0_ref[...], finish(1, m_prev))
        return score_pass(qt, 0)

    def trip(u, m_prev):
        return lax.cond(u % 2 == 1, head1_trip, head0_trip, u // 2, m_prev)

    m = lax.fori_loop(1, 2 * n_sub, trip, score_pass(0, 0))
    store(n_sub - 1, o0_ref[...], finish(1, m))


def _attn_call(q, k_l, k_c, vt_l, vt_c):
    b, s, _ = q.shape
    c_len = k_c.shape[1]
    tq = min(ATTN_TQ, s)
    sub = min(ATTN_SUB, tq)
    pairs = N_HEADS // 2
    return pl.pallas_call(
        _attn_body,
        grid=(b, pairs, s // tq),
        in_specs=[
            pl.BlockSpec((1, tq, 2 * HEAD_PAD), lambda i, j, t: (i, t, j)),
            pl.BlockSpec((1, s, 2 * HEAD_PAD), lambda i, j, t: (i, 0, j)),
            pl.BlockSpec((1, c_len, 2 * HEAD_PAD), lambda i, j, t: (i, 0, j)),
            pl.BlockSpec((1, 2 * VT_ROWS, s), lambda i, j, t: (i, j, 0)),
            pl.BlockSpec((1, 2 * VT_ROWS, c_len), lambda i, j, t: (i, j, 0)),
        ],
        out_specs=pl.BlockSpec((1, tq, 2 * V_HEAD_DIM), lambda i, j, t: (i, t, j)),
        out_shape=jax.ShapeDtypeStruct((b, s, N_HEADS * V_HEAD_DIM), BF16),
        scratch_shapes=([pltpu.VMEM((c_len + s, sub), F32)] * 2 + [pltpu.VMEM((c_len + s, sub), BF16)] * 2
                        + [pltpu.VMEM((V_HEAD_DIM, sub), F32)]),
        compiler_params=_params(("parallel", "parallel", "parallel")),
        name="attn",
    )(q, k_l, k_c, vt_l, vt_c)


def _merge_body(attn_ref, m_ref, x_ref, sh_ref, sc_ref, g1_ref, n1_ref, bg_ref, wb_ref, woa_ref, wol_ref,
                wout_ref, o_ref):
    tm = x_ref.shape[1]
    sub = tm // MERGE_SUBTILES
    rows = [slice(i * sub, (i + 1) * sub) for i in range(MERGE_SUBTILES)]
    ys = []
    for r in rows:
        h = _rms(x_ref[0, r, :], n1_ref[...]) * (1.0 + sc_ref[...]) + sh_ref[...]
        ys.append((_dot(attn_ref[0, r, :], woa_ref[...]), _dot(m_ref[0, r, :], wol_ref[...]),
                   _dot(h.astype(BF16), wb_ref[:, IN_GL:IN_END])))
    for r, (y_a, y_b, gl) in zip(rows, ys):
        gates = 0.5 + 0.5 * jnp.tanh(0.5 * (gl + bg_ref[...]))
        mix = gates[:, :D_MODEL] * y_a + gates[:, D_MODEL:] * y_b
        o_ref[0, r, :] = x_ref[0, r, :] + g1_ref[...] * _dot(mix.astype(BF16), wout_ref[...])


def _merge_call(attn, m, x, mod, g1n, w):
    b, s, _ = x.shape
    tm = min(MERGE_TM, s)
    row = lambda width: pl.BlockSpec((1, tm, width), lambda i, j: (i, j, 0))
    batch_row = lambda i, j: i
    consts = [g1n, w['b_gate'], w['in_b'], w['o_attn'], w['o_lru'], w['out']]
    return pl.pallas_call(
        _merge_body,
        grid=(b, s // tm),
        in_specs=[row(N_HEADS * V_HEAD_DIM), row(LRU_WIDTH), row(D_MODEL), _mod_spec(MOD_SHIFT1, batch_row),
                  _mod_spec(MOD_SCALE1, batch_row), _mod_spec(MOD_GATE1, batch_row)]
        + [_const_spec(a.shape) for a in consts],
        out_specs=row(D_MODEL),
        out_shape=jax.ShapeDtypeStruct((b, s, D_MODEL), F32),
        compiler_params=_params(("parallel", "parallel")),
        name="merge",
    )(attn, m, x, mod, mod, mod, *consts)


def _ffn_body(xp_ref, x_ref, xn_ref, sh_ref, sc_ref, g2_ref, n2_ref, fg_ref, wup_ref, cw_ref, cb_ref,
              wdn_ref, o_ref, f_ref):
    j = pl.program_id(1)
    nj = pl.num_programs(1)
    tm = x_ref.shape[1]
    x = x_ref[0]
    xe = jnp.concatenate([xp_ref[0], x, xn_ref[0]], axis=0)
    hf = _rms(xe, n2_ref[...]) * (1.0 + sc_ref[...]) + sh_ref[...]
    keep_top = (j > 0).astype(F32)
    keep_bot = (j < nj - 1).astype(F32)
    ht = hf[HALO:HALO + tm]
    he = jnp.concatenate([hf[:HALO] * keep_top, ht, hf[HALO + tm:] * keep_bot], axis=0).astype(BF16)
    ht = ht.astype(BF16)
    ext = tm + 2 * HALO
    for ci in range(FFN_DIM // FFN_FC):
        cols = slice(ci * FFN_FC, (ci + 1) * FFN_FC)
        gcols = slice(FFN_DIM + ci * FFN_FC, FFN_DIM + (ci + 1) * FFN_FC)
        a = _dot(he, wup_ref[:, cols])
        cw = cw_ref[:, cols]
        conv = (cb_ref[:, cols] + pltpu.roll(a, 1, 0)[HALO:HALO + tm] * cw[0:1]
                + a[HALO:HALO + tm] * cw[1:2]
                + pltpu.roll(a, ext - 1, 0)[HALO:HALO + tm] * cw[2:3])
        g = _dot(ht, wup_ref[:, gcols])
        f_ref[:, cols] = (conv * jax.nn.sigmoid(conv) * g).astype(BF16)
    y = x + g2_ref[...] * _dot(f_ref[...], wdn_ref[...])
    o_ref[0] = _rms(y, fg_ref[...])


def _ffn_call(x1, mod, w):
    b, s, _ = x1.shape
    tm = min(FFN_TM, s)
    per = tm // HALO
    nblk = s // HALO
    row = pl.BlockSpec((1, tm, D_MODEL), lambda i, j: (i, j, 0))
    prev = pl.BlockSpec((1, HALO, D_MODEL), lambda i, j: (i, jnp.maximum(j * per - 1, 0), 0))
    nxt = pl.BlockSpec((1, HALO, D_MODEL), lambda i, j: (i, jnp.minimum((j + 1) * per, nblk - 1), 0))
    batch_row = lambda i, j: i
    consts = [w['norm2_g'], w['final_g'], w['up'], w['ffn_conv_w'], w['ffn_conv_b'], w['down']]
    return pl.pallas_call(
        _ffn_body,
        grid=(b, s // tm),
        in_specs=[prev, row, nxt, _mod_spec(MOD_SHIFT2, batch_row), _mod_spec(MOD_SCALE2, batch_row),
                  _mod_spec(MOD_GATE2, batch_row)] + [_const_spec(a.shape) for a in consts],
        out_specs=row,
        out_shape=jax.ShapeDtypeStruct((b, s, D_MODEL), F32),
        scratch_shapes=[pltpu.VMEM((tm, FFN_DIM), BF16)],
        compiler_params=_params(("parallel", "parallel")),
        name="ffn",
    )(x1, x1, x1, mod, mod, mod, *consts)


def _rope_tables(n):
    rows = n // GRID_W
    row_ids = jnp.repeat(jnp.arange(rows), GRID_W).astype(F32)
    col_ids = jnp.tile(jnp.arange(GRID_W), rows).astype(F32)
    axis_dim = QK_ROPE_DIM // 2
    inv = 1.0 / (ROPE_BASE ** (jnp.arange(0, axis_dim, 2, dtype=F32) / axis_dim))
    ang = jnp.concatenate([row_ids[:, None] * inv, col_ids[:, None] * inv], axis=-1)
    cos, sin = jnp.cos(ang), jnp.sin(ang)
    ones = lambda w_: jnp.ones((n, w_), F32)
    zeros = lambda w_: jnp.zeros((n, w_), F32)
    tail = HEAD_PAD - QK_DIM
    c = jnp.concatenate([ones(QK_NOPE_DIM), cos, cos, ones(tail)], axis=-1)
    s1 = jnp.concatenate([zeros(QK_NOPE_DIM), -sin, zeros(ROPE_HALF + tail)], axis=-1)
    s2 = jnp.concatenate([zeros(QK_NOPE_DIM + ROPE_HALF), sin, zeros(tail)], axis=-1)
    return c, s1, s2


def _identity_tables(n):
    return (jnp.ones((n, HEAD_PAD), F32), jnp.zeros((n, HEAD_PAD), F32), jnp.zeros((n, HEAD_PAD), F32))


def _prep_weights(w_in, q_norm_g, kv_norm_g, w_uq, w_ukv, w_o_attn, lru_conv_w, lru_conv_b, lru_w_a,
                  lru_b_a, lru_w_x, lru_b_x, lru_lambda, w_o_lru, w_out, b_gate, norm2_g, w_up,
                  ffn_conv_w, ffn_conv_b, w_down, final_g):
    w = {}
    kr = jnp.pad(w_in[:, OFF_KR:OFF_XB], ((0, 0), (QK_NOPE_DIM, HEAD_PAD - QK_DIM)))
    w['in_a'] = jnp.concatenate([w_in[:, :OFF_KR], kr], axis=1).astype(BF16)
    w['in_b'] = w_in[:, OFF_XB:].astype(BF16)
    w['gq'] = q_norm_g[None, :]
    w['gkv'] = kv_norm_g[None, :]
    uq = w_uq.reshape(Q_LORA_RANK, N_HEADS, QK_DIM)
    w['uq'] = jnp.pad(uq, ((0, 0), (0, 0), (0, HEAD_PAD - QK_DIM))).reshape(Q_LORA_RANK, -1).astype(BF16)
    ukv = w_ukv.reshape(KV_LORA_RANK, N_HEADS, QK_NOPE_DIM + V_HEAD_DIM)
    w['uk'] = jnp.pad(ukv[..., :QK_NOPE_DIM],
                      ((0, 0), (0, 0), (0, HEAD_PAD - QK_NOPE_DIM))).reshape(KV_LORA_RANK, -1).astype(BF16)
    w['uv'] = ukv[..., QK_NOPE_DIM:].reshape(KV_LORA_RANK, -1).T.astype(BF16)
    w['conv_w'] = lru_conv_w
    w['conv_b'] = lru_conv_b[None, :]
    gw = 0.5 * jnp.concatenate([lru_w_a[0], lru_w_x[0], lru_w_a[1], lru_w_x[1]], axis=-1)
    gb = 0.5 * jnp.stack([lru_b_a[0], lru_b_x[0], lru_b_a[1], lru_b_x[1]], axis=0)
    gb = gb.reshape(4, LRU_BLOCKS, LRU_BLOCK_W).transpose(1, 0, 2).reshape(LRU_BLOCKS, 1, -1)
    terms = []
    for _ in range(GATE_BIAS_ROWS):
        t = gb * VELTKAMP_8BIT
        hi = t - (t - gb)
        terms.append(hi)
        gb = gb - hi
    bias_rows = jnp.pad(jnp.concatenate(terms, axis=1), ((0, 0), (0, LANES - GATE_BIAS_ROWS), (0, 0)))
    w['gate_w'] = jnp.concatenate([gw, bias_rows], axis=1).astype(BF16)
    w['lam'] = lru_lambda.reshape(2, LRU_BLOCKS, LRU_BLOCK_W).transpose(1, 0, 2)
    w['o_attn'] = w_o_attn.astype(BF16)
    w['o_lru'] = w_o_lru.astype(BF16)
    w['out'] = w_out.astype(BF16)
    w['b_gate'] = b_gate[None, :]
    w['norm2_g'] = norm2_g[None, :]
    w['final_g'] = final_g[None, :]
    w['up'] = w_up.astype(BF16)
    w['ffn_conv_w'] = ffn_conv_w
    w['ffn_conv_b'] = ffn_conv_b[None, :]
    w['down'] = w_down.astype(BF16)
    return w


def kernel(x, c, ctx, c_ctx, w_mod, b_mod, norm1_g, w_in, b_gate, q_norm_g, kv_norm_g, w_uq, w_ukv,
           w_o_attn, lru_conv_w, lru_conv_b, lru_w_a, lru_b_a, lru_w_x, lru_b_x, lru_lambda, w_o_lru,
           w_out, norm2_g, w_up, ffn_conv_w, ffn_conv_b, w_down, final_g):
    assert w_mod.shape[0] == 1, "single-layer block"
    b, s, _ = x.shape
    c_len = ctx.shape[1]
    assert b + 1 <= MOD_ROWS

    w = _prep_weights(w_in[0], q_norm_g[0], kv_norm_g[0], w_uq[0], w_ukv[0], w_o_attn[0], lru_conv_w[0],
                      lru_conv_b[0], lru_w_a[0], lru_b_a[0], lru_w_x[0], lru_b_x[0], lru_lambda[0],
                      w_o_lru[0], w_out[0], b_gate[0], norm2_g[0], w_up[0], ffn_conv_w[0],
                      ffn_conv_b[0], w_down[0], final_g)
    g1n = norm1_g[0][None, :]

    cc = jnp.concatenate([c, c_ctx[None, :], jnp.zeros((MOD_ROWS - b - 1, D_MODEL), F32)], axis=0)
    mod = _mod_call(cc, w_mod, b_mod).reshape(MOD_ROWS, 6, 1, D_MODEL)

    k_c, vt_c, xc_c = _proj_call(False, ctx, mod, b, g1n, w, _identity_tables(c_len))
    states = _lru_call(False, xc_c, w)

    k_l, vt_l, xc_l, q_l, gy_l = _proj_call(True, x, mod, None, g1n, w, _rope_tables(s))
    m = _lru_call(True, xc_l, w, yb=gy_l, h0=states)
    attn = _attn_call(q_l, k_l, k_c, vt_l, vt_c)
    x1 = _merge_call(attn, m, x, mod, g1n, w)
    return _ffn_call(x1, mod, w)
```

```python
import functools

import jax
import jax.numpy as jnp
from jax import lax
from jax.experimental import pallas as pl
from jax.experimental.pallas import tpu as pltpu

F32 = jnp.float32
BF16 = jnp.bfloat16

D_MODEL = 1024
GRID_W = 64
N_HEADS = 8
QK_NOPE_DIM = 64
QK_ROPE_DIM = 32
ROPE_HALF = QK_ROPE_DIM // 2
V_HEAD_DIM = 64
QK_DIM = QK_NOPE_DIM + QK_ROPE_DIM
HEAD_PAD = 128
Q_LORA_RANK = 384
KV_LORA_RANK = 256
ROPE_BASE = 10000.0
LRU_WIDTH = 1280
LRU_BLOCKS = 10
LRU_BLOCK_W = LRU_WIDTH // LRU_BLOCKS
LRU_C = 8.0
FFN_DIM = 2816
EPS = 1e-6
TINY = 1e-30
LOG2_E = 1.4426950408889634
OFF_KV = Q_LORA_RANK
OFF_KR = OFF_KV + KV_LORA_RANK
OFF_XB = OFF_KR + QK_ROPE_DIM
IN_Q = 0
IN_KV = IN_Q + Q_LORA_RANK
IN_KR = IN_KV + KV_LORA_RANK
IN_A_END = IN_KR + HEAD_PAD
IN_XB = 0
IN_YB = IN_XB + LRU_WIDTH
IN_GL = IN_YB + LRU_WIDTH
IN_END = IN_GL + 2 * D_MODEL

SUBLANES = 8
LANES = 128
HALO = SUBLANES
VMEM_LIMIT = 56 * 1024 * 1024

MOD_ROWS = 8
MOD_SHIFT1, MOD_SCALE1, MOD_GATE1, MOD_SHIFT2, MOD_SCALE2, MOD_GATE2 = range(6)
MOD_TN = 1536
PROJ_TM = 512
LRU_TC = 256
LRU_UNROLL = 32
LRU_FIX_ROWS = 64
GATE_BIAS_ROWS = 3
VELTKAMP_8BIT = 65537.0
VT_ROWS = 80
ATTN_TQ = 4096
ATTN_SUB = 256
ATTN_TK = 256
MERGE_TM = 1024
MERGE_SUBTILES = 4
FFN_TM = 1024
FFN_FC = 256


def _dot(a, b):
    return jnp.dot(a, b, preferred_element_type=F32)


def _dot_nt(a, b):
    return lax.dot_general(a, b, (((1,), (1,)), ((), ())), preferred_element_type=F32)


def _rms(x, g):
    return x * lax.rsqrt(jnp.mean(x * x, axis=-1, keepdims=True) + EPS) * g


def _const_spec(shape):
    nd = len(shape)
    return pl.BlockSpec(shape, lambda *_: (0,) * nd, pipeline_mode=pl.Buffered(1))


def _mod_spec(component, row_of):
    return pl.BlockSpec((None, None, 1, D_MODEL), lambda *idx: (row_of(*idx), component, 0, 0))


def _params(sem):
    return pltpu.CompilerParams(dimension_semantics=sem, vmem_limit_bytes=VMEM_LIMIT)


def _mod_body(c_ref, w_ref, b_ref, o_ref):
    c = c_ref[...]
    s = c * jax.nn.sigmoid(c)
    o_ref[...] = _dot(s.astype(BF16), w_ref[...].astype(BF16)) + b_ref[...]


def _mod_call(cc, w_mod, b_mod):
    n = w_mod.shape[-1]
    return pl.pallas_call(
        _mod_body,
        grid=(n // MOD_TN,),
        in_specs=[
            pl.BlockSpec((MOD_ROWS, D_MODEL), lambda j: (0, 0)),
            pl.BlockSpec((None, D_MODEL, MOD_TN), lambda j: (0, 0, j)),
            pl.BlockSpec((1, MOD_TN), lambda j: (0, j)),
        ],
        out_specs=pl.BlockSpec((MOD_ROWS, MOD_TN), lambda j: (0, j)),
        out_shape=jax.ShapeDtypeStruct((MOD_ROWS, n), F32),
        compiler_params=_params(("arbitrary",)),
        name="mod",
    )(cc, w_mod, b_mod)


def _rope128(t, c, s1, s2):
    return (t * c + pltpu.roll(t, HEAD_PAD - ROPE_HALF, 1) * s1
            + pltpu.roll(t, ROPE_HALF, 1) * s2)


def _proj_body(latent, xp_ref, x_ref, xn_ref, sh_ref, sc_ref, g1_ref, wa_ref, wb_ref, gkv_ref, wuk_ref,
               wuv_ref, cw_ref, cb_ref, c_ref, s1_ref, s2_ref, *rest):
    if latent:
        gq_ref, wuq_ref, k_ref, v_ref, xc_ref, q_ref, yb_ref = rest
    else:
        k_ref, v_ref, xc_ref = rest
    i = pl.program_id(0)
    tm = x_ref.shape[1]
    xe = jnp.concatenate([xp_ref[0], x_ref[0], xn_ref[0]], axis=0)
    hf = _rms(xe, g1_ref[...]) * (1.0 + sc_ref[...]) + sh_ref[...]
    keep_top = (i > 0).astype(F32)
    keep_bot = (i < pl.num_programs(0) - 1).astype(F32)
    h = hf[HALO:HALO + tm]
    he = jnp.concatenate([hf[:HALO] * keep_top, h, hf[HALO + tm:] * keep_bot], axis=0).astype(BF16)
    hb = h.astype(BF16)
    c, s1, s2 = c_ref[...], s1_ref[...], s2_ref[...]

    if latent:
        lat = _dot(hb, wa_ref[...])
        q_lat = lat[:, :IN_KV - IN_Q]
        lat = lat[:, IN_KV - IN_Q:]
    else:
        lat = _dot(hb, wa_ref[:, IN_KV:IN_A_END])
    kv_lat, kr_raw = lat[:, :IN_KR - IN_KV], lat[:, IN_KR - IN_KV:]
    xb = _dot(he, wb_ref[:, IN_XB:IN_YB])
    ext = tm + 2 * HALO
    cw = cw_ref[...]
    xc_ref[0] = (cb_ref[...] + pltpu.roll(xb, 2, 0)[HALO:HALO + tm] * cw[0:1]
                 + pltpu.roll(xb, 1, 0)[HALO:HALO + tm] * cw[1:2]
                 + xb[HALO:HALO + tm] * cw[2:3]
                 + pltpu.roll(xb, ext - 1, 0)[HALO:HALO + tm] * cw[3:4])
    kvn = _rms(kv_lat, gkv_ref[...]).astype(BF16)
    kk = _dot(kvn, wuk_ref[...])
    vt = _dot_nt(wuv_ref[...], kvn)
    ones = jnp.ones((VT_ROWS - V_HEAD_DIM, vt.shape[1]), F32)
    v_ref[0] = jnp.concatenate([piece for hd in range(N_HEADS)
                                for piece in (vt[hd * V_HEAD_DIM:(hd + 1) * V_HEAD_DIM], ones)],
                               axis=0).astype(BF16)
    if latent:
        qn = _rms(q_lat, gq_ref[...]).astype(BF16)
        qq = _dot(qn, wuq_ref[...])
        yb_ref[0] = _gelu_tanh(_dot(hb, wb_ref[:, IN_YB:IN_GL]))
    kr = _rope128(kr_raw, c, s1, s2)
    for hd in range(N_HEADS):
        sl = slice(hd * HEAD_PAD, (hd + 1) * HEAD_PAD)
        k_ref[0, :, sl] = (kk[:, sl] + kr).astype(BF16)
    if latent:
        scale = QK_DIM ** -0.5 * LOG2_E
        for hd in range(N_HEADS):
            sl = slice(hd * HEAD_PAD, (hd + 1) * HEAD_PAD)
            q_ref[0, :, sl] = (_rope128(qq[:, sl], c, s1, s2) * scale).astype(BF16)


def _proj_call(latent, x, mod, mod_row, g1, w, tabs):
    b, s, _ = x.shape
    tm = min(PROJ_TM, s)
    row = lambda width: pl.BlockSpec((1, tm, width), lambda i, j: (j, i, 0))
    row_of = (lambda i, j: j) if mod_row is None else (lambda i, j: mod_row)
    tab = pl.BlockSpec((tm, HEAD_PAD), lambda i, j: (i, 0))
    per = tm // HALO
    nblk = s // HALO
    prev = pl.BlockSpec((1, HALO, D_MODEL), lambda i, j: (j, jnp.maximum(i * per - 1, 0), 0))
    nxt = pl.BlockSpec((1, HALO, D_MODEL), lambda i, j: (j, jnp.minimum((i + 1) * per, nblk - 1), 0))
    ins = [x, x, x, mod, mod, g1, w['in_a'], w['in_b'], w['gkv'], w['uk'], w['uv'], w['conv_w'], w['conv_b'], *tabs]
    in_specs = ([prev, row(D_MODEL), nxt, _mod_spec(MOD_SHIFT1, row_of), _mod_spec(MOD_SCALE1, row_of)]
                + [_const_spec(a.shape) for a in ins[5:13]]
                + [tab] * 3)
    widths = [N_HEADS * HEAD_PAD, None, LRU_WIDTH]
    dtypes = [BF16, BF16, F32]
    if latent:
        extra = [w['gq'], w['uq']]
        ins += extra
        in_specs += [_const_spec(a.shape) for a in extra]
        widths += [N_HEADS * HEAD_PAD, LRU_WIDTH]
        dtypes += [BF16, F32]
    hv = N_HEADS * VT_ROWS
    out_specs = [row(wd) for wd in widths if wd is not None]
    out_shape = [jax.ShapeDtypeStruct((b, s, wd), dt) for wd, dt in zip(widths, dtypes) if wd is not None]
    out_specs.insert(1, pl.BlockSpec((1, hv, tm), lambda i, j: (j, 0, i)))
    out_shape.insert(1, jax.ShapeDtypeStruct((b, hv, s), BF16))
    return pl.pallas_call(
        functools.partial(_proj_body, latent),
        grid=(s // tm, b),
        in_specs=in_specs,
        out_specs=out_specs,
        out_shape=out_shape,
        compiler_params=_params(("parallel", "parallel")),
        name="proj_latent" if latent else "proj_ctx",
    )(*ins)


def _gelu_tanh(x):
    return 0.5 * x * (1.0 + jnp.tanh(0.7978845608028654 * (x + 0.044715 * (x * x * x))))


def _lru_body(latent, xc_ref, wg_ref, lam_ref, *rest):
    for kb in range(xc_ref.shape[2] // LANES):
        _lru_block(latent, kb, xc_ref, wg_ref, lam_ref, *rest)


def _lru_block(latent, kb, xc_ref, wg_ref, lam_ref, *rest):
    if latent:
        yb_ref, h0_ref, out_ref, gates, scans = rest
    else:
        st_ref, gates, scans = rest
    s = xc_ref.shape[1]
    lanes = slice(kb * LANES, (kb + 1) * LANES)
    af, uf, ab, ub = (gates.at[pl.ds(k * s, s)] for k in range(4))
    hfl, pfl, hbl, pbl = (scans.at[pl.ds(k * s, s)] for k in range(4))
    seg = s // SUBLANES
    tc = min(LRU_TC, s)
    nc = s // tc
    piece = min(tc, seg)

    wg = wg_ref[kb]
    lam = lam_ref[kb]
    hcsp = (0.5 * LRU_C) * (jnp.maximum(-lam, 0.0) + jnp.log1p(jnp.exp(-jnp.abs(lam))))
    bias_lhs = jnp.where(lax.broadcasted_iota(jnp.int32, (tc, LANES), 1) < GATE_BIAS_ROWS, 1.0, 0.0).astype(BF16)

    def interleaved(ci):
        pieces = []
        for k in range(tc // piece):
            t0 = ci * tc + k * piece
            j = t0 // seg
            pieces.append((slice(k * piece, (k + 1) * piece),
                           pl.ds((t0 - j * seg) * SUBLANES + j, piece, stride=SUBLANES)))
        return pieces

    def gate_chunk(ci, carry):
        r0 = pl.multiple_of(ci * tc, tc)
        xc = xc_ref[0, pl.ds(r0, tc), lanes]
        t = jnp.tanh(_dot(jnp.concatenate([xc.astype(BF16), bias_lhs], axis=1), wg))
        hx = 0.5 * xc
        for d, (a_ref, u_ref) in enumerate(((af, uf), (ab, ub))):
            t_r = t[:, (2 * d) * LANES:(2 * d + 1) * LANES]
            t_i = t[:, (2 * d + 1) * LANES:(2 * d + 2) * LANES]
            neg_log_a = hcsp[d:d + 1] + hcsp[d:d + 1] * t_r
            a = jnp.exp2(neg_log_a * (-LOG2_E))
            y = (1.0 - a) * (1.0 + a)
            mult = y * lax.rsqrt(jnp.maximum(y, TINY))
            u = (mult * hx) * (1.0 + t_i)
            for rows, rd in interleaved(ci):
                a_ref[rd, :] = a[rows]
                u_ref[rd, :] = u[rows]
        return carry

    lax.fori_loop(0, nc, gate_chunk, 0, unroll=min(16, nc))

    def scan_step(i, carry):
        hf, pf, hb, pb = carry
        fwd = pl.ds(pl.multiple_of(i * SUBLANES, SUBLANES), SUBLANES)
        bwd = pl.ds(pl.multiple_of((seg - 1 - i) * SUBLANES, SUBLANES), SUBLANES)
        a = af[fwd, :]
        hf = a * hf + uf[fwd, :]
        pf = a * pf
        hfl[fwd, :] = hf
        pfl[fwd, :] = pf
        a = ab[bwd, :]
        hb = a * hb + ub[bwd, :]
        pb = a * pb
        hbl[bwd, :] = hb
        pbl[bwd, :] = pb
        return hf, pf, hb, pb

    zeros = jnp.zeros((SUBLANES, LANES), F32)
    ones = jnp.ones((SUBLANES, LANES), F32)
    hf, pf, hb, pb = lax.fori_loop(0, seg, scan_step, (zeros, ones, zeros, ones), unroll=LRU_UNROLL)

    if latent:
        h0 = h0_ref[0, :, lanes]
        cf, cbk = h0[0:1], h0[1:2]
    else:
        cf = cbk = jnp.zeros((1, LANES), F32)
    cfs = []
    for j in range(SUBLANES):
        cfs.append(cf)
        cf = hf[j:j + 1] + pf[j:j + 1] * cf
    cbs = [None] * SUBLANES
    for j in reversed(range(SUBLANES)):
        cbs[j] = cbk
        cbk = hb[j:j + 1] + pb[j:j + 1] * cbk

    if latent:
        fix_rows = min(LRU_FIX_ROWS, s)
        cf_all = jnp.tile(jnp.concatenate(cfs, axis=0), (fix_rows // SUBLANES, 1))
        cb_all = jnp.tile(jnp.concatenate(cbs, axis=0), (fix_rows // SUBLANES, 1))

        def fix_chunk(ci, carry):
            rows = pl.ds(pl.multiple_of(ci * fix_rows, fix_rows), fix_rows)
            af[rows, :] = (hfl[rows, :] + pfl[rows, :] * cf_all) + (hbl[rows, :] + pbl[rows, :] * cb_all)
            return carry

        lax.fori_loop(0, s // fix_rows, fix_chunk, 0, unroll=4)

        def out_chunk(ci, carry):
            r0 = pl.multiple_of(ci * tc, tc)
            hsum = jnp.concatenate([af[rd, :] for _, rd in interleaved(ci)], axis=0)
            out_ref[0, pl.ds(r0, tc), lanes] = (hsum * yb_ref[0, pl.ds(r0, tc), lanes]).astype(BF16)
            return carry

        lax.fori_loop(0, nc, out_chunk, 0)
    else:
        st_ref[0, :, lanes] = jnp.concatenate([cf, cbk], axis=0)


def _lru_call(latent, xc, w, yb=None, h0=None):
    b, s, _ = xc.shape
    nb = 1 if latent else LRU_BLOCKS
    seq = pl.BlockSpec((1, s, nb * LANES), lambda i, j: (i, 0, j))
    st = pl.BlockSpec((1, 2, nb * LANES), lambda i, j: (i, 0, j))
    in_specs = [
        seq,
        pl.BlockSpec((nb, 2 * LANES, 4 * LANES), lambda i, j: (j, 0, 0)),
        pl.BlockSpec((nb, 2, LANES), lambda i, j: (j, 0, 0)),
    ]
    ins = [xc, w['gate_w'], w['lam']]
    scratch = [pltpu.VMEM((4 * s, LANES), F32)] * 2
    if latent:
        ins += [yb, h0]
        in_specs += [seq, st]
        out_specs = seq
        out_shape = jax.ShapeDtypeStruct((b, s, LRU_WIDTH), BF16)
    else:
        out_specs = st
        out_shape = jax.ShapeDtypeStruct((b, 2, LRU_WIDTH), F32)
    return pl.pallas_call(
        functools.partial(_lru_body, latent),
        grid=(b, LRU_BLOCKS // nb),
        in_specs=in_specs,
        out_specs=out_specs,
        out_shape=out_shape,
        scratch_shapes=scratch,
        compiler_params=_params(("parallel", "parallel")),
        name="lru_latent" if latent else "lru_ctx",
    )(*ins)


def _col_groups(x, op):
    rows, cols = x.shape
    return op(x.reshape(rows // SUBLANES, SUBLANES, cols), axis=0)


def _attn_body(q_ref, kl_ref, kc_ref, vl_ref, vc_ref, o_ref, s0_ref, s1_ref, p0_ref, p1_ref, o0_ref):
    s_len = kl_ref.shape[1]
    c_len = kc_ref.shape[1]
    tk = min(ATTN_TK, s_len)
    sub = s0_ref.shape[1]
    n_sub = q_ref.shape[1] // sub
    chunks = [(kc_ref, 0, c_len, 0)] + [(kl_ref, ki * tk, tk, c_len + ki * tk) for ki in range(s_len // tk)]
    s_refs, p_refs = (s0_ref, s1_ref), (p0_ref, p1_ref)

    def qrows(qt):
        return pl.ds(pl.multiple_of(qt * sub, sub), sub)

    def score_pass(qt, hd):
        lanes = slice(hd * HEAD_PAD, (hd + 1) * HEAD_PAD)
        q = q_ref[0, qrows(qt), lanes]
        m8 = None
        for k_ref, r0, rows, off in chunks:
            sk = _dot_nt(k_ref[0, r0:r0 + rows, lanes], q)
            s_refs[hd][off:off + rows, :] = sk
            g = _col_groups(sk, jnp.max)
            m8 = g if m8 is None else jnp.maximum(m8, g)
        return jnp.max(m8, axis=0, keepdims=True)

    def exp_pass(hd, m):
        for _, _, rows, off in chunks:
            p_refs[hd][off:off + rows, :] = jnp.exp2(s_refs[hd][off:off + rows, :] - m).astype(BF16)

    def value_pass(hd):
        vrows = slice(hd * VT_ROWS, (hd + 1) * VT_ROWS)
        acc = (_dot(vc_ref[0, vrows, :], p_refs[hd][0:c_len, :])
               + _dot(vl_ref[0, vrows, :], p_refs[hd][c_len:c_len + s_len, :]))
        return acc[:V_HEAD_DIM] / acc[V_HEAD_DIM:V_HEAD_DIM + 1]

    def store(qt, o0, o1):
        o_ref[0, qrows(qt), :] = jnp.concatenate([o0, o1], axis=0).T.astype(BF16)

    def finish(hd, m):
        exp_pass(hd, m)
        return value_pass(hd)

    def head1_trip(qt, m_prev):
        o0_ref[...] = finish(0, m_prev)
        return score_pass(qt, 1)

    def head0_trip(qt, m_prev):
        store(qt - 1, o0_ref[...], finish(1, m_prev))
        return score_pass(qt, 0)

    def trip(u, m_prev):
        return lax.cond(u % 2 == 1, head1_trip, head0_trip, u // 2, m_prev)

    m = lax.fori_loop(1, 2 * n_sub, trip, score_pass(0, 0))
    store(n_sub - 1, o0_ref[...], finish(1, m))


def _attn_call(q, k_l, k_c, vt_l, vt_c):
    b, s, _ = q.shape
    c_len = k_c.shape[1]
    tq = min(ATTN_TQ, s)
    sub = min(ATTN_SUB, tq)
    pairs = N_HEADS // 2
    return pl.pallas_call(
        _attn_body,
        grid=(b, pairs, s // tq),
        in_specs=[
            pl.BlockSpec((1, tq, 2 * HEAD_PAD), lambda i, j, t: (i, t, j)),
            pl.BlockSpec((1, s, 2 * HEAD_PAD), lambda i, j, t: (i, 0, j)),
            pl.BlockSpec((1, c_len, 2 * HEAD_PAD), lambda i, j, t: (i, 0, j)),
            pl.BlockSpec((1, 2 * VT_ROWS, s), lambda i, j, t: (i, j, 0)),
            pl.BlockSpec((1, 2 * VT_ROWS, c_len), lambda i, j, t: (i, j, 0)),
        ],
        out_specs=pl.BlockSpec((1, tq, 2 * V_HEAD_DIM), lambda i, j, t: (i, t, j)),
        out_shape=jax.ShapeDtypeStruct((b, s, N_HEADS * V_HEAD_DIM), BF16),
        scratch_shapes=([pltpu.VMEM((c_len + s, sub), F32)] * 2 + [pltpu.VMEM((c_len + s, sub), BF16)] * 2
                        + [pltpu.VMEM((V_HEAD_DIM, sub), F32)]),
        compiler_params=_params(("parallel", "parallel", "parallel")),
        name="attn",
    )(q, k_l, k_c, vt_l, vt_c)


def _merge_body(attn_ref, m_ref, x_ref, sh_ref, sc_ref, g1_ref, n1_ref, bg_ref, wb_ref, woa_ref, wol_ref,
                wout_ref, o_ref):
    tm = x_ref.shape[1]
    sub = tm // MERGE_SUBTILES
    rows = [slice(i * sub, (i + 1) * sub) for i in range(MERGE_SUBTILES)]
    ys = []
    for r in rows:
        h = _rms(x_ref[0, r, :], n1_ref[...]) * (1.0 + sc_ref[...]) + sh_ref[...]
        ys.append((_dot(attn_ref[0, r, :], woa_ref[...]), _dot(m_ref[0, r, :], wol_ref[...]),
                   _dot(h.astype(BF16), wb_ref[:, IN_GL:IN_END])))
    for r, (y_a, y_b, gl) in zip(rows, ys):
        gates = 0.5 + 0.5 * jnp.tanh(0.5 * (gl + bg_ref[...]))
        mix = gates[:, :D_MODEL] * y_a + gates[:, D_MODEL:] * y_b
        o_ref[0, r, :] = x_ref[0, r, :] + g1_ref[...] * _dot(mix.astype(BF16), wout_ref[...])


def _merge_call(attn, m, x, mod, g1n, w):
    b, s, _ = x.shape
    tm = min(MERGE_TM, s)
    row = lambda width: pl.BlockSpec((1, tm, width), lambda i, j: (i, j, 0))
    batch_row = lambda i, j: i
    consts = [g1n, w['b_gate'], w['in_b'], w['o_attn'], w['o_lru'], w['out']]
    return pl.pallas_call(
        _merge_body,
        grid=(b, s // tm),
        in_specs=[row(N_HEADS * V_HEAD_DIM), row(LRU_WIDTH), row(D_MODEL), _mod_spec(MOD_SHIFT1, batch_row),
                  _mod_spec(MOD_SCALE1, batch_row), _mod_spec(MOD_GATE1, batch_row)]
        + [_const_spec(a.shape) for a in consts],
        out_specs=row(D_MODEL),
        out_shape=jax.ShapeDtypeStruct((b, s, D_MODEL), F32),
        compiler_params=_params(("parallel", "parallel")),
        name="merge",
    )(attn, m, x, mod, mod, mod, *consts)


def _ffn_body(xp_ref, x_ref, xn_ref, sh_ref, sc_ref, g2_ref, n2_ref, fg_ref, wup_ref, cw_ref, cb_ref,
              wdn_ref, o_ref, f_ref):
    j = pl.program_id(1)
    nj = pl.num_programs(1)
    tm = x_ref.shape[1]
    x = x_ref[0]
    xe = jnp.concatenate([xp_ref[0], x, xn_ref[0]], axis=0)
    hf = _rms(xe, n2_ref[...]) * (1.0 + sc_ref[...]) + sh_ref[...]
    keep_top = (j > 0).astype(F32)
    keep_bot = (j < nj - 1).astype(F32)
    ht = hf[HALO:HALO + tm]
    he = jnp.concatenate([hf[:HALO] * keep_top, ht, hf[HALO + tm:] * keep_bot], axis=0).astype(BF16)
    ht = ht.astype(BF16)
    ext = tm + 2 * HALO
    for ci in range(FFN_DIM // FFN_FC):
        cols = slice(ci * FFN_FC, (ci + 1) * FFN_FC)
        gcols = slice(FFN_DIM + ci * FFN_FC, FFN_DIM + (ci + 1) * FFN_FC)
        a = _dot(he, wup_ref[:, cols])
        cw = cw_ref[:, cols]
        conv = (cb_ref[:, cols] + pltpu.roll(a, 1, 0)[HALO:HALO + tm] * cw[0:1]
                + a[HALO:HALO + tm] * cw[1:2]
                + pltpu.roll(a, ext - 1, 0)[HALO:HALO + tm] * cw[2:3])
        g = _dot(ht, wup_ref[:, gcols])
        f_ref[:, cols] = (conv * jax.nn.sigmoid(conv) * g).astype(BF16)
    y = x + g2_ref[...] * _dot(f_ref[...], wdn_ref[...])
    o_ref[0] = _rms(y, fg_ref[...])


def _ffn_call(x1, mod, w):
    b, s, _ = x1.shape
    tm = min(FFN_TM, s)
    per = tm // HALO
    nblk = s // HALO
    row = pl.BlockSpec((1, tm, D_MODEL), lambda i, j: (i, j, 0))
    prev = pl.BlockSpec((1, HALO, D_MODEL), lambda i, j: (i, jnp.maximum(j * per - 1, 0), 0))
    nxt = pl.BlockSpec((1, HALO, D_MODEL), lambda i, j: (i, jnp.minimum((j + 1) * per, nblk - 1), 0))
    batch_row = lambda i, j: i
    consts = [w['norm2_g'], w['final_g'], w['up'], w['ffn_conv_w'], w['ffn_conv_b'], w['down']]
    return pl.pallas_call(
        _ffn_body,
        grid=(b, s // tm),
        in_specs=[prev, row, nxt, _mod_spec(MOD_SHIFT2, batch_row), _mod_spec(MOD_SCALE2, batch_row),
                  _mod_spec(MOD_GATE2, batch_row)] + [_const_spec(a.shape) for a in consts],
        out_specs=row,
        out_shape=jax.ShapeDtypeStruct((b, s, D_MODEL), F32),
        scratch_shapes=[pltpu.VMEM((tm, FFN_DIM), BF16)],
        compiler_params=_params(("parallel", "parallel")),
        name="ffn",
    )(x1, x1, x1, mod, mod, mod, *consts)


def _rope_tables(n):
    rows = n // GRID_W
    row_ids = jnp.repeat(jnp.arange(rows), GRID_W).astype(F32)
    col_ids = jnp.tile(jnp.arange(GRID_W), rows).astype(F32)
    axis_dim = QK_ROPE_DIM // 2
    inv = 1.0 / (ROPE_BASE ** (jnp.arange(0, axis_dim, 2, dtype=F32) / axis_dim))
    ang = jnp.concatenate([row_ids[:, None] * inv, col_ids[:, None] * inv], axis=-1)
    cos, sin = jnp.cos(ang), jnp.sin(ang)
    ones = lambda w_: jnp.ones((n, w_), F32)
    zeros = lambda w_: jnp.zeros((n, w_), F32)
    tail = HEAD_PAD - QK_DIM
    c = jnp.concatenate([ones(QK_NOPE_DIM), cos, cos, ones(tail)], axis=-1)
    s1 = jnp.concatenate([zeros(QK_NOPE_DIM), -sin, zeros(ROPE_HALF + tail)], axis=-1)
    s2 = jnp.concatenate([zeros(QK_NOPE_DIM + ROPE_HALF), sin, zeros(tail)], axis=-1)
    return c, s1, s2


def _identity_tables(n):
    return (jnp.ones((n, HEAD_PAD), F32), jnp.zeros((n, HEAD_PAD), F32), jnp.zeros((n, HEAD_PAD), F32))


def _prep_weights(w_in, q_norm_g, kv_norm_g, w_uq, w_ukv, w_o_attn, lru_conv_w, lru_conv_b, lru_w_a,
                  lru_b_a, lru_w_x, lru_b_x, lru_lambda, w_o_lru, w_out, b_gate, norm2_g, w_up,
                  ffn_conv_w, ffn_conv_b, w_down, final_g):
    w = {}
    kr = jnp.pad(w_in[:, OFF_KR:OFF_XB], ((0, 0), (QK_NOPE_DIM, HEAD_PAD - QK_DIM)))
    w['in_a'] = jnp.concatenate([w_in[:, :OFF_KR], kr], axis=1).astype(BF16)
    w['in_b'] = w_in[:, OFF_XB:].astype(BF16)
    w['gq'] = q_norm_g[None, :]
    w['gkv'] = kv_norm_g[None, :]
    uq = w_uq.reshape(Q_LORA_RANK, N_HEADS, QK_DIM)
    w['uq'] = jnp.pad(uq, ((0, 0), (0, 0), (0, HEAD_PAD - QK_DIM))).reshape(Q_LORA_RANK, -1).astype(BF16)
    ukv = w_ukv.reshape(KV_LORA_RANK, N_HEADS, QK_NOPE_DIM + V_HEAD_DIM)
    w['uk'] = jnp.pad(ukv[..., :QK_NOPE_DIM],
                      ((0, 0), (0, 0), (0, HEAD_PAD - QK_NOPE_DIM))).reshape(KV_LORA_RANK, -1).astype(BF16)
    w['uv'] = ukv[..., QK_NOPE_DIM:].reshape(KV_LORA_RANK, -1).T.astype(BF16)
    w['conv_w'] = lru_conv_w
    w['conv_b'] = lru_conv_b[None, :]
    gw = 0.5 * jnp.concatenate([lru_w_a[0], lru_w_x[0], lru_w_a[1], lru_w_x[1]], axis=-1)
    gb = 0.5 * jnp.stack([lru_b_a[0], lru_b_x[0], lru_b_a[1], lru_b_x[1]], axis=0)
    gb = gb.reshape(4, LRU_BLOCKS, LRU_BLOCK_W).transpose(1, 0, 2).reshape(LRU_BLOCKS, 1, -1)
    terms = []
    for _ in range(GATE_BIAS_ROWS):
        t = gb * VELTKAMP_8BIT
        hi = t - (t - gb)
        terms.append(hi)
        gb = gb - hi
    bias_rows = jnp.pad(jnp.concatenate(terms, axis=1), ((0, 0), (0, LANES - GATE_BIAS_ROWS), (0, 0)))
    w['gate_w'] = jnp.concatenate([gw, bias_rows], axis=1).astype(BF16)
    w['lam'] = lru_lambda.reshape(2, LRU_BLOCKS, LRU_BLOCK_W).transpose(1, 0, 2)
    w['o_attn'] = w_o_attn.astype(BF16)
    w['o_lru'] = w_o_lru.astype(BF16)
    w['out'] = w_out.astype(BF16)
    w['b_gate'] = b_gate[None, :]
    w['norm2_g'] = norm2_g[None, :]
    w['final_g'] = final_g[None, :]
    w['up'] = w_up.astype(BF16)
    w['ffn_conv_w'] = ffn_conv_w
    w['ffn_conv_b'] = ffn_conv_b[None, :]
    w['down'] = w_down.astype(BF16)
    return w


def kernel(x, c, ctx, c_ctx, w_mod, b_mod, norm1_g, w_in, b_gate, q_norm_g, kv_norm_g, w_uq, w_ukv,
           w_o_attn, lru_conv_w, lru_conv_b, lru_w_a, lru_b_a, lru_w_x, lru_b_x, lru_lambda, w_o_lru,
           w_out, norm2_g, w_up, ffn_conv_w, ffn_conv_b, w_down, final_g):
    assert w_mod.shape[0] == 1, "single-layer block"
    b, s, _ = x.shape
    c_len = ctx.shape[1]
    assert b + 1 <= MOD_ROWS

    w = _prep_weights(w_in[0], q_norm_g[0], kv_norm_g[0], w_uq[0], w_ukv[0], w_o_attn[0], lru_conv_w[0],
                      lru_conv_b[0], lru_w_a[0], lru_b_a[0], lru_w_x[0], lru_b_x[0], lru_lambda[0],
                      w_o_lru[0], w_out[0], b_gate[0], norm2_g[0], w_up[0], ffn_conv_w[0],
                      ffn_conv_b[0], w_down[0], final_g)
    g1n = norm1_g[0][None, :]

    cc = jnp.concatenate([c, c_ctx[None, :], jnp.zeros((MOD_ROWS - b - 1, D_MODEL), F32)], axis=0)
    mod = _mod_call(cc, w_mod, b_mod).reshape(MOD_ROWS, 6, 1, D_MODEL)

    k_c, vt_c, xc_c = _proj_call(False, ctx, mod, b, g1n, w, _identity_tables(c_len))
    states = _lru_call(False, xc_c, w)

    k_l, vt_l, xc_l, q_l, gy_l = _proj_call(True, x, mod, None, g1n, w, _rope_tables(s))
    m = _lru_call(True, xc_l, w, yb=gy_l, h0=states)
    attn = _attn_call(q_l, k_l, k_c, vt_l, vt_c)
    x1 = _merge_call(attn, m, x, mod, g1n, w)
    return _ffn_call(x1, mod, w)
```
